```python
import math
import jax, jax.numpy as jnp
from jax import lax
import numpy as np

D_MODEL = 2048
BATCH = 4
SEQ = 4096
DEPTH = 2
DEC_BATCH = 8
DEC_SEQ = 32
PAST_LEN = 2048

CHUNK = 64
D_MIX = D_MODEL
HEAD_DIM = 64
N_Q_HEADS = (D_MIX // 2) // HEAD_DIM
N_KV_HEADS = 4
WINDOW = 128
N_WIN_CHUNKS = WINDOW // CHUNK
ROPE_THETA = 10000.0
RET_HEAD_DIM = 128
N_RET_HEADS = (D_MIX // 2) // RET_HEAD_DIM
D_ATTN = N_Q_HEADS * HEAD_DIM
D_KV = N_KV_HEADS * HEAD_DIM
D_RET = N_RET_HEADS * RET_HEAD_DIM
D_IN = D_ATTN + 2 * D_KV + 4 * D_RET
D_FF = 5504
N_EXPERTS = 8
TOP_K = 2
D_FF_EXPERT = 7168
N_DENSE = (DEPTH + 1) // 2
N_MOE = DEPTH // 2
RMS_EPS = 1e-6
GN_EPS = 1e-5

kernel_name = 'chunk_causal_hymba_swa_retention_moe_step'


def rms_norm(x, g):
    xf = x.astype(jnp.float32)
    y = xf * lax.rsqrt(jnp.mean(xf * xf, axis=-1, keepdims=True) + RMS_EPS)
    return (y * g.astype(jnp.float32)).astype(x.dtype)


def rope(x, pos, inv_freq):
    ang = pos.astype(jnp.float32)[:, None] * inv_freq[None, :]
    cos = jnp.cos(ang)[:, None, :]
    sin = jnp.sin(ang)[:, None, :]
    x1, x2 = jnp.split(x.astype(jnp.float32), 2, axis=-1)
    return jnp.concatenate([x1 * cos - x2 * sin, x2 * cos + x1 * sin], axis=-1).astype(x.dtype)


def project(xn, pos, w_in, q_g, k_g):
    B, S, _ = xn.shape
    p = xn @ w_in
    c1 = D_ATTN
    c2 = c1 + D_KV
    c3 = c2 + D_KV
    c4 = c3 + D_RET
    c5 = c4 + D_RET
    c6 = c5 + D_RET
    qa, ka, va, qr, kr, vr, gate = jnp.split(p, [c1, c2, c3, c4, c5, c6], axis=-1)
    inv_a = ROPE_THETA ** (-jnp.arange(0, HEAD_DIM, 2, dtype=jnp.float32) / HEAD_DIM)
    qa = rope(rms_norm(qa.reshape(B, S, N_Q_HEADS, HEAD_DIM), q_g), pos, inv_a)
    ka = rope(rms_norm(ka.reshape(B, S, N_KV_HEADS, HEAD_DIM), k_g), pos, inv_a)
    va = va.reshape(B, S, N_KV_HEADS, HEAD_DIM)
    inv_r = 10000.0 ** (-jnp.linspace(0.0, 1.0, RET_HEAD_DIM // 2, dtype=jnp.float32))
    qr = rope(qr.reshape(B, S, N_RET_HEADS, RET_HEAD_DIM), pos, inv_r)
    kr = rope(kr.reshape(B, S, N_RET_HEADS, RET_HEAD_DIM), pos, inv_r) * (RET_HEAD_DIM ** -0.5)
    vr = vr.reshape(B, S, N_RET_HEADS, RET_HEAD_DIM)
    return qa, ka, va, qr, kr, vr, gate


def sink_softmax(s, sinks):
    sk = sinks.astype(jnp.float32).reshape(N_KV_HEADS, -1)[:, :, None, None]
    m = jnp.maximum(jnp.max(s, axis=-1, keepdims=True), sk)
    p = jnp.exp(s - m)
    return p / (jnp.sum(p, axis=-1, keepdims=True) + jnp.exp(sk - m))


def swa_prompt(q, k, v, sinks):
    B, S, Hq, d = q.shape
    nc = S // CHUNK
    G = Hq // N_KV_HEADS
    pad = N_WIN_CHUNKS * CHUNK
    kp = jnp.pad(k, ((0, 0), (pad, 0), (0, 0), (0, 0))).reshape(B, nc + N_WIN_CHUNKS, CHUNK, N_KV_HEADS, d)
    vp = jnp.pad(v, ((0, 0), (pad, 0), (0, 0), (0, 0))).reshape(B, nc + N_WIN_CHUNKS, CHUNK, N_KV_HEADS, d)
    kb = jnp.concatenate([kp[:, i:i + nc] for i in range(N_WIN_CHUNKS + 1)], axis=2)
    vb = jnp.concatenate([vp[:, i:i + nc] for i in range(N_WIN_CHUNKS + 1)], axis=2)
    L = kb.shape[2]
    qb = q.reshape(B, nc, CHUNK, N_KV_HEADS, G, d)
    s = jnp.einsum('bcqhgd,bckhd->bchgqk', qb, kb).astype(jnp.float32) * (d ** -0.5)
    key_pos = (jnp.arange(nc)[:, None] - N_WIN_CHUNKS) * CHUNK + jnp.arange(L)[None, :]
    s = jnp.where((key_pos >= 0)[None, :, None, None, None, :], s, -jnp.inf)
    p = sink_softmax(s, sinks)
    o = jnp.einsum('bchgqk,bckhd->bcqhgd', p.astype(v.dtype), vb)
    return o.reshape(B, S, Hq * d)


def swa_sample(q, k_new, v_new, ck, cv, sinks):
    B, T, Hq, d = q.shape
    G = Hq // N_KV_HEADS
    k = jnp.concatenate([ck.astype(k_new.dtype), k_new], axis=1)
    v = jnp.concatenate([cv.astype(v_new.dtype), v_new], axis=1)
    qg = q.reshape(B, T, N_KV_HEADS, G, d)
    s = jnp.einsum('bqhgd,bkhd->bhgqk', qg, k).astype(jnp.float32) * (d ** -0.5)
    p = sink_softmax(s, sinks)
    o = jnp.einsum('bhgqk,bkhd->bqhgd', p.astype(v.dtype), v)
    return o.reshape(B, T, Hq * d)


def retention(q, k, v, R0, chunk):
    B, S, H, d = q.shape
    nc = S // chunk
    log_g = jnp.log1p(-jnp.exp2(-5.0 - jnp.arange(H, dtype=jnp.float32)))
    idx = jnp.arange(chunk, dtype=jnp.float32)
    D = jnp.exp(log_g[:, None, None] * jnp.abs(idx[:, None] - idx[None, :]))
    dec_in = jnp.exp(log_g[None, :] * (idx[:, None] + 1.0))
    dec_kv = jnp.exp(log_g[None, :] * (chunk - 1.0 - idx[:, None]))
    dec_chunk = jnp.exp(log_g * chunk)
    qc = q.astype(jnp.float32).reshape(B, nc, chunk, H, d)
    kc = k.astype(jnp.float32).reshape(B, nc, chunk, H, d)
    vc = v.astype(jnp.float32).reshape(B, nc, chunk, H, d)
    s = jnp.einsum('bcihd,bcjhd->bchij', qc, kc) * D
    o_intra = jnp.einsum('bchij,bcjhd->bcihd', s, vc)
    kv = jnp.einsum('bcjhd,bcjhe->bchde', kc * dec_kv[:, :, None], vc)

    def step(R, kv_c):
        return R * dec_chunk[None, :, None, None] + kv_c, R

    R_final, R_prev = lax.scan(step, R0.astype(jnp.float32), jnp.moveaxis(kv, 1, 0))
    R_prev = jnp.moveaxis(R_prev, 0, 1)
    o_inter = jnp.einsum('bcihd,bchde->bcihe', qc, R_prev) * dec_in[:, :, None]
    return (o_intra + o_inter).reshape(B, S, H, d), R_final


def merge(ao, ro, gate, gn_g, gn_b, w_out):
    B, S, H, d = ro.shape
    mu = jnp.mean(ro, axis=-1, keepdims=True)
    var = jnp.mean(jnp.square(ro - mu), axis=-1, keepdims=True)
    r = ((ro - mu) * lax.rsqrt(var + GN_EPS)).reshape(B, S, H * d)
    r = r * gn_g.astype(jnp.float32) + gn_b.astype(jnp.float32)
    r = jax.nn.silu(gate.astype(jnp.float32)) * r
    return jnp.concatenate([ao, r.astype(ao.dtype)], axis=-1) @ w_out


def swiglu(x, wg, wu, wd):
    return (jax.nn.silu(x @ wg) * (x @ wu)) @ wd


def moe_ffn(x, w_router, wg, wu, wd):
    shp = x.shape
    xf = x.reshape(-1, shp[-1])
    logits = (xf @ w_router).astype(jnp.float32)
    top_v, top_i = lax.top_k(logits, TOP_K)
    w = jax.nn.softmax(top_v, axis=-1)
    gates = jnp.sum(jax.nn.one_hot(top_i, N_EXPERTS, dtype=jnp.float32) * w[..., None], axis=1)
    out = jnp.zeros_like(xf)
    for e in range(N_EXPERTS):
        out = out + gates[:, e:e + 1].astype(xf.dtype) * swiglu(xf, wg[e], wu[e], wd[e])
    return out.reshape(shp)


def setup_inputs(seed: int = 0) -> dict:
    key = jax.random.key(seed)
    ks = jax.random.split(key, 24)
    f32 = jnp.float32
    n = lambda k, s, sc: jax.random.normal(k, s, f32) * sc
    win_rows = min(WINDOW, PAST_LEN)
    return {
        'x_prompt': n(ks[0], (BATCH, SEQ, D_MODEL), 1.0),
        'x_sample': n(ks[1], (DEC_BATCH, DEC_SEQ, D_MODEL), 1.0),
        'cache_attn_k': n(ks[2], (DEPTH, DEC_BATCH, win_rows, N_KV_HEADS, HEAD_DIM), 1.0),
        'cache_attn_v': n(ks[3], (DEPTH, DEC_BATCH, win_rows, N_KV_HEADS, HEAD_DIM), 1.0),
        'state_ret': n(ks[4], (DEPTH, DEC_BATCH, N_RET_HEADS, RET_HEAD_DIM, RET_HEAD_DIM), 0.5),
        'ln_mix': 1.0 + n(ks[5], (DEPTH, D_MODEL), 0.02),
        'w_in': n(ks[6], (DEPTH, D_MODEL, D_IN), D_MODEL ** -0.5),
        'q_norm': 1.0 + n(ks[7], (DEPTH, HEAD_DIM), 0.02),
        'k_norm': 1.0 + n(ks[8], (DEPTH, HEAD_DIM), 0.02),
        'attn_sinks': n(ks[9], (DEPTH, N_Q_HEADS), 0.5),
        'ret_gn_g': 1.0 + n(ks[10], (DEPTH, D_RET), 0.02),
        'ret_gn_b': n(ks[11], (DEPTH, D_RET), 0.02),
        'w_out': n(ks[12], (DEPTH, D_ATTN + D_RET, D_MODEL), (D_ATTN + D_RET) ** -0.5),
        'ln_ffn': 1.0 + n(ks[13], (DEPTH, D_MODEL), 0.02),
        'ffn_w_gate': n(ks[14], (N_DENSE, D_MODEL, D_FF), D_MODEL ** -0.5),
        'ffn_w_up': n(ks[15], (N_DENSE, D_MODEL, D_FF), D_MODEL ** -0.5),
        'ffn_w_down': n(ks[16], (N_DENSE, D_FF, D_MODEL), D_FF ** -0.5),
        'moe_router': n(ks[17], (N_MOE, D_MODEL, N_EXPERTS), D_MODEL ** -0.5),
        'moe_w_gate': n(ks[18], (N_MOE, N_EXPERTS, D_MODEL, D_FF_EXPERT), D_MODEL ** -0.5),
        'moe_w_up': n(ks[19], (N_MOE, N_EXPERTS, D_MODEL, D_FF_EXPERT), D_MODEL ** -0.5),
        'moe_w_down': n(ks[20], (N_MOE, N_EXPERTS, D_FF_EXPERT, D_MODEL), D_FF_EXPERT ** -0.5),
    }


def reference(x_prompt, x_sample, cache_attn_k, cache_attn_v, state_ret,
              ln_mix, w_in, q_norm, k_norm, attn_sinks, ret_gn_g, ret_gn_b, w_out, ln_ffn,
              ffn_w_gate, ffn_w_up, ffn_w_down, moe_router, moe_w_gate, moe_w_up, moe_w_down):
    B, S, _ = x_prompt.shape
    DB, T, _ = x_sample.shape
    pos_p = jnp.arange(S)
    pos_s = PAST_LEN + jnp.arange(T)
    keep_p = min(WINDOW, S)
    xp, xs = x_prompt, x_sample
    nk_p, nv_p, nr_p, nk_s, nv_s, nr_s = [], [], [], [], [], []

    def channel_mix(l, h):
        hn = rms_norm(h, ln_ffn[l])
        if l % 2 == 0:
            i = l // 2
            return swiglu(hn, ffn_w_gate[i], ffn_w_up[i], ffn_w_down[i])
        i = l // 2
        return moe_ffn(hn, moe_router[i], moe_w_gate[i], moe_w_up[i], moe_w_down[i])

    for l in range(DEPTH):
        xn = rms_norm(xp, ln_mix[l])
        qa, ka, va, qr, kr, vr, gate = project(xn, pos_p, w_in[l], q_norm[l], k_norm[l])
        ao = swa_prompt(qa, ka, va, attn_sinks[l])
        ro, Rp = retention(qr, kr, vr, jnp.zeros((B, N_RET_HEADS, RET_HEAD_DIM, RET_HEAD_DIM), jnp.float32), CHUNK)
        xp = xp + merge(ao, ro, gate, ret_gn_g[l], ret_gn_b[l], w_out[l])
        xp = xp + channel_mix(l, xp)
        nk_p.append(ka[:, S - keep_p:])
        nv_p.append(va[:, S - keep_p:])
        nr_p.append(Rp)
        xn = rms_norm(xs, ln_mix[l])
        qa, ka, va, qr, kr, vr, gate = project(xn, pos_s, w_in[l], q_norm[l], k_norm[l])
        ao = swa_sample(qa, ka, va, cache_attn_k[l], cache_attn_v[l], attn_sinks[l])
        ro, Rs = retention(qr, kr, vr, state_ret[l], T)
        xs = xs + merge(ao, ro, gate, ret_gn_g[l], ret_gn_b[l], w_out[l])
        xs = xs + channel_mix(l, xs)
        nk_s.append(ka)
        nv_s.append(va)
        nr_s.append(Rs)

    return (xp, xs, jnp.stack(nk_p), jnp.stack(nv_p), jnp.stack(nr_p),
            jnp.stack(nk_s), jnp.stack(nv_s), jnp.stack(nr_s))
```

```python
import functools

import numpy as np
import jax
import jax.numpy as jnp
from jax import lax
from jax.experimental import pallas as pl
from jax.experimental.pallas import tpu as pltpu

CHUNK = 64
WINDOW = 128
HEAD_DIM = 64
RET_HEAD_DIM = 128
PAST_LEN = 2048
ROPE_THETA = 10000.0
TOP_K = 2
RMS_EPS = 1e-6
GN_EPS = 1e-5

LANES = 128
HALF = HEAD_DIM // 2
VMEM_LIMIT = 56 * 1024 * 1024
COL_TILE = 512
F32 = jnp.float32
BF16 = jnp.bfloat16


def _pick_tile(n, target, mult=16):
    best = None
    for t in range(mult, min(n, target) + 1, mult):
        if n % t == 0:
            best = t
    assert best is not None, (n, target)
    return best


def _params(*sem):
    return pltpu.CompilerParams(dimension_semantics=sem, vmem_limit_bytes=VMEM_LIMIT)


def _layout_perms(n_q, n_kv):
    g_sz = n_q // n_kv
    n_pairs = n_kv // 2
    perm_k = np.zeros(n_kv * HEAD_DIM, np.int32)
    perm_q = np.zeros(n_q * HEAD_DIM, np.int32)
    perm_ao = np.zeros(n_q * HEAD_DIM, np.int32)
    sink_heads = np.zeros((n_pairs, 2 * g_sz), np.int32)
    for p in range(n_pairs):
        for hf in range(2):
            for s in range(2):
                for r in range(HALF):
                    perm_k[p * LANES + hf * 64 + s * HALF + r] = (2 * p + s) * HEAD_DIM + hf * HALF + r
        for g in range(g_sz):
            bq = p * g_sz + g
            for s in range(2):
                head = g_sz * (2 * p + s) + g
                sink_heads[p, 2 * g + s] = head
                for hf in range(2):
                    for r in range(HALF):
                        perm_q[bq * LANES + hf * 64 + s * HALF + r] = head * HEAD_DIM + hf * HALF + r
                for d in range(HEAD_DIM):
                    perm_ao[bq * LANES + s * HEAD_DIM + d] = head * HEAD_DIM + d
    return perm_q, perm_k, perm_ao, sink_heads


def _rope_tables(pos):
    pf = pos.astype(F32)[:, None]
    inv_a = ROPE_THETA ** (-jnp.arange(0, HEAD_DIM, 2, dtype=F32) / HEAD_DIM)
    ang = pf * inv_a[None, :]
    c, s = jnp.cos(ang), jnp.sin(ang)
    ca = jnp.concatenate([c, c, c, c], axis=-1)
    sa = jnp.concatenate([-s, -s, s, s], axis=-1)
    inv_r = 10000.0 ** (-jnp.linspace(0.0, 1.0, RET_HEAD_DIM // 2, dtype=F32))
    ang = pf * inv_r[None, :]
    c, s = jnp.cos(ang), jnp.sin(ang)
    cr = jnp.concatenate([c, c], axis=-1)
    sr = jnp.concatenate([-s, s], axis=-1)
    return ca, sa, cr, sr


def _decay_tables(n_heads, chunk):
    log_g = jnp.log1p(-jnp.exp2(-5.0 - jnp.arange(n_heads, dtype=F32)))
    idx = jnp.arange(chunk, dtype=F32)
    dmat = jnp.exp(log_g[:, None, None] * jnp.abs(idx[:, None] - idx[None, :]))
    dec_in = jnp.exp(log_g[None, :] * (idx[:, None] + 1.0))
    dec_kv = jnp.exp(log_g[None, :] * (chunk - 1.0 - idx[:, None]))
    dec_chunk = jnp.exp(log_g * chunk)
    bc = lambda t: jnp.broadcast_to(t.T[:, :, None], (n_heads, chunk, LANES))
    return dmat, bc(dec_in), bc(dec_kv), dec_chunk


def _rope128(y, c, s):
    return y * c + pltpu.roll(y, 64, 1) * s


def _proj_kernel(nq, nr, x_ref, ln_ref, w_ref, m_ref, gq_ref, gk_ref, ca_ref, sa_ref, cr_ref, sr_ref,
                 qa_ref, ka_ref, va_ref, qr_ref, kr_ref, vr_ref, gate_ref, xn_ref):
    j = pl.program_id(1)

    @pl.when(j == 0)
    def _norm():
        x = x_ref[...]
        ms = jnp.mean(x * x, axis=-1, keepdims=True)
        xn_ref[...] = (x * lax.rsqrt(ms + RMS_EPS) * ln_ref[...]).astype(BF16)

    y = jnp.dot(xn_ref[...], w_ref[...], preferred_element_type=F32)
    n_sub = COL_TILE // LANES

    def head_norm_rope(yc, g):
        y2 = yc * yc
        hi = y2.astype(BF16)
        lo = (y2 - hi.astype(F32)).astype(BF16)
        ss = (jnp.dot(hi, m_ref[...], preferred_element_type=F32)
              + jnp.dot(lo, m_ref[...], preferred_element_type=F32))
        yn = yc * lax.rsqrt(ss * (1.0 / HEAD_DIM) + RMS_EPS) * g
        return _rope128(yn, ca_ref[...], sa_ref[...])

    @pl.when(j < nq)
    def _qa():
        for c in range(n_sub):
            sl = slice(c * LANES, (c + 1) * LANES)
            qa_ref[:, sl] = (head_norm_rope(y[:, sl], gq_ref[...]) * (HEAD_DIM ** -0.5)).astype(BF16)

    @pl.when(j == nq)
    def _kv():
        half = COL_TILE // 2
        for c in range(half // LANES):
            sl = slice(c * LANES, (c + 1) * LANES)
            ka_ref[:, sl] = head_norm_rope(y[:, sl], gk_ref[...])
        va_ref[...] = y[:, half:]

    @pl.when((j > nq) & (j <= nq + nr))
    def _qr():
        for c in range(n_sub):
            sl = slice(c * LANES, (c + 1) * LANES)
            qr_ref[:, sl] = _rope128(y[:, sl], cr_ref[...], sr_ref[...]).astype(BF16)

    @pl.when((j > nq + nr) & (j <= nq + 2 * nr))
    def _kr():
        for c in range(n_sub):
            sl = slice(c * LANES, (c + 1) * LANES)
            kr_ref[:, sl] = _rope128(y[:, sl], cr_ref[...], sr_ref[...]) * (RET_HEAD_DIM ** -0.5)

    @pl.when((j > nq + 2 * nr) & (j <= nq + 3 * nr))
    def _vr():
        vr_ref[...] = y.astype(BF16)

    @pl.when(j > nq + 3 * nr)
    def _gate():
        gate_ref[...] = y


def _proj(x, ln, w, m128, gq, gk, ca, sa, cr, sr, d_attn, d_kv, d_ret):
    n, d = x.shape
    assert d_attn % COL_TILE == 0 and 2 * d_kv == COL_TILE and d_ret % COL_TILE == 0
    nq, nr = d_attn // COL_TILE, d_ret // COL_TILE
    n_col = nq + 1 + 4 * nr
    tm = _pick_tile(n, 640)
    row = lambda i, j: (i, 0)
    clip = lambda lo, cnt: (lambda i, j: (i, jnp.clip(j - lo, 0, cnt - 1)))
    tab = pl.BlockSpec((tm, LANES), row)
    vec = pl.BlockSpec((1, LANES), lambda i, j: (0, 0))
    return pl.pallas_call(
        functools.partial(_proj_kernel, nq, nr),
        grid=(n // tm, n_col),
        in_specs=[
            pl.BlockSpec((tm, d), row),
            pl.BlockSpec((1, d), lambda i, j: (0, 0)),
            pl.BlockSpec((d, COL_TILE), lambda i, j: (0, j)),
            pl.BlockSpec((LANES, LANES), lambda i, j: (0, 0)),
            vec, vec, tab, tab, tab, tab,
        ],
        out_specs=[
            pl.BlockSpec((tm, COL_TILE), clip(0, nq)),
            pl.BlockSpec((tm, d_kv), row),
            pl.BlockSpec((tm, d_kv), row),
            pl.BlockSpec((tm, COL_TILE), clip(nq + 1, nr)),
            pl.BlockSpec((tm, COL_TILE), clip(nq + 1 + nr, nr)),
            pl.BlockSpec((tm, COL_TILE), clip(nq + 1 + 2 * nr, nr)),
            pl.BlockSpec((tm, COL_TILE), clip(nq + 1 + 3 * nr, nr)),
        ],
        out_shape=[
            jax.ShapeDtypeStruct((n, d_attn), BF16),
            jax.ShapeDtypeStruct((n, d_kv), F32),
            jax.ShapeDtypeStruct((n, d_kv), F32),
            jax.ShapeDtypeStruct((n, d_ret), BF16),
            jax.ShapeDtypeStruct((n, d_ret), F32),
            jax.ShapeDtypeStruct((n, d_ret), BF16),
            jax.ShapeDtypeStruct((n, d_ret), F32),
        ],
        scratch_shapes=[pltpu.VMEM((tm, d), BF16)],
        compiler_params=_params("parallel", "arbitrary"),
        name="proj",
    )(x, ln, w, m128, gq, gk, ca, sa, cr, sr)


def _attend(qa, k_all, v_all, valid, sink_ref, t, n_pairs, g_sz):
    lane = lax.broadcasted_iota(jnp.int32, (1, LANES), 1)
    slot_a = ((lane // HALF) % 2) == 0
    first_half = lane < HEAD_DIM
    zero = jnp.zeros((), BF16)
    outs = []
    for p in range(n_pairs):
        kp = k_all[:, p * LANES:(p + 1) * LANES]
        vp = v_all[:, p * LANES:(p + 1) * LANES]
        parts = []
        for g in range(g_sz):
            qb = qa[:, (p * g_sz + g) * LANES:(p * g_sz + g + 1) * LANES]
            parts.append(jnp.where(slot_a, qb, zero))
            parts.append(jnp.where(slot_a, zero, qb))
        qs = jnp.concatenate(parts, axis=0)
        s = lax.dot_general(qs, kp, (((1,), (1,)), ((), ())), preferred_element_type=F32)
        if valid is not None:
            s = jnp.where(valid, s, -jnp.inf)
        sk = sink_ref[p][:, 0:1]
        m = jnp.maximum(jnp.max(s, axis=-1, keepdims=True), sk)
        e = jnp.exp(s - m)
        den = jnp.sum(e, axis=-1, keepdims=True) + jnp.exp(sk - m)
        o = jnp.dot((e / den).astype(BF16), vp, preferred_element_type=F32)
        for g in range(g_sz):
            o_a = o[(2 * g) * t:(2 * g + 1) * t]
            o_b = o[(2 * g + 1) * t:(2 * g + 2) * t]
            outs.append(jnp.where(first_half, o_a, o_b))
    return outs


def _retain(qr, kr, vr, gate, r_in_ref, r_out_ref, dmat_ref, decin_ref, deckv_ref, dchunk_ref, gng, gnb, n_heads):
    outs = []
    for h in range(n_heads):
        sl = slice(h * RET_HEAD_DIM, (h + 1) * RET_HEAD_DIM)
        q = qr[:, sl]
        kf = kr[:, sl]
        v = vr[:, sl]
        s = lax.dot_general(q, kf.astype(BF16), (((1,), (1,)), ((), ())), preferred_element_type=F32) * dmat_ref[h]
        o = jnp.dot(s.astype(BF16), v, preferred_element_type=F32)
        r_prev = r_in_ref[h]
        o = o + jnp.dot(q, r_prev.astype(BF16), preferred_element_type=F32) * decin_ref[h]
        kd = (kf * deckv_ref[h]).astype(BF16)
        kv = lax.dot_general(kd, v, (((0,), (0,)), ((), ())), preferred_element_type=F32)
        r_out_ref[h] = r_prev * dchunk_ref[h] + kv
        mu = jnp.mean(o, axis=-1, keepdims=True)
        dev = o - mu
        var = jnp.mean(dev * dev, axis=-1, keepdims=True)
        r = dev * lax.rsqrt(var + GN_EPS) * gng[:, sl] + gnb[:, sl]
        gt = gate[:, sl]
        outs.append(gt * jax.nn.sigmoid(gt) * r)
    return outs


def _store_merged(merged_ref, ao, ro, d_attn):
    for b, blk in enumerate(ao):
        merged_ref[:, b * LANES:(b + 1) * LANES] = blk.astype(BF16)
    for h, blk in enumerate(ro):
        merged_ref[:, d_attn + h * RET_HEAD_DIM:d_attn + (h + 1) * RET_HEAD_DIM] = blk.astype(BF16)


def _mixer_prompt_kernel(nc, t, n_pairs, g_sz, n_heads, d_attn,
                         dchunk_ref, qa_ref, *rest):
    n_win = WINDOW // t + 1
    k_refs, v_refs = rest[:n_win], rest[n_win:2 * n_win]
    (qr_ref, kr_ref, vr_ref, gate_ref, sink_ref, dmat_ref, decin_ref, deckv_ref, gng_ref, gnb_ref,
     merged_ref, rout_ref, r_ref) = rest[2 * n_win:]
    c = pl.program_id(1)

    @pl.when(c == 0)
    def _reset():
        r_ref[...] = jnp.zeros_like(r_ref)

    k_all = jnp.concatenate([r[...].astype(BF16) for r in k_refs], axis=0)
    v_all = jnp.concatenate([r[...].astype(BF16) for r in v_refs], axis=0)
    kidx = lax.broadcasted_iota(jnp.int32, (1, n_win * t), 1)
    valid = kidx >= (n_win - 1 - c) * t
    ao = _attend(qa_ref[...], k_all, v_all, valid, sink_ref, t, n_pairs, g_sz)
    ro = _retain(qr_ref[...], kr_ref[...], vr_ref[...], gate_ref[...], r_ref, r_ref,
                 dmat_ref, decin_ref, deckv_ref, dchunk_ref, gng_ref[...], gnb_ref[...], n_heads)
    _store_merged(merged_ref, ao, ro, d_attn)

    @pl.when(c == nc - 1)
    def _emit_state():
        rout_ref[...] = r_ref[...]


def _mixer_sample_kernel(t, n_pairs, g_sz, n_heads, d_attn,
                         dchunk_ref, qa_ref, ka_ref, va_ref, qr_ref, kr_ref, vr_ref, gate_ref,
                         ck_ref, cv_ref, r0_ref,
                         sink_ref, dmat_ref, decin_ref, deckv_ref, gng_ref, gnb_ref,
                         merged_ref, rout_ref):
    k_all = jnp.concatenate([ck_ref[...], ka_ref[...].astype(BF16)], axis=0)
    v_all = jnp.concatenate([cv_ref[...], va_ref[...].astype(BF16)], axis=0)
    ao = _attend(qa_ref[...], k_all, v_all, None, sink_ref, t, n_pairs, g_sz)
    ro = _retain(qr_ref[...], kr_ref[...], vr_ref[...], gate_ref[...], r0_ref, rout_ref,
                 dmat_ref, decin_ref, deckv_ref, dchunk_ref, gng_ref[...], gnb_ref[...], n_heads)
    _store_merged(merged_ref, ao, ro, d_attn)


def _sink_table(sinks, sink_heads, t):
    per_row = jnp.repeat(sinks.astype(F32)[sink_heads], t, axis=1)
    return jnp.broadcast_to(per_row[:, :, None], per_row.shape + (LANES,))


def _mixer_prompt(proj_out, sinks, sink_heads, gng, gnb, batch, seq, n_kv, n_q, n_heads):
    qa, ka, va, qr, kr, vr, gate = proj_out
    t = CHUNK
    nc = seq // t
    d_attn, d_kv, d_ret = n_q * HEAD_DIM, n_kv * HEAD_DIM, n_heads * RET_HEAD_DIM
    n_pairs, g_sz = n_kv // 2, n_q // n_kv
    dmat, decin, deckv, dchunk = _decay_tables(n_heads, t)
    sink_tab = _sink_table(sinks, sink_heads, t)
    row = lambda b, c, *_: (b * nc + c, 0)
    back = lambda k: (lambda b, c, *_: (b * nc + jnp.maximum(c - k, 0), 0))
    const3 = lambda b, c, *_: (0, 0, 0)
    const2 = lambda b, c, *_: (0, 0)
    n_win = WINDOW // t + 1
    kv_specs = [pl.BlockSpec((t, d_kv), back(n_win - 1 - s)) for s in range(n_win)]
    grid_spec = pltpu.PrefetchScalarGridSpec(
        num_scalar_prefetch=1,
        grid=(batch, nc),
        in_specs=[
            pl.BlockSpec((t, d_attn), row), *kv_specs, *kv_specs,
            pl.BlockSpec((t, d_ret), row), pl.BlockSpec((t, d_ret), row), pl.BlockSpec((t, d_ret), row),
            pl.BlockSpec((t, d_ret), row),
            pl.BlockSpec(sink_tab.shape, const3), pl.BlockSpec(dmat.shape, const3),
            pl.BlockSpec(decin.shape, const3), pl.BlockSpec(deckv.shape, const3),
            pl.BlockSpec((1, d_ret), const2), pl.BlockSpec((1, d_ret), const2),
        ],
        out_specs=[
            pl.BlockSpec((t, d_attn + d_ret), row),
            pl.BlockSpec((None, n_heads, RET_HEAD_DIM, RET_HEAD_DIM), lambda b, c, *_: (b, 0, 0, 0)),
        ],
        scratch_shapes=[pltpu.VMEM((n_heads, RET_HEAD_DIM, RET_HEAD_DIM), F32)],
    )
    return pl.pallas_call(
        functools.partial(_mixer_prompt_kernel, nc, t, n_pairs, g_sz, n_heads, d_attn),
        grid_spec=grid_spec,
        out_shape=[jax.ShapeDtypeStruct((batch * seq, d_attn + d_ret), BF16),
                   jax.ShapeDtypeStruct((batch, n_heads, RET_HEAD_DIM, RET_HEAD_DIM), F32)],
        compiler_params=_params("parallel", "arbitrary"),
        name="mixer_prompt",
    )(dchunk, qa, *([ka] * n_win), *([va] * n_win), qr, kr, vr, gate, sink_tab, dmat, decin, deckv, gng, gnb)


def _mixer_sample(proj_out, ck, cv, r0, sinks, sink_heads, gng, gnb, row0, dbatch, t, n_kv, n_q, n_heads):
    qa, ka, va, qr, kr, vr, gate = proj_out
    d_attn, d_kv, d_ret = n_q * HEAD_DIM, n_kv * HEAD_DIM, n_heads * RET_HEAD_DIM
    n_pairs, g_sz = n_kv // 2, n_q // n_kv
    dmat, decin, deckv, dchunk = _decay_tables(n_heads, t)
    sink_tab = _sink_table(sinks, sink_heads, t)
    blk0 = row0 // t
    row = lambda b, *_: (blk0 + b, 0)
    const3 = lambda b, *_: (0, 0, 0)
    const2 = lambda b, *_: (0, 0)
    state_spec = pl.BlockSpec((None, n_heads, RET_HEAD_DIM, RET_HEAD_DIM), lambda b, *_: (b, 0, 0, 0))
    n_hist = ck.shape[1]
    grid_spec = pltpu.PrefetchScalarGridSpec(
        num_scalar_prefetch=1,
        grid=(dbatch,),
        in_specs=[
            pl.BlockSpec((t, d_attn), row), pl.BlockSpec((t, d_kv), row), pl.BlockSpec((t, d_kv), row),
            pl.BlockSpec((t, d_ret), row), pl.BlockSpec((t, d_ret), row), pl.BlockSpec((t, d_ret), row),
            pl.BlockSpec((t, d_ret), row),
            pl.BlockSpec((None, n_hist, d_kv), lambda b, *_: (b, 0, 0)),
            pl.BlockSpec((None, n_hist, d_kv), lambda b, *_: (b, 0, 0)),
            state_spec,
            pl.BlockSpec(sink_tab.shape, const3), pl.BlockSpec(dmat.shape, const3),
            pl.BlockSpec(decin.shape, const3), pl.BlockSpec(deckv.shape, const3),
            pl.BlockSpec((1, d_ret), const2), pl.BlockSpec((1, d_ret), const2),
        ],
        out_specs=[pl.BlockSpec((t, d_attn + d_ret), lambda b, *_: (b, 0)), state_spec],
    )
    return pl.pallas_call(
        functools.partial(_mixer_sample_kernel, t, n_pairs, g_sz, n_heads, d_attn),
        grid_spec=grid_spec,
        out_shape=[jax.ShapeDtypeStruct((dbatch * t, d_attn + d_ret), BF16),
                   jax.ShapeDtypeStruct(r0.shape, F32)],
        compiler_params=_params("parallel"),
        name="mixer_sample",
    )(dchunk, qa, ka, va, qr, kr, vr, gate, ck, cv, r0, sink_tab, dmat, decin, deckv, gng, gnb)


def _outproj_kernel(m_ref, w_ref, x_ref, o_ref):
    o_ref[...] = x_ref[...] + jnp.dot(m_ref[...], w_ref[...], preferred_element_type=F32)


def _outproj(merged, w, x):
    n, d = x.shape
    k = merged.shape[1]
    tm = _pick_tile(n, 640)
    tn = _pick_tile(d, 1024, LANES)
    return pl.pallas_call(
        _outproj_kernel,
        grid=(n // tm, d // tn),
        in_specs=[pl.BlockSpec((tm, k), lambda i, j: (i, 0)),
                  pl.BlockSpec((k, tn), lambda i, j: (0, j)),
                  pl.BlockSpec((tm, tn), lambda i, j: (i, j))],
        out_specs=pl.BlockSpec((tm, tn), lambda i, j: (i, j)),
        out_shape=jax.ShapeDtypeStruct((n, d), F32),
        compiler_params=_params("parallel", "arbitrary"),
        name="outproj",
    )(merged, w, x)


def _swiglu_step(xn, wg, wu, wd):
    g = jnp.dot(xn, wg, preferred_element_type=F32)
    u = jnp.dot(xn, wu, preferred_element_type=F32)
    h = (g * jax.nn.sigmoid(g) * u).astype(BF16)
    return jnp.dot(h, wd, preferred_element_type=F32)


def _ffn_kernel(nf, x_ref, ln_ref, wg_ref, wu_ref, wd_ref, o_ref, xn_ref, acc_ref):
    f = pl.program_id(1)

    @pl.when(f == 0)
    def _norm():
        x = x_ref[...]
        ms = jnp.mean(x * x, axis=-1, keepdims=True)
        xn_ref[...] = (x * lax.rsqrt(ms + RMS_EPS) * ln_ref[...]).astype(BF16)
        acc_ref[...] = jnp.zeros_like(acc_ref)

    acc_ref[...] += _swiglu_step(xn_ref[...], wg_ref[...], wu_ref[...], wd_ref[...])

    @pl.when(f == nf - 1)
    def _emit():
        o_ref[...] = x_ref[...] + acc_ref[...]


def _ffn(x, ln, wg, wu, wd):
    n, d = x.shape
    fp = wg.shape[1]
    tf = _pick_tile(fp, 512, LANES)
    nf = fp // tf
    tm = _pick_tile(n, 640)
    return pl.pallas_call(
        functools.partial(_ffn_kernel, nf),
        grid=(n // tm, nf),
        in_specs=[pl.BlockSpec((tm, d), lambda i, f: (i, 0)),
                  pl.BlockSpec((1, d), lambda i, f: (0, 0)),
                  pl.BlockSpec((d, tf), lambda i, f: (0, f)),
                  pl.BlockSpec((d, tf), lambda i, f: (0, f)),
                  pl.BlockSpec((tf, d), lambda i, f: (f, 0))],
        out_specs=pl.BlockSpec((tm, d), lambda i, f: (i, 0)),
        out_shape=jax.ShapeDtypeStruct((n, d), F32),
        scratch_shapes=[pltpu.VMEM((tm, d), BF16), pltpu.VMEM((tm, d), F32)],
        compiler_params=_params("parallel", "arbitrary"),
        name="ffn",
    )(x, ln, wg, wu, wd)


def _router_kernel(n_exp, x_ref, ln_ref, wr_hi_ref, wr_lo_ref, hn_ref, idx_ref, gate_ref):
    x = x_ref[...]
    ms = jnp.mean(x * x, axis=-1, keepdims=True)
    hf = x * lax.rsqrt(ms + RMS_EPS) * ln_ref[...]
    hn = hf.astype(BF16)
    hn_ref[...] = hn
    h_lo = (hf - hn.astype(F32)).astype(BF16)
    logits = (jnp.dot(hn, wr_hi_ref[...], preferred_element_type=F32)
              + jnp.dot(h_lo, wr_hi_ref[...], preferred_element_type=F32)
              + jnp.dot(hn, wr_lo_ref[...], preferred_element_type=F32))
    lane = lax.broadcasted_iota(jnp.int32, logits.shape, 1)
    logits = jnp.where(lane < n_exp, logits, -jnp.inf)
    m1 = jnp.max(logits, axis=-1, keepdims=True)
    i1 = jnp.min(jnp.where(logits == m1, lane, LANES), axis=-1, keepdims=True)
    rest = jnp.where(lane == i1, -jnp.inf, logits)
    m2 = jnp.max(rest, axis=-1, keepdims=True)
    i2 = jnp.min(jnp.where(rest == m2, lane, LANES), axis=-1, keepdims=True)
    e2 = jnp.exp(m2 - m1)
    den = 1.0 + e2
    idx_ref[...] = jnp.where(lane == 0, i1, jnp.where(lane == 1, i2, 0))
    gate_ref[...] = jnp.where(lane == 0, 1.0 / den, jnp.where(lane == 1, e2 / den, 0.0))


def _router(x, ln, wr_hi, wr_lo, n_exp):
    n, d = x.shape
    tm = _pick_tile(n, 640)
    row = lambda i: (i, 0)
    return pl.pallas_call(
        functools.partial(_router_kernel, n_exp),
        grid=(n // tm,),
        in_specs=[pl.BlockSpec((tm, d), row), pl.BlockSpec((1, d), lambda i: (0, 0)),
                  pl.BlockSpec((d, LANES), lambda i: (0, 0)), pl.BlockSpec((d, LANES), lambda i: (0, 0))],
        out_specs=[pl.BlockSpec((tm, d), row), pl.BlockSpec((tm, LANES), row), pl.BlockSpec((tm, LANES), row)],
        out_shape=[jax.ShapeDtypeStruct((n, d), BF16), jax.ShapeDtypeStruct((n, LANES), jnp.int32),
                   jax.ShapeDtypeStruct((n, LANES), F32)],
        compiler_params=_params("parallel"),
        name="router",
    )(x, ln, wr_hi, wr_lo)


def _experts_kernel(nf, te_ref, act_ref, x_ref, wg_ref, wu_ref, wd_ref, o_ref, acc_ref):
    i = pl.program_id(0)
    f = pl.program_id(1)
    active = act_ref[i] == 1

    @pl.when(active & (f == 0))
    def _first():
        acc_ref[...] = _swiglu_step(x_ref[...], wg_ref[...], wu_ref[...], wd_ref[...])

    @pl.when(active & (f > 0))
    def _rest():
        acc_ref[...] += _swiglu_step(x_ref[...], wg_ref[...], wu_ref[...], wd_ref[...])

    @pl.when(active & (f == nf - 1))
    def _emit():
        o_ref[...] = acc_ref[...]

    @pl.when(jnp.logical_not(active) & (f == nf - 1))
    def _pad():
        o_ref[...] = jnp.zeros_like(o_ref)


def _experts(xs, wg, wu, wd, tile_expert, tile_active, tm):
    p, d = xs.shape
    n_exp, _, fe = wg.shape
    tf = _pick_tile(fe, 512, LANES)
    nf = fe // tf
    fidx = lambda i, f, te, act: f * act[i] + (nf - 1) * (1 - act[i])
    grid_spec = pltpu.PrefetchScalarGridSpec(
        num_scalar_prefetch=2,
        grid=(p // tm, nf),
        in_specs=[pl.BlockSpec((tm, d), lambda i, f, te, act: (i, 0)),
                  pl.BlockSpec((None, d, tf), lambda i, f, te, act: (te[i], 0, fidx(i, f, te, act))),
                  pl.BlockSpec((None, d, tf), lambda i, f, te, act: (te[i], 0, fidx(i, f, te, act))),
                  pl.BlockSpec((None, tf, d), lambda i, f, te, act: (te[i], fidx(i, f, te, act), 0))],
        out_specs=pl.BlockSpec((tm, d), lambda i, f, te, act: (i, 0)),
        scratch_shapes=[pltpu.VMEM((tm, d), F32)],
    )
    return pl.pallas_call(
        functools.partial(_experts_kernel, nf),
        grid_spec=grid_spec,
        out_shape=jax.ShapeDtypeStruct((p, d), F32),
        compiler_params=_params("parallel", "arbitrary"),
        name="experts",
    )(tile_expert, tile_active, xs, wg, wu, wd)


def _moe(x, ln, w_router, wg, wu, wd):
    n, d = x.shape
    n_exp = w_router.shape[1]
    wr = jnp.pad(w_router.astype(F32), ((0, 0), (0, LANES - n_exp)))
    wr_hi = wr.astype(BF16)
    wr_lo = (wr - wr_hi.astype(F32)).astype(BF16)
    hn, ridx, rgate = _router(x, ln, wr_hi, wr_lo, n_exp)
    tm = 1024 if TOP_K * n >= 8192 else 64
    n_asg = TOP_K * n
    n_tiles = -(-(n_asg + n_exp * (tm - 1)) // tm)
    flat_e = ridx[:, :TOP_K].reshape(-1)
    onehot = (flat_e[:, None] == jnp.arange(n_exp, dtype=jnp.int32)[None, :]).astype(jnp.int32)
    csum = jnp.cumsum(onehot, axis=0)
    counts = csum[-1]
    rank = jnp.take_along_axis(csum, flat_e[:, None], axis=1)[:, 0] - 1
    padded = ((counts + tm - 1) // tm) * tm
    ends = jnp.cumsum(padded)
    pos = (ends - padded)[flat_e] + rank
    src_tok = jnp.zeros((n_tiles * tm,), jnp.int32).at[pos].set(jnp.arange(n_asg, dtype=jnp.int32) // TOP_K)
    tile_start = jnp.arange(n_tiles, dtype=jnp.int32) * tm
    te = jnp.sum((tile_start[:, None] >= ends[None, :]).astype(jnp.int32), axis=1)
    tile_active = (te < n_exp).astype(jnp.int32)
    tile_expert = jnp.minimum(te, n_exp - 1)
    xs = jnp.take(hn, src_tok, axis=0)
    ys = _experts(xs, wg, wu, wd, tile_expert, tile_active, tm)
    yg = jnp.take(ys, pos, axis=0).reshape(n, TOP_K, d)
    g = rgate[:, :TOP_K]
    return x + (g[:, 0:1] * yg[:, 0] + g[:, 1:2] * yg[:, 1])


def kernel(x_prompt, x_sample, cache_attn_k, cache_attn_v, state_ret, ln_mix, w_in, q_norm, k_norm, attn_sinks,
           ret_gn_g, ret_gn_b, w_out, ln_ffn, ffn_w_gate, ffn_w_up, ffn_w_down, moe_router, moe_w_gate, moe_w_up,
           moe_w_down):
    batch, seq, d = x_prompt.shape
    dbatch, t_s, _ = x_sample.shape
    depth = ln_mix.shape[0]
    n_kv = cache_attn_k.shape[3]
    n_heads = state_ret.shape[2]
    d_kv = n_kv * HEAD_DIM
    d_ret = n_heads * RET_HEAD_DIM
    d_attn = w_out.shape[1] - d_ret
    n_q = d_attn // HEAD_DIM
    assert cache_attn_k.shape[4] == HEAD_DIM and state_ret.shape[3] == RET_HEAD_DIM and n_kv % 2 == 0
    assert seq % CHUNK == 0 and WINDOW % CHUNK == 0 and seq >= WINDOW and cache_attn_k.shape[2] == WINDOW
    assert w_in.shape[2] == d_attn + 2 * d_kv + 4 * d_ret
    n_p, n_s = batch * seq, dbatch * t_s
    keep = min(WINDOW, seq)

    perm_q, perm_k, perm_ao, sink_heads = _layout_perms(n_q, n_kv)
    inv_k = np.argsort(perm_k)
    col_perm = np.concatenate([perm_q, d_attn + perm_k, np.arange(d_attn + d_kv, w_in.shape[2])])
    row_perm = np.concatenate([perm_ao, np.arange(d_attn, d_attn + d_ret)])

    pos = jnp.concatenate([jnp.tile(jnp.arange(seq), batch), PAST_LEN + jnp.tile(jnp.arange(t_s), dbatch)])
    ca, sa, cr, sr = _rope_tables(pos)
    lane = np.arange(LANES)
    m128 = jnp.asarray(((lane[:, None] // HALF) % 2) == ((lane[None, :] // HALF) % 2), BF16)
    head_gain = lambda g: jnp.tile(g.astype(F32).reshape(2, HALF), (1, 2)).reshape(1, LANES)

    x = jnp.concatenate([x_prompt.reshape(n_p, d), x_sample.reshape(n_s, d)], axis=0)
    nk_p, nv_p, nr_p, nk_s, nv_s, nr_s = [], [], [], [], [], []
    for l in range(depth):
        w_in_l = w_in[l][:, col_perm].astype(BF16)
        w_out_l = w_out[l][row_perm, :].astype(BF16)
        proj_out = _proj(x, ln_mix[l][None, :], w_in_l, m128, head_gain(q_norm[l]), head_gain(k_norm[l]),
                         ca, sa, cr, sr, d_attn, d_kv, d_ret)
        gng, gnb = ret_gn_g[l][None, :], ret_gn_b[l][None, :]
        merged_p, r_p = _mixer_prompt(proj_out, attn_sinks[l], sink_heads, gng, gnb, batch, seq, n_kv, n_q, n_heads)
        ck = cache_attn_k[l].reshape(dbatch, WINDOW, d_kv)[:, :, perm_k].astype(BF16)
        cv = cache_attn_v[l].reshape(dbatch, WINDOW, d_kv).astype(BF16)
        merged_s, r_s = _mixer_sample(proj_out, ck, cv, state_ret[l], attn_sinks[l], sink_heads, gng, gnb,
                                      n_p, dbatch, t_s, n_kv, n_q, n_heads)
        x = _outproj(jnp.concatenate([merged_p, merged_s], axis=0), w_out_l, x)

        ka_nat = proj_out[1][:, inv_k]
        va = proj_out[2]
        nk_p.append(ka_nat[:n_p].reshape(batch, seq, n_kv, HEAD_DIM)[:, seq - keep:])
        nv_p.append(va[:n_p].reshape(batch, seq, n_kv, HEAD_DIM)[:, seq - keep:])
        nk_s.append(ka_nat[n_p:].reshape(dbatch, t_s, n_kv, HEAD_DIM))
        nv_s.append(va[n_p:].reshape(dbatch, t_s, n_kv, HEAD_DIM))
        nr_p.append(r_p)
        nr_s.append(r_s)

        i = l // 2
        if l % 2 == 0:
            f = ffn_w_gate.shape[2]
            fpad = (-f) % COL_TILE
            wg = jnp.pad(ffn_w_gate[i].astype(BF16), ((0, 0), (0, fpad)))
            wu = jnp.pad(ffn_w_up[i].astype(BF16), ((0, 0), (0, fpad)))
            wd = jnp.pad(ffn_w_down[i].astype(BF16), ((0, fpad), (0, 0)))
            x = _ffn(x, ln_ffn[l][None, :], wg, wu, wd)
        else:
            x = _moe(x, ln_ffn[l][None, :], moe_router[i], moe_w_gate[i].astype(BF16), moe_w_up[i].astype(BF16),
                     moe_w_down[i].astype(BF16))

    return (x[:n_p].reshape(batch, seq, d), x[n_p:].reshape(dbatch, t_s, d),
            jnp.stack(nk_p), jnp.stack(nv_p), jnp.stack(nr_p), jnp.stack(nk_s), jnp.stack(nv_s), jnp.stack(nr_s))
```

```python
import functools

import numpy as np
import jax
import jax.numpy as jnp
from jax import lax
from jax.experimental import pallas as pl
from jax.experimental.pallas import tpu as pltpu

CHUNK = 64
WINDOW = 128
HEAD_DIM = 64
RET_HEAD_DIM = 128
PAST_LEN = 2048
ROPE_THETA = 10000.0
TOP_K = 2
RMS_EPS = 1e-6
GN_EPS = 1e-5

LANES = 128
HALF = HEAD_DIM // 2
VMEM_LIMIT = 56 * 1024 * 1024
COL_TILE = 512
F32 = jnp.float32
BF16 = jnp.bfloat16


def _pick_tile(n, target, mult=16):
    best = None
    for t in range(mult, min(n, target) + 1, mult):
        if n % t == 0:
            best = t
    assert best is not None, (n, target)
    return best


def _params(*sem):
    return pltpu.CompilerParams(dimension_semantics=sem, vmem_limit_bytes=VMEM_LIMIT)


def _layout_perms(n_q, n_kv):
    g_sz = n_q // n_kv
    n_pairs = n_kv // 2
    perm_k = np.zeros(n_kv * HEAD_DIM, np.int32)
    perm_q = np.zeros(n_q * HEAD_DIM, np.int32)
    perm_ao = np.zeros(n_q * HEAD_DIM, np.int32)
    sink_heads = np.zeros((n_pairs, 2 * g_sz), np.int32)
    for p in range(n_pairs):
        for hf in range(2):
            for s in range(2):
                for r in range(HALF):
                    perm_k[p * LANES + hf * 64 + s * HALF + r] = (2 * p + s) * HEAD_DIM + hf * HALF + r
        for g in range(g_sz):
            bq = p * g_sz + g
            for s in range(2):
                head = g_sz * (2 * p + s) + g
                sink_heads[p, 2 * g + s] = head
                for hf in range(2):
                    for r in range(HALF):
                        perm_q[bq * LANES + hf * 64 + s * HALF + r] = head * HEAD_DIM + hf * HALF + r
                for d in range(HEAD_DIM):
                    perm_ao[bq * LANES + s * HEAD_DIM + d] = head * HEAD_DIM + d
    return perm_q, perm_k, perm_ao, sink_heads


def _rope_tables(pos):
    pf = pos.astype(F32)[:, None]
    inv_a = ROPE_THETA ** (-jnp.arange(0, HEAD_DIM, 2, dtype=F32) / HEAD_DIM)
    ang = pf * inv_a[None, :]
    c, s = jnp.cos(ang), jnp.sin(ang)
    ca = jnp.concatenate([c, c, c, c], axis=-1)
    sa = jnp.concatenate([-s, -s, s, s], axis=-1)
    inv_r = 10000.0 ** (-jnp.linspace(0.0, 1.0, RET_HEAD_DIM // 2, dtype=F32))
    ang = pf * inv_r[None, :]
    c, s = jnp.cos(ang), jnp.sin(ang)
    cr = jnp.concatenate([c, c], axis=-1)
    sr = jnp.concatenate([-s, s], axis=-1)
    return ca, sa, cr, sr


def _decay_tables(n_heads, chunk):
    log_g = jnp.log1p(-jnp.exp2(-5.0 - jnp.arange(n_heads, dtype=F32)))
    idx = jnp.arange(chunk, dtype=F32)
    dmat = jnp.exp(log_g[:, None, None] * jnp.abs(idx[:, None] - idx[None, :]))
    dec_in = jnp.exp(log_g[None, :] * (idx[:, None] + 1.0))
    dec_kv = jnp.exp(log_g[None, :] * (chunk - 1.0 - idx[:, None]))
    dec_chunk = jnp.exp(log_g * chunk)
    bc = lambda t: jnp.broadcast_to(t.T[:, :, None], (n_heads, chunk, LANES))
    return dmat, bc(dec_in), bc(dec_kv), dec_chunk


def _rope128(y, c, s):
    return y * c + pltpu.roll(y, 64, 1) * s


def _proj_kernel(nq, nr, x_ref, ln_ref, w_ref, m_ref, gq_ref, gk_ref, ca_ref, sa_ref, cr_ref, sr_ref,
                 qa_ref, ka_ref, va_ref, qr_ref, kr_ref, vr_ref, gate_ref, xn_ref):
    j = pl.program_id(1)

    @pl.when(j == 0)
    def _norm():
        x = x_ref[...]
        ms = jnp.mean(x * x, axis=-1, keepdims=True)
        xn_ref[...] = (x * lax.rsqrt(ms + RMS_EPS) * ln_ref[...]).astype(BF16)

    y = jnp.dot(xn_ref[...], w_ref[...], preferred_element_type=F32)
    n_sub = COL_TILE // LANES

    def head_norm_rope(yc, g):
        y2 = yc * yc
        hi = y2.astype(BF16)
        lo = (y2 - hi.astype(F32)).astype(BF16)
        ss = (jnp.dot(hi, m_ref[...], preferred_element_type=F32)
              + jnp.dot(lo, m_ref[...], preferred_element_type=F32))
        yn = yc * lax.rsqrt(ss * (1.0 / HEAD_DIM) + RMS_EPS) * g
        return _rope128(yn, ca_ref[...], sa_ref[...])

    @pl.when(j < nq)
    def _qa():
        for c in range(n_sub):
            sl = slice(c * LANES, (c + 1) * LANES)
            qa_ref[:, sl] = (head_norm_rope(y[:, sl], gq_ref[...]) * (HEAD_DIM ** -0.5)).astype(BF16)

    @pl.when(j == nq)
    def _kv():
        half = COL_TILE // 2
        for c in range(half // LANES):
            sl = slice(c * LANES, (c + 1) * LANES)
            ka_ref[:, sl] = head_norm_rope(y[:, sl], gk_ref[...])
        va_ref[...] = y[:, half:]

    @pl.when((j > nq) & (j <= nq + nr))
    def _qr():
        for c in range(n_sub):
            sl = slice(c * LANES, (c + 1) * LANES)
            qr_ref[:, sl] = _rope128(y[:, sl], cr_ref[...], sr_ref[...]).astype(BF16)

    @pl.when((j > nq + nr) & (j <= nq + 2 * nr))
    def _kr():
        for c in range(n_sub):
            sl = slice(c * LANES, (c + 1) * LANES)
            kr_ref[:, sl] = _rope128(y[:, sl], cr_ref[...], sr_ref[...]) * (RET_HEAD_DIM ** -0.5)

    @pl.when((j > nq + 2 * nr) & (j <= nq + 3 * nr))
    def _vr():
        vr_ref[...] = y.astype(BF16)

    @pl.when(j > nq + 3 * nr)
    def _gate():
        gate_ref[...] = y


def _proj(x, ln, w, m128, gq, gk, ca, sa, cr, sr, d_attn, d_kv, d_ret, tm_target):
    n, d = x.shape
    assert d_attn % COL_TILE == 0 and 2 * d_kv == COL_TILE and d_ret % COL_TILE == 0
    nq, nr = d_attn // COL_TILE, d_ret // COL_TILE
    n_col = nq + 1 + 4 * nr
    tm = _pick_tile(n, tm_target)
    row = lambda i, j: (i, 0)
    clip = lambda lo, cnt: (lambda i, j: (i, jnp.clip(j - lo, 0, cnt - 1)))
    tab = pl.BlockSpec((tm, LANES), row)
    vec = pl.BlockSpec((1, LANES), lambda i, j: (0, 0))
    return pl.pallas_call(
        functools.partial(_proj_kernel, nq, nr),
        grid=(n // tm, n_col),
        in_specs=[
            pl.BlockSpec((tm, d), row),
            pl.BlockSpec((1, d), lambda i, j: (0, 0)),
            pl.BlockSpec((d, COL_TILE), lambda i, j: (0, j)),
            pl.BlockSpec((LANES, LANES), lambda i, j: (0, 0)),
            vec, vec, tab, tab, tab, tab,
        ],
        out_specs=[
            pl.BlockSpec((tm, COL_TILE), clip(0, nq)),
            pl.BlockSpec((tm, d_kv), row),
            pl.BlockSpec((tm, d_kv), row),
            pl.BlockSpec((tm, COL_TILE), clip(nq + 1, nr)),
            pl.BlockSpec((tm, COL_TILE), clip(nq + 1 + nr, nr)),
            pl.BlockSpec((tm, COL_TILE), clip(nq + 1 + 2 * nr, nr)),
            pl.BlockSpec((tm, COL_TILE), clip(nq + 1 + 3 * nr, nr)),
        ],
        out_shape=[
            jax.ShapeDtypeStruct((n, d_attn), BF16),
            jax.ShapeDtypeStruct((n, d_kv), F32),
            jax.ShapeDtypeStruct((n, d_kv), F32),
            jax.ShapeDtypeStruct((n, d_ret), BF16),
            jax.ShapeDtypeStruct((n, d_ret), F32),
            jax.ShapeDtypeStruct((n, d_ret), BF16),
            jax.ShapeDtypeStruct((n, d_ret), F32),
        ],
        scratch_shapes=[pltpu.VMEM((tm, d), BF16)],
        compiler_params=_params("parallel", "arbitrary"),
        name="proj",
    )(x, ln, w, m128, gq, gk, ca, sa, cr, sr)


def _attend(items, valid, sink_ref, t, n_pairs, g_sz):
    lane = lax.broadcasted_iota(jnp.int32, (1, LANES), 1)
    slot_a = ((lane // HALF) % 2) == 0
    first_half = lane < HEAD_DIM
    zero = jnp.zeros((), BF16)
    jobs = [(i, p) for i in range(len(items)) for p in range(n_pairs)]
    scores = []
    for i, p in jobs:
        qa, k_all, _ = items[i]
        parts = []
        for g in range(g_sz):
            qb = qa[:, (p * g_sz + g) * LANES:(p * g_sz + g + 1) * LANES]
            parts.append(jnp.where(slot_a, qb, zero))
            parts.append(jnp.where(slot_a, zero, qb))
        qs = jnp.concatenate(parts, axis=0)
        scores.append(lax.dot_general(qs, k_all[:, p * LANES:(p + 1) * LANES], (((1,), (1,)), ((), ())),
                                      preferred_element_type=F32))
    probs = []
    for (i, p), s in zip(jobs, scores):
        if valid is not None:
            s = jnp.where(valid, s, -jnp.inf)
        sk = sink_ref[p][:, 0:1]
        m = jnp.maximum(jnp.max(s, axis=-1, keepdims=True), sk)
        e = jnp.exp(s - m)
        den = jnp.sum(e, axis=-1, keepdims=True) + jnp.exp(sk - m)
        probs.append((e / den).astype(BF16))
    outs = [[] for _ in items]
    for (i, p), pn in zip(jobs, probs):
        o = jnp.dot(pn, items[i][2][:, p * LANES:(p + 1) * LANES], preferred_element_type=F32)
        for g in range(g_sz):
            o_a = o[(2 * g) * t:(2 * g + 1) * t]
            o_b = o[(2 * g + 1) * t:(2 * g + 2) * t]
            outs[i].append(jnp.where(first_half, o_a, o_b))
    return outs


def _retain(items, dmat_ref, decin_ref, deckv_ref, dchunk_ref, gng, gnb, n_heads):
    nt = (((1,), (1,)), ((), ()))
    tn = (((0,), (0,)), ((), ()))
    jobs = [(i, h) for i in range(len(items)) for h in range(n_heads)]
    sl = lambda h: slice(h * RET_HEAD_DIM, (h + 1) * RET_HEAD_DIM)
    intra, inter, r_prevs = [], [], []
    for i, h in jobs:
        qr, kr, _, _, r_in_ref, _ = items[i]
        q = qr[:, sl(h)]
        r_prev = r_in_ref[h]
        r_prevs.append(r_prev)
        intra.append(lax.dot_general(q, kr[:, sl(h)].astype(BF16), nt, preferred_element_type=F32) * dmat_ref[h])
        inter.append(jnp.dot(q, r_prev.astype(BF16), preferred_element_type=F32) * decin_ref[h])
    for (i, h), r_prev in zip(jobs, r_prevs):
        _, kr, vr, _, _, r_out_ref = items[i]
        kd = (kr[:, sl(h)] * deckv_ref[h]).astype(BF16)
        kv = lax.dot_general(kd, vr[:, sl(h)], tn, preferred_element_type=F32)
        r_out_ref[h] = r_prev * dchunk_ref[h] + kv
    outs = [[] for _ in items]
    for (i, h), s, o_inter in zip(jobs, intra, inter):
        _, _, vr, gate, _, _ = items[i]
        o = jnp.dot(s.astype(BF16), vr[:, sl(h)], preferred_element_type=F32) + o_inter
        mu = jnp.mean(o, axis=-1, keepdims=True)
        dev = o - mu
        var = jnp.mean(dev * dev, axis=-1, keepdims=True)
        r = dev * lax.rsqrt(var + GN_EPS) * gng[:, sl(h)] + gnb[:, sl(h)]
        gt = gate[:, sl(h)]
        outs[i].append(gt * jax.nn.sigmoid(gt) * r)
    return outs


def _store_merged(merged_ref, ao, ro, d_attn):
    for b, blk in enumerate(ao):
        merged_ref[:, b * LANES:(b + 1) * LANES] = blk.astype(BF16)
    for h, blk in enumerate(ro):
        merged_ref[:, d_attn + h * RET_HEAD_DIM:d_attn + (h + 1) * RET_HEAD_DIM] = blk.astype(BF16)


def _mixer_prompt_kernel(nc, t, batch, n_pairs, g_sz, n_heads, d_attn,
                         dchunk_ref, qa_ref, *rest):
    n_win = WINDOW // t + 1
    k_refs, v_refs = rest[:n_win], rest[n_win:2 * n_win]
    (qr_ref, kr_ref, vr_ref, gate_ref, sink_ref, dmat_ref, decin_ref, deckv_ref, gng_ref, gnb_ref,
     merged_ref, rout_ref, r_ref) = rest[2 * n_win:]
    c = pl.program_id(0)

    @pl.when(c == 0)
    def _reset():
        r_ref[...] = jnp.zeros_like(r_ref)

    kidx = lax.broadcasted_iota(jnp.int32, (1, n_win * t), 1)
    valid = kidx >= (n_win - 1 - c) * t
    att_items = [(qa_ref[b],
                  jnp.concatenate([r[b].astype(BF16) for r in k_refs], axis=0),
                  jnp.concatenate([r[b].astype(BF16) for r in v_refs], axis=0)) for b in range(batch)]
    ret_items = [(qr_ref[b], kr_ref[b], vr_ref[b], gate_ref[b], r_ref.at[b], r_ref.at[b]) for b in range(batch)]
    ao = _attend(att_items, valid, sink_ref, t, n_pairs, g_sz)
    ro = _retain(ret_items, dmat_ref, decin_ref, deckv_ref, dchunk_ref, gng_ref[...], gnb_ref[...], n_heads)
    for b in range(batch):
        _store_merged(merged_ref.at[b], ao[b], ro[b], d_attn)

    @pl.when(c == nc - 1)
    def _emit_state():
        rout_ref[...] = r_ref[...]


def _mixer_sample_kernel(t, n_pairs, g_sz, n_heads, d_attn,
                         dchunk_ref, qa_ref, ka_ref, va_ref, qr_ref, kr_ref, vr_ref, gate_ref,
                         ck_ref, cv_ref, r0_ref,
                         sink_ref, dmat_ref, decin_ref, deckv_ref, gng_ref, gnb_ref,
                         merged_ref, rout_ref):
    k_all = jnp.concatenate([ck_ref[...], ka_ref[...].astype(BF16)], axis=0)
    v_all = jnp.concatenate([cv_ref[...], va_ref[...].astype(BF16)], axis=0)
    ao = _attend([(qa_ref[...], k_all, v_all)], None, sink_ref, t, n_pairs, g_sz)
    ro = _retain([(qr_ref[...], kr_ref[...], vr_ref[...], gate_ref[...], r0_ref, rout_ref)],
                 dmat_ref, decin_ref, deckv_ref, dchunk_ref, gng_ref[...], gnb_ref[...], n_heads)
    _store_merged(merged_ref, ao[0], ro[0], d_attn)


def _sink_table(sinks, sink_heads, t):
    per_row = jnp.repeat(sinks.astype(F32)[sink_heads], t, axis=1)
    return jnp.broadcast_to(per_row[:, :, None], per_row.shape + (LANES,))


def _mixer_prompt(proj_out, sinks, sink_heads, gng, gnb, batch, seq, n_kv, n_q, n_heads):
    qa, ka, va, qr, kr, vr, gate = proj_out
    t = CHUNK
    nc = seq // t
    d_attn, d_kv, d_ret = n_q * HEAD_DIM, n_kv * HEAD_DIM, n_heads * RET_HEAD_DIM
    n_pairs, g_sz = n_kv // 2, n_q // n_kv
    dmat, decin, deckv, dchunk = _decay_tables(n_heads, t)
    sink_tab = _sink_table(sinks, sink_heads, t)
    seq3 = lambda a: a.reshape(batch, seq, a.shape[-1])
    qa, ka, va, qr, kr, vr, gate = map(seq3, proj_out)
    row = lambda c, *_: (0, c, 0)
    back = lambda k: (lambda c, *_: (0, jnp.maximum(c - k, 0), 0))
    const3 = lambda c, *_: (0, 0, 0)
    const2 = lambda c, *_: (0, 0)
    n_win = WINDOW // t + 1
    kv_specs = [pl.BlockSpec((batch, t, d_kv), back(n_win - 1 - s)) for s in range(n_win)]
    state_shape = (batch, n_heads, RET_HEAD_DIM, RET_HEAD_DIM)
    grid_spec = pltpu.PrefetchScalarGridSpec(
        num_scalar_prefetch=1,
        grid=(nc,),
        in_specs=[
            pl.BlockSpec((batch, t, d_attn), row), *kv_specs, *kv_specs,
            pl.BlockSpec((batch, t, d_ret), row), pl.BlockSpec((batch, t, d_ret), row),
            pl.BlockSpec((batch, t, d_ret), row), pl.BlockSpec((batch, t, d_ret), row),
            pl.BlockSpec(sink_tab.shape, const3), pl.BlockSpec(dmat.shape, const3),
            pl.BlockSpec(decin.shape, const3), pl.BlockSpec(deckv.shape, const3),
            pl.BlockSpec((1, d_ret), const2), pl.BlockSpec((1, d_ret), const2),
        ],
        out_specs=[
            pl.BlockSpec((batch, t, d_attn + d_ret), row),
            pl.BlockSpec(state_shape, lambda c, *_: (0, 0, 0, 0)),
        ],
        scratch_shapes=[pltpu.VMEM(state_shape, F32)],
    )
    merged, r_fin = pl.pallas_call(
        functools.partial(_mixer_prompt_kernel, nc, t, batch, n_pairs, g_sz, n_heads, d_attn),
        grid_spec=grid_spec,
        out_shape=[jax.ShapeDtypeStruct((batch, seq, d_attn + d_ret), BF16),
                   jax.ShapeDtypeStruct(state_shape, F32)],
        compiler_params=_params("arbitrary"),
        name="mixer_prompt",
    )(dchunk, qa, *([ka] * n_win), *([va] * n_win), qr, kr, vr, gate, sink_tab, dmat, decin, deckv, gng, gnb)
    return merged.reshape(batch * seq, d_attn + d_ret), r_fin


def _mixer_sample(proj_out, ck, cv, r0, sinks, sink_heads, gng, gnb, row0, dbatch, t, n_kv, n_q, n_heads):
    qa, ka, va, qr, kr, vr, gate = proj_out
    d_attn, d_kv, d_ret = n_q * HEAD_DIM, n_kv * HEAD_DIM, n_heads * RET_HEAD_DIM
    n_pairs, g_sz = n_kv // 2, n_q // n_kv
    dmat, decin, deckv, dchunk = _decay_tables(n_heads, t)
    sink_tab = _sink_table(sinks, sink_heads, t)
    blk0 = row0 // t
    row = lambda b, *_: (blk0 + b, 0)
    const3 = lambda b, *_: (0, 0, 0)
    const2 = lambda b, *_: (0, 0)
    state_spec = pl.BlockSpec((None, n_heads, RET_HEAD_DIM, RET_HEAD_DIM), lambda b, *_: (b, 0, 0, 0))
    n_hist = ck.shape[1]
    grid_spec = pltpu.PrefetchScalarGridSpec(
        num_scalar_prefetch=1,
        grid=(dbatch,),
        in_specs=[
            pl.BlockSpec((t, d_attn), row), pl.BlockSpec((t, d_kv), row), pl.BlockSpec((t, d_kv), row),
            pl.BlockSpec((t, d_ret), row), pl.BlockSpec((t, d_ret), row), pl.BlockSpec((t, d_ret), row),
            pl.BlockSpec((t, d_ret), row),
            pl.BlockSpec((None, n_hist, d_kv), lambda b, *_: (b, 0, 0)),
            pl.BlockSpec((None, n_hist, d_kv), lambda b, *_: (b, 0, 0)),
            state_spec,
            pl.BlockSpec(sink_tab.shape, const3), pl.BlockSpec(dmat.shape, const3),
            pl.BlockSpec(decin.shape, const3), pl.BlockSpec(deckv.shape, const3),
            pl.BlockSpec((1, d_ret), const2), pl.BlockSpec((1, d_ret), const2),
        ],
        out_specs=[pl.BlockSpec((t, d_attn + d_ret), lambda b, *_: (b, 0)), state_spec],
    )
    return pl.pallas_call(
        functools.partial(_mixer_sample_kernel, t, n_pairs, g_sz, n_heads, d_attn),
        grid_spec=grid_spec,
        out_shape=[jax.ShapeDtypeStruct((dbatch * t, d_attn + d_ret), BF16),
                   jax.ShapeDtypeStruct(r0.shape, F32)],
        compiler_params=_params("parallel"),
        name="mixer_sample",
    )(dchunk, qa, ka, va, qr, kr, vr, gate, ck, cv, r0, sink_tab, dmat, decin, deckv, gng, gnb)


def _outproj_kernel(m_ref, w_ref, x_ref, o_ref):
    o_ref[...] = x_ref[...] + jnp.dot(m_ref[...], w_ref[...], preferred_element_type=F32)


def _outproj(merged, w, x):
    n, d = x.shape
    k = merged.shape[1]
    tm = _pick_tile(n, 1024)
    tn = _pick_tile(d, 1024, LANES)
    return pl.pallas_call(
        _outproj_kernel,
        grid=(n // tm, d // tn),
        in_specs=[pl.BlockSpec((tm, k), lambda i, j: (i, 0)),
                  pl.BlockSpec((k, tn), lambda i, j: (0, j)),
                  pl.BlockSpec((tm, tn), lambda i, j: (i, j))],
        out_specs=pl.BlockSpec((tm, tn), lambda i, j: (i, j)),
        out_shape=jax.ShapeDtypeStruct((n, d), F32),
        compiler_params=_params("parallel", "arbitrary"),
        name="outproj",
    )(merged, w, x)


def _swiglu_step(xn, wg, wu, wd):
    g = jnp.dot(xn, wg, preferred_element_type=F32)
    u = jnp.dot(xn, wu, preferred_element_type=F32)
    h = (g * jax.nn.sigmoid(g) * u).astype(BF16)
    return jnp.dot(h, wd, preferred_element_type=F32)


def _ffn_kernel(x_ref, ln_ref, wg_ref, wu_ref, wd_ref, o_ref, xn_ref):
    f = pl.program_id(1)

    @pl.when(f == 0)
    def _norm():
        x = x_ref[...]
        ms = jnp.mean(x * x, axis=-1, keepdims=True)
        xn_ref[...] = (x * lax.rsqrt(ms + RMS_EPS) * ln_ref[...]).astype(BF16)
        o_ref[...] = x

    o_ref[...] += _swiglu_step(xn_ref[...], wg_ref[...], wu_ref[...], wd_ref[...])


def _ffn(x, ln, wg, wu, wd):
    n, d = x.shape
    fp = wg.shape[1]
    tf = _pick_tile(fp, 512, LANES)
    tm = _pick_tile(n, 1024)
    return pl.pallas_call(
        _ffn_kernel,
        grid=(n // tm, fp // tf),
        in_specs=[pl.BlockSpec((tm, d), lambda i, f: (i, 0)),
                  pl.BlockSpec((1, d), lambda i, f: (0, 0)),
                  pl.BlockSpec((d, tf), lambda i, f: (0, f)),
                  pl.BlockSpec((d, tf), lambda i, f: (0, f)),
                  pl.BlockSpec((tf, d), lambda i, f: (f, 0))],
        out_specs=pl.BlockSpec((tm, d), lambda i, f: (i, 0)),
        out_shape=jax.ShapeDtypeStruct((n, d), F32),
        scratch_shapes=[pltpu.VMEM((tm, d), BF16)],
        compiler_params=_params("parallel", "arbitrary"),
        name="ffn",
    )(x, ln, wg, wu, wd)


def _cast_kernel(x_ref, o_ref):
    o_ref[...] = x_ref[...].astype(BF16)


def _cast_bf16(w):
    w2 = w.reshape(-1, w.shape[-1])
    r, c = w2.shape
    bc = _pick_tile(c, 2048, LANES)
    br = _pick_tile(r, max(16, (2 * 1024 * 1024) // bc))
    out = pl.pallas_call(
        _cast_kernel,
        grid=(r // br, c // bc),
        in_specs=[pl.BlockSpec((br, bc), lambda i, j: (i, j))],
        out_specs=pl.BlockSpec((br, bc), lambda i, j: (i, j)),
        out_shape=jax.ShapeDtypeStruct((r, c), BF16),
        compiler_params=_params("parallel", "parallel"),
        name="cast_bf16",
    )(w2)
    return out.reshape(w.shape)


def _router_kernel(n_exp, x_ref, ln_ref, wr_hi_ref, wr_lo_ref, hn_ref, idx_ref, gate_ref):
    x = x_ref[...]
    ms = jnp.mean(x * x, axis=-1, keepdims=True)
    hf = x * lax.rsqrt(ms + RMS_EPS) * ln_ref[...]
    hn = hf.astype(BF16)
    hn_ref[...] = hn
    h_lo = (hf - hn.astype(F32)).astype(BF16)
    logits = (jnp.dot(hn, wr_hi_ref[...], preferred_element_type=F32)
              + jnp.dot(h_lo, wr_hi_ref[...], preferred_element_type=F32)
              + jnp.dot(hn, wr_lo_ref[...], preferred_element_type=F32))
    lane = lax.broadcasted_iota(jnp.int32, logits.shape, 1)
    logits = jnp.where(lane < n_exp, logits, -jnp.inf)
    m1 = jnp.max(logits, axis=-1, keepdims=True)
    i1 = jnp.min(jnp.where(logits == m1, lane, LANES), axis=-1, keepdims=True)
    rest = jnp.where(lane == i1, -jnp.inf, logits)
    m2 = jnp.max(rest, axis=-1, keepdims=True)
    i2 = jnp.min(jnp.where(rest == m2, lane, LANES), axis=-1, keepdims=True)
    e2 = jnp.exp(m2 - m1)
    den = 1.0 + e2
    idx_ref[...] = jnp.where(lane == 0, i1, jnp.where(lane == 1, i2, 0))
    gate_ref[...] = jnp.where(lane == 0, 1.0 / den, jnp.where(lane == 1, e2 / den, 0.0))


def _router(x, ln, wr_hi, wr_lo, n_exp):
    n, d = x.shape
    tm = _pick_tile(n, 512)
    row = lambda i: (i, 0)
    return pl.pallas_call(
        functools.partial(_router_kernel, n_exp),
        grid=(n // tm,),
        in_specs=[pl.BlockSpec((tm, d), row), pl.BlockSpec((1, d), lambda i: (0, 0)),
                  pl.BlockSpec((d, LANES), lambda i: (0, 0)), pl.BlockSpec((d, LANES), lambda i: (0, 0))],
        out_specs=[pl.BlockSpec((tm, d), row), pl.BlockSpec((tm, LANES), row), pl.BlockSpec((tm, LANES), row)],
        out_shape=[jax.ShapeDtypeStruct((n, d), BF16), jax.ShapeDtypeStruct((n, LANES), jnp.int32),
                   jax.ShapeDtypeStruct((n, LANES), F32)],
        compiler_params=_params("parallel"),
        name="router",
    )(x, ln, wr_hi, wr_lo)


def _experts_kernel(nf, te_ref, act_ref, x_ref, wg_ref, wu_ref, wd_ref, o_ref, acc_ref):
    i = pl.program_id(0)
    f = pl.program_id(1)
    active = act_ref[i] == 1

    @pl.when(active & (f == 0))
    def _first():
        acc_ref[...] = _swiglu_step(x_ref[...], wg_ref[...], wu_ref[...], wd_ref[...])

    @pl.when(active & (f > 0))
    def _rest():
        acc_ref[...] += _swiglu_step(x_ref[...], wg_ref[...], wu_ref[...], wd_ref[...])

    @pl.when(active & (f == nf - 1))
    def _emit():
        o_ref[...] = acc_ref[...]

    @pl.when(jnp.logical_not(active) & (f == nf - 1))
    def _pad():
        o_ref[...] = jnp.zeros_like(o_ref)


def _experts(xs, wg, wu, wd, tile_expert, tile_active, tm):
    p, d = xs.shape
    n_exp, _, fe = wg.shape
    tf = _pick_tile(fe, 512, LANES)
    nf = fe // tf
    fidx = lambda i, f, te, act: f * act[i] + (nf - 1) * (1 - act[i])
    grid_spec = pltpu.PrefetchScalarGridSpec(
        num_scalar_prefetch=2,
        grid=(p // tm, nf),
        in_specs=[pl.BlockSpec((tm, d), lambda i, f, te, act: (i, 0)),
                  pl.BlockSpec((None, d, tf), lambda i, f, te, act: (te[i], 0, fidx(i, f, te, act))),
                  pl.BlockSpec((None, d, tf), lambda i, f, te, act: (te[i], 0, fidx(i, f, te, act))),
                  pl.BlockSpec((None, tf, d), lambda i, f, te, act: (te[i], fidx(i, f, te, act), 0))],
        out_specs=pl.BlockSpec((tm, d), lambda i, f, te, act: (i, 0)),
        scratch_shapes=[pltpu.VMEM((tm, d), F32)],
    )
    return pl.pallas_call(
        functools.partial(_experts_kernel, nf),
        grid_spec=grid_spec,
        out_shape=jax.ShapeDtypeStruct((p, d), F32),
        compiler_params=_params("parallel", "arbitrary"),
        name="experts",
    )(tile_expert, tile_active, xs, wg, wu, wd)


def _combine_kernel(x_ref, y0_ref, y1_ref, g_ref, o_ref):
    g = g_ref[...]
    o_ref[...] = x_ref[...] + (g[:, 0:1] * y0_ref[...] + g[:, 1:2] * y1_ref[...])


def _combine(x, y0, y1, gate, row0):
    n, d = x.shape
    tm = _pick_tile(n, 256)
    assert row0 % tm == 0
    blk0 = row0 // tm
    own = lambda i: (i, 0)
    shifted = lambda i: (blk0 + i, 0)
    return pl.pallas_call(
        _combine_kernel,
        grid=(n // tm,),
        in_specs=[pl.BlockSpec((tm, d), own), pl.BlockSpec((tm, d), shifted), pl.BlockSpec((tm, d), shifted),
                  pl.BlockSpec((tm, LANES), own)],
        out_specs=pl.BlockSpec((tm, d), own),
        out_shape=jax.ShapeDtypeStruct((n, d), F32),
        compiler_params=_params("parallel"),
        name="combine",
    )(x, y0, y1, gate)


def _moe(x_p, x_s, ln, w_router, wg, wu, wd):
    n_p, d = x_p.shape
    n = n_p + x_s.shape[0]
    n_exp = w_router.shape[1]
    wr = jnp.pad(w_router.astype(F32), ((0, 0), (0, LANES - n_exp)))
    wr_hi = wr.astype(BF16)
    wr_lo = (wr - wr_hi.astype(F32)).astype(BF16)
    hn_p, ridx_p, rgate_p = _router(x_p, ln, wr_hi, wr_lo, n_exp)
    hn_s, ridx_s, rgate_s = _router(x_s, ln, wr_hi, wr_lo, n_exp)
    hn = jnp.concatenate([hn_p, hn_s], axis=0)
    tm = 1024 if TOP_K * n >= 8192 else 64
    n_asg = TOP_K * n
    n_tiles = -(-(n_asg + n_exp * (tm - 1)) // tm)
    flat_e = jnp.concatenate([ridx_p[:, :TOP_K], ridx_s[:, :TOP_K]], axis=0).reshape(-1)
    onehot = (flat_e[:, None] == jnp.arange(n_exp, dtype=jnp.int32)[None, :]).astype(jnp.int32)
    csum = jnp.cumsum(onehot, axis=0)
    counts = csum[-1]
    rank = jnp.take_along_axis(csum, flat_e[:, None], axis=1)[:, 0] - 1
    padded = ((counts + tm - 1) // tm) * tm
    ends = jnp.cumsum(padded)
    pos = (ends - padded)[flat_e] + rank
    src_tok = jnp.zeros((n_tiles * tm,), jnp.int32).at[pos].set(jnp.arange(n_asg, dtype=jnp.int32) // TOP_K)
    tile_start = jnp.arange(n_tiles, dtype=jnp.int32) * tm
    te = jnp.sum((tile_start[:, None] >= ends[None, :]).astype(jnp.int32), axis=1)
    tile_active = (te < n_exp).astype(jnp.int32)
    tile_expert = jnp.minimum(te, n_exp - 1)
    xs = jnp.take(hn, src_tok, axis=0)
    ys = _experts(xs, _cast_bf16(wg), _cast_bf16(wu), _cast_bf16(wd), tile_expert, tile_active, tm)
    pos2 = pos.reshape(n, TOP_K)
    y0 = jnp.take(ys, pos2[:, 0], axis=0)
    y1 = jnp.take(ys, pos2[:, 1], axis=0)
    return _combine(x_p, y0, y1, rgate_p, 0), _combine(x_s, y0, y1, rgate_s, n_p)


def kernel(x_prompt, x_sample, cache_attn_k, cache_attn_v, state_ret, ln_mix, w_in, q_norm, k_norm, attn_sinks,
           ret_gn_g, ret_gn_b, w_out, ln_ffn, ffn_w_gate, ffn_w_up, ffn_w_down, moe_router, moe_w_gate, moe_w_up,
           moe_w_down):
    batch, seq, d = x_prompt.shape
    dbatch, t_s, _ = x_sample.shape
    depth = ln_mix.shape[0]
    n_kv = cache_attn_k.shape[3]
    n_heads = state_ret.shape[2]
    d_kv = n_kv * HEAD_DIM
    d_ret = n_heads * RET_HEAD_DIM
    d_attn = w_out.shape[1] - d_ret
    n_q = d_attn // HEAD_DIM
    assert cache_attn_k.shape[4] == HEAD_DIM and state_ret.shape[3] == RET_HEAD_DIM and n_kv % 2 == 0
    assert seq % CHUNK == 0 and WINDOW % CHUNK == 0 and seq >= WINDOW and cache_attn_k.shape[2] == WINDOW
    assert w_in.shape[2] == d_attn + 2 * d_kv + 4 * d_ret
    n_p, n_s = batch * seq, dbatch * t_s
    keep = min(WINDOW, seq)

    perm_q, perm_k, perm_ao, sink_heads = _layout_perms(n_q, n_kv)
    inv_k = np.argsort(perm_k)
    col_perm = np.concatenate([perm_q, d_attn + perm_k, np.arange(d_attn + d_kv, w_in.shape[2])])
    row_perm = np.concatenate([perm_ao, np.arange(d_attn, d_attn + d_ret)])

    tabs_p = _rope_tables(jnp.tile(jnp.arange(seq), batch))
    tabs_s = _rope_tables(PAST_LEN + jnp.tile(jnp.arange(t_s), dbatch))
    lane = np.arange(LANES)
    m128 = jnp.asarray(((lane[:, None] // HALF) % 2) == ((lane[None, :] // HALF) % 2), BF16)
    head_gain = lambda g: jnp.tile(g.astype(F32).reshape(2, HALF), (1, 2)).reshape(1, LANES)

    x_p, x_s = x_prompt.reshape(n_p, d), x_sample.reshape(n_s, d)
    nk_p, nv_p, nr_p, nk_s, nv_s, nr_s = [], [], [], [], [], []
    for l in range(depth):
        w_in_l = w_in[l][:, col_perm].astype(BF16)
        w_out_l = w_out[l][row_perm, :].astype(BF16)
        ln, gq, gk = ln_mix[l][None, :], head_gain(q_norm[l]), head_gain(k_norm[l])
        gng, gnb = ret_gn_g[l][None, :], ret_gn_b[l][None, :]
        proj_p = _proj(x_p, ln, w_in_l, m128, gq, gk, *tabs_p, d_attn, d_kv, d_ret, 512)
        proj_s = _proj(x_s, ln, w_in_l, m128, gq, gk, *tabs_s, d_attn, d_kv, d_ret, 256)
        merged_p, r_p = _mixer_prompt(proj_p, attn_sinks[l], sink_heads, gng, gnb, batch, seq, n_kv, n_q, n_heads)
        ck = cache_attn_k[l].reshape(dbatch, WINDOW, d_kv)[:, :, perm_k].astype(BF16)
        cv = cache_attn_v[l].reshape(dbatch, WINDOW, d_kv).astype(BF16)
        merged_s, r_s = _mixer_sample(proj_s, ck, cv, state_ret[l], attn_sinks[l], sink_heads, gng, gnb,
                                      0, dbatch, t_s, n_kv, n_q, n_heads)
        x_p = _outproj(merged_p, w_out_l, x_p)
        x_s = _outproj(merged_s, w_out_l, x_s)

        ka_last = proj_p[1].reshape(batch, seq, d_kv)[:, seq - keep:]
        nk_p.append(ka_last[:, :, inv_k].reshape(batch, keep, n_kv, HEAD_DIM))
        nv_p.append(proj_p[2].reshape(batch, seq, n_kv, HEAD_DIM)[:, seq - keep:])
        nk_s.append(proj_s[1][:, inv_k].reshape(dbatch, t_s, n_kv, HEAD_DIM))
        nv_s.append(proj_s[2].reshape(dbatch, t_s, n_kv, HEAD_DIM))
        nr_p.append(r_p)
        nr_s.append(r_s)

        i = l // 2
        lf = ln_ffn[l][None, :]
        if l % 2 == 0:
            fpad = (-ffn_w_gate.shape[2]) % COL_TILE
            wg = jnp.pad(ffn_w_gate[i].astype(BF16), ((0, 0), (0, fpad)))
            wu = jnp.pad(ffn_w_up[i].astype(BF16), ((0, 0), (0, fpad)))
            wd = jnp.pad(ffn_w_down[i].astype(BF16), ((0, fpad), (0, 0)))
            x_p, x_s = _ffn(x_p, lf, wg, wu, wd), _ffn(x_s, lf, wg, wu, wd)
        else:
            x_p, x_s = _moe(x_p, x_s, lf, moe_router[i], moe_w_gate[i], moe_w_up[i], moe_w_down[i])

    return (x_p.reshape(batch, seq, d), x_s.reshape(dbatch, t_s, d),
            jnp.stack(nk_p), jnp.stack(nv_p), jnp.stack(nr_p), jnp.stack(nk_s), jnp.stack(nv_s), jnp.stack(nr_s))
```

```python
import functools

import numpy as np
import jax
import jax.numpy as jnp
from jax import lax
from jax.experimental import pallas as pl
from jax.experimental.pallas import tpu as pltpu

CHUNK = 64
WINDOW = 128
HEAD_DIM = 64
RET_HEAD_DIM = 128
PAST_LEN = 2048
ROPE_THETA = 10000.0
TOP_K = 2
RMS_EPS = 1e-6
GN_EPS = 1e-5

LANES = 128
HALF = HEAD_DIM // 2
VMEM_LIMIT = 56 * 1024 * 1024
COL_TILE = 512
F32 = jnp.float32
BF16 = jnp.bfloat16


def _pick_tile(n, target, mult=16):
    best = None
    for t in range(mult, min(n, target) + 1, mult):
        if n % t == 0:
            best = t
    assert best is not None, (n, target)
    return best


def _params(*sem):
    return pltpu.CompilerParams(dimension_semantics=sem, vmem_limit_bytes=VMEM_LIMIT)


def _layout_perms(n_q, n_kv):
    g_sz = n_q // n_kv
    n_pairs = n_kv // 2
    perm_k = np.zeros(n_kv * HEAD_DIM, np.int32)
    perm_q = np.zeros(n_q * HEAD_DIM, np.int32)
    perm_ao = np.zeros(n_q * HEAD_DIM, np.int32)
    sink_heads = np.zeros((n_pairs, 2 * g_sz), np.int32)
    for p in range(n_pairs):
        for hf in range(2):
            for s in range(2):
                for r in range(HALF):
                    perm_k[p * LANES + hf * 64 + s * HALF + r] = (2 * p + s) * HEAD_DIM + hf * HALF + r
        for g in range(g_sz):
            bq = p * g_sz + g
            for s in range(2):
                head = g_sz * (2 * p + s) + g
                sink_heads[p, 2 * g + s] = head
                for hf in range(2):
                    for r in range(HALF):
                        perm_q[bq * LANES + hf * 64 + s * HALF + r] = head * HEAD_DIM + hf * HALF + r
                for d in range(HEAD_DIM):
                    perm_ao[bq * LANES + s * HEAD_DIM + d] = head * HEAD_DIM + d
    return perm_q, perm_k, perm_ao, sink_heads


def _rope_tables(pos):
    pf = pos.astype(F32)[:, None]
    inv_a = ROPE_THETA ** (-jnp.arange(0, HEAD_DIM, 2, dtype=F32) / HEAD_DIM)
    ang = pf * inv_a[None, :]
    c, s = jnp.cos(ang), jnp.sin(ang)
    ca = jnp.concatenate([c, c, c, c], axis=-1)
    sa = jnp.concatenate([-s, -s, s, s], axis=-1)
    inv_r = 10000.0 ** (-jnp.linspace(0.0, 1.0, RET_HEAD_DIM // 2, dtype=F32))
    ang = pf * inv_r[None, :]
    c, s = jnp.cos(ang), jnp.sin(ang)
    cr = jnp.concatenate([c, c], axis=-1)
    sr = jnp.concatenate([-s, s], axis=-1)
    return ca, sa, cr, sr


def _decay_tables(n_heads, chunk):
    log_g = jnp.log1p(-jnp.exp2(-5.0 - jnp.arange(n_heads, dtype=F32)))
    idx = jnp.arange(chunk, dtype=F32)
    dmat = jnp.exp(log_g[:, None, None] * jnp.abs(idx[:, None] - idx[None, :]))
    dec_in = jnp.exp(log_g[None, :] * (idx[:, None] + 1.0))
    dec_kv = jnp.exp(log_g[None, :] * (chunk - 1.0 - idx[:, None]))
    dec_chunk = jnp.exp(log_g * chunk)
    bc = lambda t: jnp.broadcast_to(t.T[:, :, None], (n_heads, chunk, LANES))
    return dmat, bc(dec_in), bc(dec_kv), dec_chunk


def _rope128(y, c, s):
    return y * c + pltpu.roll(y, 64, 1) * s


def _proj_kernel(nq, nr, x_ref, ln_ref, w_ref, m_ref, gq_ref, gk_ref, ca_ref, sa_ref, cr_ref, sr_ref,
                 qa_ref, ka_ref, va_ref, qr_ref, kr_ref, vr_ref, gate_ref, xn_ref):
    j = pl.program_id(1)

    @pl.when(j == 0)
    def _norm():
        x = x_ref[...]
        ms = jnp.mean(x * x, axis=-1, keepdims=True)
        xn_ref[...] = (x * lax.rsqrt(ms + RMS_EPS) * ln_ref[...]).astype(BF16)

    y = jnp.dot(xn_ref[...], w_ref[...], preferred_element_type=F32)
    n_sub = COL_TILE // LANES

    def head_norm_rope(yc, g):
        y2 = yc * yc
        hi = y2.astype(BF16)
        lo = (y2 - hi.astype(F32)).astype(BF16)
        ss = (jnp.dot(hi, m_ref[...], preferred_element_type=F32)
              + jnp.dot(lo, m_ref[...], preferred_element_type=F32))
        yn = yc * lax.rsqrt(ss * (1.0 / HEAD_DIM) + RMS_EPS) * g
        return _rope128(yn, ca_ref[...], sa_ref[...])

    @pl.when(j < nq)
    def _qa():
        for c in range(n_sub):
            sl = slice(c * LANES, (c + 1) * LANES)
            qa_ref[:, sl] = (head_norm_rope(y[:, sl], gq_ref[...]) * (HEAD_DIM ** -0.5)).astype(BF16)

    @pl.when(j == nq)
    def _kv():
        half = COL_TILE // 2
        for c in range(half // LANES):
            sl = slice(c * LANES, (c + 1) * LANES)
            ka_ref[:, sl] = head_norm_rope(y[:, sl], gk_ref[...])
        va_ref[...] = y[:, half:]

    @pl.when((j > nq) & (j <= nq + nr))
    def _qr():
        for c in range(n_sub):
            sl = slice(c * LANES, (c + 1) * LANES)
            qr_ref[:, sl] = _rope128(y[:, sl], cr_ref[...], sr_ref[...]).astype(BF16)

    @pl.when((j > nq + nr) & (j <= nq + 2 * nr))
    def _kr():
        for c in range(n_sub):
            sl = slice(c * LANES, (c + 1) * LANES)
            kr_ref[:, sl] = _rope128(y[:, sl], cr_ref[...], sr_ref[...]) * (RET_HEAD_DIM ** -0.5)

    @pl.when((j > nq + 2 * nr) & (j <= nq + 3 * nr))
    def _vr():
        vr_ref[...] = y.astype(BF16)

    @pl.when(j > nq + 3 * nr)
    def _gate():
        gate_ref[...] = y


def _proj(x, ln, w, m128, gq, gk, ca, sa, cr, sr, d_attn, d_kv, d_ret, tm_target):
    n, d = x.shape
    assert d_attn % COL_TILE == 0 and 2 * d_kv == COL_TILE and d_ret % COL_TILE == 0
    nq, nr = d_attn // COL_TILE, d_ret // COL_TILE
    n_col = nq + 1 + 4 * nr
    tm = _pick_tile(n, tm_target)
    row = lambda i, j: (i, 0)
    clip = lambda lo, cnt: (lambda i, j: (i, jnp.clip(j - lo, 0, cnt - 1)))
    tab = pl.BlockSpec((tm, LANES), row)
    vec = pl.BlockSpec((1, LANES), lambda i, j: (0, 0))
    return pl.pallas_call(
        functools.partial(_proj_kernel, nq, nr),
        grid=(n // tm, n_col),
        in_specs=[
            pl.BlockSpec((tm, d), row),
            pl.BlockSpec((1, d), lambda i, j: (0, 0)),
            pl.BlockSpec((d, COL_TILE), lambda i, j: (0, j)),
            pl.BlockSpec((LANES, LANES), lambda i, j: (0, 0)),
            vec, vec, tab, tab, tab, tab,
        ],
        out_specs=[
            pl.BlockSpec((tm, COL_TILE), clip(0, nq)),
            pl.BlockSpec((tm, d_kv), row),
            pl.BlockSpec((tm, d_kv), row),
            pl.BlockSpec((tm, COL_TILE), clip(nq + 1, nr)),
            pl.BlockSpec((tm, COL_TILE), clip(nq + 1 + nr, nr)),
            pl.BlockSpec((tm, COL_TILE), clip(nq + 1 + 2 * nr, nr)),
            pl.BlockSpec((tm, COL_TILE), clip(nq + 1 + 3 * nr, nr)),
        ],
        out_shape=[
            jax.ShapeDtypeStruct((n, d_attn), BF16),
            jax.ShapeDtypeStruct((n, d_kv), F32),
            jax.ShapeDtypeStruct((n, d_kv), F32),
            jax.ShapeDtypeStruct((n, d_ret), BF16),
            jax.ShapeDtypeStruct((n, d_ret), F32),
            jax.ShapeDtypeStruct((n, d_ret), BF16),
            jax.ShapeDtypeStruct((n, d_ret), F32),
        ],
        scratch_shapes=[pltpu.VMEM((tm, d), BF16)],
        compiler_params=_params("parallel", "arbitrary"),
        name="proj",
    )(x, ln, w, m128, gq, gk, ca, sa, cr, sr)


def _attend(items, valid, sink_ref, t, n_pairs, g_sz):
    lane = lax.broadcasted_iota(jnp.int32, (1, LANES), 1)
    slot_a = ((lane // HALF) % 2) == 0
    first_half = lane < HEAD_DIM
    zero = jnp.zeros((), BF16)
    jobs = [(i, p) for i in range(len(items)) for p in range(n_pairs)]
    scores = []
    for i, p in jobs:
        qa, k_all, _ = items[i]
        parts = []
        for g in range(g_sz):
            qb = qa[:, (p * g_sz + g) * LANES:(p * g_sz + g + 1) * LANES]
            parts.append(jnp.where(slot_a, qb, zero))
            parts.append(jnp.where(slot_a, zero, qb))
        qs = jnp.concatenate(parts, axis=0)
        scores.append(lax.dot_general(qs, k_all[:, p * LANES:(p + 1) * LANES], (((1,), (1,)), ((), ())),
                                      preferred_element_type=F32))
    probs = []
    for (i, p), s in zip(jobs, scores):
        if valid is not None:
            s = jnp.where(valid, s, -jnp.inf)
        sk = sink_ref[p][:, 0:1]
        m = jnp.maximum(jnp.max(s, axis=-1, keepdims=True), sk)
        e = jnp.exp(s - m)
        den = jnp.sum(e, axis=-1, keepdims=True) + jnp.exp(sk - m)
        probs.append((e / den).astype(BF16))
    outs = [[] for _ in items]
    for (i, p), pn in zip(jobs, probs):
        o = jnp.dot(pn, items[i][2][:, p * LANES:(p + 1) * LANES], preferred_element_type=F32)
        for g in range(g_sz):
            o_a = o[(2 * g) * t:(2 * g + 1) * t]
            o_b = o[(2 * g + 1) * t:(2 * g + 2) * t]
            outs[i].append(jnp.where(first_half, o_a, o_b))
    return outs


def _retain(items, dmat_ref, decin_ref, deckv_ref, dchunk_ref, gng, gnb, n_heads):
    nt = (((1,), (1,)), ((), ()))
    tn = (((0,), (0,)), ((), ()))
    jobs = [(i, h) for i in range(len(items)) for h in range(n_heads)]
    sl = lambda h: slice(h * RET_HEAD_DIM, (h + 1) * RET_HEAD_DIM)
    intra, inter, r_prevs = [], [], []
    for i, h in jobs:
        qr, kr, _, _, r_in_ref, _ = items[i]
        q = qr[:, sl(h)]
        r_prev = r_in_ref[h]
        r_prevs.append(r_prev)
        intra.append(lax.dot_general(q, kr[:, sl(h)].astype(BF16), nt, preferred_element_type=F32) * dmat_ref[h])
        inter.append(jnp.dot(q, r_prev.astype(BF16), preferred_element_type=F32) * decin_ref[h])
    for (i, h), r_prev in zip(jobs, r_prevs):
        _, kr, vr, _, _, r_out_ref = items[i]
        kd = (kr[:, sl(h)] * deckv_ref[h]).astype(BF16)
        kv = lax.dot_general(kd, vr[:, sl(h)], tn, preferred_element_type=F32)
        r_out_ref[h] = r_prev * dchunk_ref[h] + kv
    outs = [[] for _ in items]
    for (i, h), s, o_inter in zip(jobs, intra, inter):
        _, _, vr, gate, _, _ = items[i]
        o = jnp.dot(s.astype(BF16), vr[:, sl(h)], preferred_element_type=F32) + o_inter
        mu = jnp.mean(o, axis=-1, keepdims=True)
        dev = o - mu
        var = jnp.mean(dev * dev, axis=-1, keepdims=True)
        r = dev * lax.rsqrt(var + GN_EPS) * gng[:, sl(h)] + gnb[:, sl(h)]
        gt = gate[:, sl(h)]
        outs[i].append(gt * jax.nn.sigmoid(gt) * r)
    return outs


def _store_merged(merged_ref, ao, ro, d_attn):
    for b, blk in enumerate(ao):
        merged_ref[:, b * LANES:(b + 1) * LANES] = blk.astype(BF16)
    for h, blk in enumerate(ro):
        merged_ref[:, d_attn + h * RET_HEAD_DIM:d_attn + (h + 1) * RET_HEAD_DIM] = blk.astype(BF16)


def _mixer_prompt_kernel(nc, t, batch, n_pairs, g_sz, n_heads, d_attn,
                         dchunk_ref, qa_ref, *rest):
    n_win = WINDOW // t + 1
    k_refs, v_refs = rest[:n_win], rest[n_win:2 * n_win]
    (qr_ref, kr_ref, vr_ref, gate_ref, sink_ref, dmat_ref, decin_ref, deckv_ref, gng_ref, gnb_ref,
     merged_ref, rout_ref, r_ref) = rest[2 * n_win:]
    c = pl.program_id(0)

    @pl.when(c == 0)
    def _reset():
        r_ref[...] = jnp.zeros_like(r_ref)

    kidx = lax.broadcasted_iota(jnp.int32, (1, n_win * t), 1)
    valid = kidx >= (n_win - 1 - c) * t
    att_items = [(qa_ref[b],
                  jnp.concatenate([r[b].astype(BF16) for r in k_refs], axis=0),
                  jnp.concatenate([r[b].astype(BF16) for r in v_refs], axis=0)) for b in range(batch)]
    ret_items = [(qr_ref[b], kr_ref[b], vr_ref[b], gate_ref[b], r_ref.at[b], r_ref.at[b]) for b in range(batch)]
    ao = _attend(att_items, valid, sink_ref, t, n_pairs, g_sz)
    ro = _retain(ret_items, dmat_ref, decin_ref, deckv_ref, dchunk_ref, gng_ref[...], gnb_ref[...], n_heads)
    for b in range(batch):
        _store_merged(merged_ref.at[b], ao[b], ro[b], d_attn)

    @pl.when(c == nc - 1)
    def _emit_state():
        rout_ref[...] = r_ref[...]


def _mixer_sample_kernel(t, n_pairs, g_sz, n_heads, d_attn,
                         dchunk_ref, qa_ref, ka_ref, va_ref, qr_ref, kr_ref, vr_ref, gate_ref,
                         ck_ref, cv_ref, r0_ref,
                         sink_ref, dmat_ref, decin_ref, deckv_ref, gng_ref, gnb_ref,
                         merged_ref, rout_ref):
    k_all = jnp.concatenate([ck_ref[...], ka_ref[...].astype(BF16)], axis=0)
    v_all = jnp.concatenate([cv_ref[...], va_ref[...].astype(BF16)], axis=0)
    ao = _attend([(qa_ref[...], k_all, v_all)], None, sink_ref, t, n_pairs, g_sz)
    ro = _retain([(qr_ref[...], kr_ref[...], vr_ref[...], gate_ref[...], r0_ref, rout_ref)],
                 dmat_ref, decin_ref, deckv_ref, dchunk_ref, gng_ref[...], gnb_ref[...], n_heads)
    _store_merged(merged_ref, ao[0], ro[0], d_attn)


def _sink_table(sinks, sink_heads, t):
    per_row = jnp.repeat(sinks.astype(F32)[sink_heads], t, axis=1)
    return jnp.broadcast_to(per_row[:, :, None], per_row.shape + (LANES,))


def _mixer_prompt(proj_out, sinks, sink_heads, gng, gnb, batch, seq, n_kv, n_q, n_heads):
    qa, ka, va, qr, kr, vr, gate = proj_out
    t = CHUNK
    nc = seq // t
    d_attn, d_kv, d_ret = n_q * HEAD_DIM, n_kv * HEAD_DIM, n_heads * RET_HEAD_DIM
    n_pairs, g_sz = n_kv // 2, n_q // n_kv
    dmat, decin, deckv, dchunk = _decay_tables(n_heads, t)
    sink_tab = _sink_table(sinks, sink_heads, t)
    seq3 = lambda a: a.reshape(batch, seq, a.shape[-1])
    qa, ka, va, qr, kr, vr, gate = map(seq3, proj_out)
    row = lambda c, *_: (0, c, 0)
    back = lambda k: (lambda c, *_: (0, jnp.maximum(c - k, 0), 0))
    const3 = lambda c, *_: (0, 0, 0)
    const2 = lambda c, *_: (0, 0)
    n_win = WINDOW // t + 1
    kv_specs = [pl.BlockSpec((batch, t, d_kv), back(n_win - 1 - s)) for s in range(n_win)]
    state_shape = (batch, n_heads, RET_HEAD_DIM, RET_HEAD_DIM)
    grid_spec = pltpu.PrefetchScalarGridSpec(
        num_scalar_prefetch=1,
        grid=(nc,),
        in_specs=[
            pl.BlockSpec((batch, t, d_attn), row), *kv_specs, *kv_specs,
            pl.BlockSpec((batch, t, d_ret), row), pl.BlockSpec((batch, t, d_ret), row),
            pl.BlockSpec((batch, t, d_ret), row), pl.BlockSpec((batch, t, d_ret), row),
            pl.BlockSpec(sink_tab.shape, const3), pl.BlockSpec(dmat.shape, const3),
            pl.BlockSpec(decin.shape, const3), pl.BlockSpec(deckv.shape, const3),
            pl.BlockSpec((1, d_ret), const2), pl.BlockSpec((1, d_ret), const2),
        ],
        out_specs=[
            pl.BlockSpec((batch, t, d_attn + d_ret), row),
            pl.BlockSpec(state_shape, lambda c, *_: (0, 0, 0, 0)),
        ],
        scratch_shapes=[pltpu.VMEM(state_shape, F32)],
    )
    merged, r_fin = pl.pallas_call(
        functools.partial(_mixer_prompt_kernel, nc, t, batch, n_pairs, g_sz, n_heads, d_attn),
        grid_spec=grid_spec,
        out_shape=[jax.ShapeDtypeStruct((batch, seq, d_attn + d_ret), BF16),
                   jax.ShapeDtypeStruct(state_shape, F32)],
        compiler_params=_params("arbitrary"),
        name="mixer_prompt",
    )(dchunk, qa, *([ka] * n_win), *([va] * n_win), qr, kr, vr, gate, sink_tab, dmat, decin, deckv, gng, gnb)
    return merged.reshape(batch * seq, d_attn + d_ret), r_fin


def _mixer_sample(proj_out, ck, cv, r0, sinks, sink_heads, gng, gnb, row0, dbatch, t, n_kv, n_q, n_heads):
    qa, ka, va, qr, kr, vr, gate = proj_out
    d_attn, d_kv, d_ret = n_q * HEAD_DIM, n_kv * HEAD_DIM, n_heads * RET_HEAD_DIM
    n_pairs, g_sz = n_kv // 2, n_q // n_kv
    dmat, decin, deckv, dchunk = _decay_tables(n_heads, t)
    sink_tab = _sink_table(sinks, sink_heads, t)
    blk0 = row0 // t
    row = lambda b, *_: (blk0 + b, 0)
    const3 = lambda b, *_: (0, 0, 0)
    const2 = lambda b, *_: (0, 0)
    state_spec = pl.BlockSpec((None, n_heads, RET_HEAD_DIM, RET_HEAD_DIM), lambda b, *_: (b, 0, 0, 0))
    n_hist = ck.shape[1]
    grid_spec = pltpu.PrefetchScalarGridSpec(
        num_scalar_prefetch=1,
        grid=(dbatch,),
        in_specs=[
            pl.BlockSpec((t, d_attn), row), pl.BlockSpec((t, d_kv), row), pl.BlockSpec((t, d_kv), row),
            pl.BlockSpec((t, d_ret), row), pl.BlockSpec((t, d_ret), row), pl.BlockSpec((t, d_ret), row),
            pl.BlockSpec((t, d_ret), row),
            pl.BlockSpec((None, n_hist, d_kv), lambda b, *_: (b, 0, 0)),
            pl.BlockSpec((None, n_hist, d_kv), lambda b, *_: (b, 0, 0)),
            state_spec,
            pl.BlockSpec(sink_tab.shape, const3), pl.BlockSpec(dmat.shape, const3),
            pl.BlockSpec(decin.shape, const3), pl.BlockSpec(deckv.shape, const3),
            pl.BlockSpec((1, d_ret), const2), pl.BlockSpec((1, d_ret), const2),
        ],
        out_specs=[pl.BlockSpec((t, d_attn + d_ret), lambda b, *_: (b, 0)), state_spec],
    )
    return pl.pallas_call(
        functools.partial(_mixer_sample_kernel, t, n_pairs, g_sz, n_heads, d_attn),
        grid_spec=grid_spec,
        out_shape=[jax.ShapeDtypeStruct((dbatch * t, d_attn + d_ret), BF16),
                   jax.ShapeDtypeStruct(r0.shape, F32)],
        compiler_params=_params("parallel"),
        name="mixer_sample",
    )(dchunk, qa, ka, va, qr, kr, vr, gate, ck, cv, r0, sink_tab, dmat, decin, deckv, gng, gnb)


def _outproj_kernel(m_ref, w_ref, x_ref, o_ref):
    o_ref[...] = x_ref[...] + jnp.dot(m_ref[...], w_ref[...], preferred_element_type=F32)


def _outproj(merged, w, x):
    n, d = x.shape
    k = merged.shape[1]
    tm = _pick_tile(n, 1024)
    tn = _pick_tile(d, 1024, LANES)
    return pl.pallas_call(
        _outproj_kernel,
        grid=(n // tm, d // tn),
        in_specs=[pl.BlockSpec((tm, k), lambda i, j: (i, 0)),
                  pl.BlockSpec((k, tn), lambda i, j: (0, j)),
                  pl.BlockSpec((tm, tn), lambda i, j: (i, j))],
        out_specs=pl.BlockSpec((tm, tn), lambda i, j: (i, j)),
        out_shape=jax.ShapeDtypeStruct((n, d), F32),
        compiler_params=_params("parallel", "arbitrary"),
        name="outproj",
    )(merged, w, x)


def _swiglu_step(xn, wg, wu, wd):
    g = jnp.dot(xn, wg, preferred_element_type=F32)
    u = jnp.dot(xn, wu, preferred_element_type=F32)
    h = (g * jax.nn.sigmoid(g) * u).astype(BF16)
    return jnp.dot(h, wd, preferred_element_type=F32)


def _ffn_kernel(x_ref, ln_ref, wg_ref, wu_ref, wd_ref, o_ref, xn_ref):
    f = pl.program_id(1)

    @pl.when(f == 0)
    def _norm():
        x = x_ref[...]
        ms = jnp.mean(x * x, axis=-1, keepdims=True)
        xn_ref[...] = (x * lax.rsqrt(ms + RMS_EPS) * ln_ref[...]).astype(BF16)
        o_ref[...] = x

    o_ref[...] += _swiglu_step(xn_ref[...], wg_ref[...], wu_ref[...], wd_ref[...])


def _ffn(x, ln, wg, wu, wd):
    n, d = x.shape
    fp = wg.shape[1]
    tf = _pick_tile(fp, 512, LANES)
    tm = _pick_tile(n, 1024)
    return pl.pallas_call(
        _ffn_kernel,
        grid=(n // tm, fp // tf),
        in_specs=[pl.BlockSpec((tm, d), lambda i, f: (i, 0)),
                  pl.BlockSpec((1, d), lambda i, f: (0, 0)),
                  pl.BlockSpec((d, tf), lambda i, f: (0, f)),
                  pl.BlockSpec((d, tf), lambda i, f: (0, f)),
                  pl.BlockSpec((tf, d), lambda i, f: (f, 0))],
        out_specs=pl.BlockSpec((tm, d), lambda i, f: (i, 0)),
        out_shape=jax.ShapeDtypeStruct((n, d), F32),
        scratch_shapes=[pltpu.VMEM((tm, d), BF16)],
        compiler_params=_params("parallel", "arbitrary"),
        name="ffn",
    )(x, ln, wg, wu, wd)


def _router_kernel(n_exp, x_ref, ln_ref, wr_hi_ref, wr_lo_ref, hn_ref, idx_ref, gate_ref):
    x = x_ref[...]
    ms = jnp.mean(x * x, axis=-1, keepdims=True)
    hf = x * lax.rsqrt(ms + RMS_EPS) * ln_ref[...]
    hn = hf.astype(BF16)
    hn_ref[...] = hn
    h_lo = (hf - hn.astype(F32)).astype(BF16)
    logits = (jnp.dot(hn, wr_hi_ref[...], preferred_element_type=F32)
              + jnp.dot(h_lo, wr_hi_ref[...], preferred_element_type=F32)
              + jnp.dot(hn, wr_lo_ref[...], preferred_element_type=F32))
    lane = lax.broadcasted_iota(jnp.int32, logits.shape, 1)
    logits = jnp.where(lane < n_exp, logits, -jnp.inf)
    m1 = jnp.max(logits, axis=-1, keepdims=True)
    i1 = jnp.min(jnp.where(logits == m1, lane, LANES), axis=-1, keepdims=True)
    rest = jnp.where(lane == i1, -jnp.inf, logits)
    m2 = jnp.max(rest, axis=-1, keepdims=True)
    i2 = jnp.min(jnp.where(rest == m2, lane, LANES), axis=-1, keepdims=True)
    e2 = jnp.exp(m2 - m1)
    den = 1.0 + e2
    idx_ref[...] = jnp.where(lane == 0, i1, jnp.where(lane == 1, i2, 0))
    gate_ref[...] = jnp.where(lane == 0, 1.0 / den, jnp.where(lane == 1, e2 / den, 0.0))


def _router(x, ln, wr_hi, wr_lo, n_exp):
    n, d = x.shape
    tm = _pick_tile(n, 512)
    row = lambda i: (i, 0)
    return pl.pallas_call(
        functools.partial(_router_kernel, n_exp),
        grid=(n // tm,),
        in_specs=[pl.BlockSpec((tm, d), row), pl.BlockSpec((1, d), lambda i: (0, 0)),
                  pl.BlockSpec((d, LANES), lambda i: (0, 0)), pl.BlockSpec((d, LANES), lambda i: (0, 0))],
        out_specs=[pl.BlockSpec((tm, d), row), pl.BlockSpec((tm, LANES), row), pl.BlockSpec((tm, LANES), row)],
        out_shape=[jax.ShapeDtypeStruct((n, d), BF16), jax.ShapeDtypeStruct((n, LANES), jnp.int32),
                   jax.ShapeDtypeStruct((n, LANES), F32)],
        compiler_params=_params("parallel"),
        name="router",
    )(x, ln, wr_hi, wr_lo)


def _experts_kernel(te_ref, act_ref, x_ref, wg_ref, wu_ref, wd_ref, o_ref):
    i = pl.program_id(0)
    f = pl.program_id(1)

    @pl.when(f == 0)
    def _init():
        o_ref[...] = jnp.zeros_like(o_ref)

    @pl.when(act_ref[i] == 1)
    def _step():
        o_ref[...] += _swiglu_step(x_ref[...], wg_ref[...].astype(BF16), wu_ref[...].astype(BF16),
                                   wd_ref[...].astype(BF16))


def _experts(xs, wg, wu, wd, tile_expert, tile_active, tm):
    p, d = xs.shape
    n_exp, _, fe = wg.shape
    tf = _pick_tile(fe, 512, LANES)
    nf = fe // tf
    fidx = lambda i, f, te, act: f * act[i] + (nf - 1) * (1 - act[i])
    once = pl.Buffered(1)
    grid_spec = pltpu.PrefetchScalarGridSpec(
        num_scalar_prefetch=2,
        grid=(p // tm, nf),
        in_specs=[pl.BlockSpec((tm, d), lambda i, f, te, act: (i, 0), pipeline_mode=once),
                  pl.BlockSpec((None, d, tf), lambda i, f, te, act: (te[i], 0, fidx(i, f, te, act))),
                  pl.BlockSpec((None, d, tf), lambda i, f, te, act: (te[i], 0, fidx(i, f, te, act))),
                  pl.BlockSpec((None, tf, d), lambda i, f, te, act: (te[i], fidx(i, f, te, act), 0))],
        out_specs=pl.BlockSpec((tm, d), lambda i, f, te, act: (i, 0), pipeline_mode=once),
    )
    return pl.pallas_call(
        _experts_kernel,
        grid_spec=grid_spec,
        out_shape=jax.ShapeDtypeStruct((p, d), F32),
        compiler_params=_params("parallel", "arbitrary"),
        name="experts",
    )(tile_expert, tile_active, xs, wg, wu, wd)


def _combine_kernel(x_ref, y0_ref, y1_ref, g_ref, o_ref):
    g = g_ref[...]
    o_ref[...] = x_ref[...] + (g[:, 0:1] * y0_ref[...] + g[:, 1:2] * y1_ref[...])


def _combine(x, y0, y1, gate, row0):
    n, d = x.shape
    tm = _pick_tile(n, 256)
    assert row0 % tm == 0
    blk0 = row0 // tm
    own = lambda i: (i, 0)
    shifted = lambda i: (blk0 + i, 0)
    return pl.pallas_call(
        _combine_kernel,
        grid=(n // tm,),
        in_specs=[pl.BlockSpec((tm, d), own), pl.BlockSpec((tm, d), shifted), pl.BlockSpec((tm, d), shifted),
                  pl.BlockSpec((tm, LANES), own)],
        out_specs=pl.BlockSpec((tm, d), own),
        out_shape=jax.ShapeDtypeStruct((n, d), F32),
        compiler_params=_params("parallel"),
        name="combine",
    )(x, y0, y1, gate)


def _moe(x_p, x_s, ln, w_router, wg, wu, wd):
    n_p, d = x_p.shape
    n = n_p + x_s.shape[0]
    n_exp = w_router.shape[1]
    wr = jnp.pad(w_router.astype(F32), ((0, 0), (0, LANES - n_exp)))
    wr_hi = wr.astype(BF16)
    wr_lo = (wr - wr_hi.astype(F32)).astype(BF16)
    hn_p, ridx_p, rgate_p = _router(x_p, ln, wr_hi, wr_lo, n_exp)
    hn_s, ridx_s, rgate_s = _router(x_s, ln, wr_hi, wr_lo, n_exp)
    hn = jnp.concatenate([hn_p, hn_s], axis=0)
    tm = 1024 if TOP_K * n >= 8192 else 64
    n_asg = TOP_K * n
    n_tiles = -(-(n_asg + n_exp * (tm - 1)) // tm)
    flat_e = jnp.concatenate([ridx_p[:, :TOP_K], ridx_s[:, :TOP_K]], axis=0).reshape(-1)
    onehot = (flat_e[:, None] == jnp.arange(n_exp, dtype=jnp.int32)[None, :]).astype(jnp.int32)
    csum = jnp.cumsum(onehot, axis=0)
    counts = csum[-1]
    rank = jnp.take_along_axis(csum, flat_e[:, None], axis=1)[:, 0] - 1
    padded = ((counts + tm - 1) // tm) * tm
    ends = jnp.cumsum(padded)
    pos = (ends - padded)[flat_e] + rank
    src_tok = jnp.zeros((n_tiles * tm,), jnp.int32).at[pos].set(jnp.arange(n_asg, dtype=jnp.int32) // TOP_K)
    tile_start = jnp.arange(n_tiles, dtype=jnp.int32) * tm
    te = jnp.sum((tile_start[:, None] >= ends[None, :]).astype(jnp.int32), axis=1)
    tile_active = (te < n_exp).astype(jnp.int32)
    tile_expert = jnp.minimum(te, n_exp - 1)
    xs = jnp.take(hn, src_tok, axis=0)
    ys = _experts(xs, wg, wu, wd, tile_expert, tile_active, tm)
    pos2 = pos.reshape(n, TOP_K)
    y0 = jnp.take(ys, pos2[:, 0], axis=0)
    y1 = jnp.take(ys, pos2[:, 1], axis=0)
    return _combine(x_p, y0, y1, rgate_p, 0), _combine(x_s, y0, y1, rgate_s, n_p)


def kernel(x_prompt, x_sample, cache_attn_k, cache_attn_v, state_ret, ln_mix, w_in, q_norm, k_norm, attn_sinks,
           ret_gn_g, ret_gn_b, w_out, ln_ffn, ffn_w_gate, ffn_w_up, ffn_w_down, moe_router, moe_w_gate, moe_w_up,
           moe_w_down):
    batch, seq, d = x_prompt.shape
    dbatch, t_s, _ = x_sample.shape
    depth = ln_mix.shape[0]
    n_kv = cache_attn_k.shape[3]
    n_heads = state_ret.shape[2]
    d_kv = n_kv * HEAD_DIM
    d_ret = n_heads * RET_HEAD_DIM
    d_attn = w_out.shape[1] - d_ret
    n_q = d_attn // HEAD_DIM
    assert cache_attn_k.shape[4] == HEAD_DIM and state_ret.shape[3] == RET_HEAD_DIM and n_kv % 2 == 0
    assert seq % CHUNK == 0 and WINDOW % CHUNK == 0 and seq >= WINDOW and cache_attn_k.shape[2] == WINDOW
    assert w_in.shape[2] == d_attn + 2 * d_kv + 4 * d_ret
    n_p, n_s = batch * seq, dbatch * t_s
    keep = min(WINDOW, seq)

    perm_q, perm_k, perm_ao, sink_heads = _layout_perms(n_q, n_kv)
    inv_k = np.argsort(perm_k)
    col_perm = np.concatenate([perm_q, d_attn + perm_k, np.arange(d_attn + d_kv, w_in.shape[2])])
    row_perm = np.concatenate([perm_ao, np.arange(d_attn, d_attn + d_ret)])

    tabs_p = _rope_tables(jnp.tile(jnp.arange(seq), batch))
    tabs_s = _rope_tables(PAST_LEN + jnp.tile(jnp.arange(t_s), dbatch))
    lane = np.arange(LANES)
    m128 = jnp.asarray(((lane[:, None] // HALF) % 2) == ((lane[None, :] // HALF) % 2), BF16)
    head_gain = lambda g: jnp.tile(g.astype(F32).reshape(2, HALF), (1, 2)).reshape(1, LANES)

    x_p, x_s = x_prompt.reshape(n_p, d), x_sample.reshape(n_s, d)
    nk_p, nv_p, nr_p, nk_s, nv_s, nr_s = [], [], [], [], [], []
    for l in range(depth):
        w_in_l = w_in[l][:, col_perm].astype(BF16)
        w_out_l = w_out[l][row_perm, :].astype(BF16)
        ln, gq, gk = ln_mix[l][None, :], head_gain(q_norm[l]), head_gain(k_norm[l])
        gng, gnb = ret_gn_g[l][None, :], ret_gn_b[l][None, :]
        proj_p = _proj(x_p, ln, w_in_l, m128, gq, gk, *tabs_p, d_attn, d_kv, d_ret, 512)
        proj_s = _proj(x_s, ln, w_in_l, m128, gq, gk, *tabs_s, d_attn, d_kv, d_ret, 256)
        merged_p, r_p = _mixer_prompt(proj_p, attn_sinks[l], sink_heads, gng, gnb, batch, seq, n_kv, n_q, n_heads)
        ck = cache_attn_k[l].reshape(dbatch, WINDOW, d_kv)[:, :, perm_k].astype(BF16)
        cv = cache_attn_v[l].reshape(dbatch, WINDOW, d_kv).astype(BF16)
        merged_s, r_s = _mixer_sample(proj_s, ck, cv, state_ret[l], attn_sinks[l], sink_heads, gng, gnb,
                                      0, dbatch, t_s, n_kv, n_q, n_heads)
        x_p = _outproj(merged_p, w_out_l, x_p)
        x_s = _outproj(merged_s, w_out_l, x_s)

        ka_last = proj_p[1].reshape(batch, seq, d_kv)[:, seq - keep:]
        nk_p.append(ka_last[:, :, inv_k].reshape(batch, keep, n_kv, HEAD_DIM))
        nv_p.append(proj_p[2].reshape(batch, seq, n_kv, HEAD_DIM)[:, seq - keep:])
        nk_s.append(proj_s[1][:, inv_k].reshape(dbatch, t_s, n_kv, HEAD_DIM))
        nv_s.append(proj_s[2].reshape(dbatch, t_s, n_kv, HEAD_DIM))
        nr_p.append(r_p)
        nr_s.append(r_s)

        i = l // 2
        lf = ln_ffn[l][None, :]
        if l % 2 == 0:
            fpad = (-ffn_w_gate.shape[2]) % COL_TILE
            wg = jnp.pad(ffn_w_gate[i].astype(BF16), ((0, 0), (0, fpad)))
            wu = jnp.pad(ffn_w_up[i].astype(BF16), ((0, 0), (0, fpad)))
            wd = jnp.pad(ffn_w_down[i].astype(BF16), ((0, fpad), (0, 0)))
            x_p, x_s = _ffn(x_p, lf, wg, wu, wd), _ffn(x_s, lf, wg, wu, wd)
        else:
            x_p, x_s = _moe(x_p, x_s, lf, moe_router[i], moe_w_gate[i], moe_w_up[i], moe_w_down[i])

    return (x_p.reshape(batch, seq, d), x_s.reshape(dbatch, t_s, d),
            jnp.stack(nk_p), jnp.stack(nv_p), jnp.stack(nr_p), jnp.stack(nk_s), jnp.stack(nv_s), jnp.stack(nr_s))
```

```python
import functools

import numpy as np
import jax
import jax.numpy as jnp
from jax import lax
from jax.experimental import pallas as pl
from jax.experimental.pallas import tpu as pltpu

CHUNK = 64
WINDOW = 128
HEAD_DIM = 64
RET_HEAD_DIM = 128
PAST_LEN = 2048
ROPE_THETA = 10000.0
TOP_K = 2
RMS_EPS = 1e-6
GN_EPS = 1e-5

LANES = 128
HALF = HEAD_DIM // 2
VMEM_LIMIT = 56 * 1024 * 1024
COL_TILE = 512
F32 = jnp.float32
BF16 = jnp.bfloat16


def _pick_tile(n, target, mult=16):
    best = None
    for t in range(mult, min(n, target) + 1, mult):
        if n % t == 0:
            best = t
    assert best is not None, (n, target)
    return best


def _params(*sem):
    return pltpu.CompilerParams(dimension_semantics=sem, vmem_limit_bytes=VMEM_LIMIT)


def _layout_perms(n_q, n_kv):
    g_sz = n_q // n_kv
    n_pairs = n_kv // 2
    perm_k = np.zeros(n_kv * HEAD_DIM, np.int32)
    perm_q = np.zeros(n_q * HEAD_DIM, np.int32)
    perm_ao = np.zeros(n_q * HEAD_DIM, np.int32)
    sink_heads = np.zeros((n_pairs, 2 * g_sz), np.int32)
    for p in range(n_pairs):
        for hf in range(2):
            for s in range(2):
                for r in range(HALF):
                    perm_k[p * LANES + hf * 64 + s * HALF + r] = (2 * p + s) * HEAD_DIM + hf * HALF + r
        for g in range(g_sz):
            bq = p * g_sz + g
            for s in range(2):
                head = g_sz * (2 * p + s) + g
                sink_heads[p, 2 * g + s] = head
                for hf in range(2):
                    for r in range(HALF):
                        perm_q[bq * LANES + hf * 64 + s * HALF + r] = head * HEAD_DIM + hf * HALF + r
                for d in range(HEAD_DIM):
                    perm_ao[bq * LANES + s * HEAD_DIM + d] = head * HEAD_DIM + d
    return perm_q, perm_k, perm_ao, sink_heads


def _rope_tables(pos):
    pf = pos.astype(F32)[:, None]
    inv_a = ROPE_THETA ** (-jnp.arange(0, HEAD_DIM, 2, dtype=F32) / HEAD_DIM)
    ang = pf * inv_a[None, :]
    c, s = jnp.cos(ang), jnp.sin(ang)
    ca = jnp.concatenate([c, c, c, c], axis=-1)
    sa = jnp.concatenate([-s, -s, s, s], axis=-1)
    inv_r = 10000.0 ** (-jnp.linspace(0.0, 1.0, RET_HEAD_DIM // 2, dtype=F32))
    ang = pf * inv_r[None, :]
    c, s = jnp.cos(ang), jnp.sin(ang)
    cr = jnp.concatenate([c, c], axis=-1)
    sr = jnp.concatenate([-s, s], axis=-1)
    return ca, sa, cr, sr


def _decay_tables(n_heads, chunk):
    log_g = jnp.log1p(-jnp.exp2(-5.0 - jnp.arange(n_heads, dtype=F32)))
    idx = jnp.arange(chunk, dtype=F32)
    dmat = jnp.exp(log_g[:, None, None] * jnp.abs(idx[:, None] - idx[None, :]))
    dec_in = jnp.exp(log_g[None, :] * (idx[:, None] + 1.0))
    dec_kv = jnp.exp(log_g[None, :] * (chunk - 1.0 - idx[:, None]))
    dec_chunk = jnp.exp(log_g * chunk)
    bc = lambda t: jnp.broadcast_to(t.T[:, :, None], (n_heads, chunk, LANES))
    return dmat, bc(dec_in), bc(dec_kv), dec_chunk


def _rope128(y, c, s):
    return y * c + pltpu.roll(y, 64, 1) * s


def _proj_kernel(nq, nr, x_ref, ln_ref, w_ref, m_ref, gq_ref, gk_ref, ca_ref, sa_ref, cr_ref, sr_ref,
                 qa_ref, ka_ref, va_ref, qr_ref, kr_ref, vr_ref, gate_ref, xn_ref):
    j = pl.program_id(1)

    @pl.when(j == 0)
    def _norm():
        x = x_ref[...]
        ms = jnp.mean(x * x, axis=-1, keepdims=True)
        xn_ref[...] = (x * lax.rsqrt(ms + RMS_EPS) * ln_ref[...]).astype(BF16)

    y = jnp.dot(xn_ref[...], w_ref[...], preferred_element_type=F32)
    n_sub = COL_TILE // LANES

    def head_norm_rope(yc, g):
        y2 = yc * yc
        hi = y2.astype(BF16)
        lo = (y2 - hi.astype(F32)).astype(BF16)
        ss = (jnp.dot(hi, m_ref[...], preferred_element_type=F32)
              + jnp.dot(lo, m_ref[...], preferred_element_type=F32))
        yn = yc * lax.rsqrt(ss * (1.0 / HEAD_DIM) + RMS_EPS) * g
        return _rope128(yn, ca_ref[...], sa_ref[...])

    @pl.when(j < nq)
    def _qa():
        for c in range(n_sub):
            sl = slice(c * LANES, (c + 1) * LANES)
            qa_ref[:, sl] = (head_norm_rope(y[:, sl], gq_ref[...]) * (HEAD_DIM ** -0.5)).astype(BF16)

    @pl.when(j == nq)
    def _kv():
        half = COL_TILE // 2
        for c in range(half // LANES):
            sl = slice(c * LANES, (c + 1) * LANES)
            ka_ref[:, sl] = head_norm_rope(y[:, sl], gk_ref[...])
        va_ref[...] = y[:, half:]

    @pl.when((j > nq) & (j <= nq + nr))
    def _qr():
        for c in range(n_sub):
            sl = slice(c * LANES, (c + 1) * LANES)
            qr_ref[:, sl] = _rope128(y[:, sl], cr_ref[...], sr_ref[...]).astype(BF16)

    @pl.when((j > nq + nr) & (j <= nq + 2 * nr))
    def _kr():
        for c in range(n_sub):
            sl = slice(c * LANES, (c + 1) * LANES)
            kr_ref[:, sl] = _rope128(y[:, sl], cr_ref[...], sr_ref[...]) * (RET_HEAD_DIM ** -0.5)

    @pl.when((j > nq + 2 * nr) & (j <= nq + 3 * nr))
    def _vr():
        vr_ref[...] = y.astype(BF16)

    @pl.when(j > nq + 3 * nr)
    def _gate():
        gate_ref[...] = y


def _proj(x, ln, w, m128, gq, gk, ca, sa, cr, sr, d_attn, d_kv, d_ret, tm_target):
    n, d = x.shape
    assert d_attn % COL_TILE == 0 and 2 * d_kv == COL_TILE and d_ret % COL_TILE == 0
    nq, nr = d_attn // COL_TILE, d_ret // COL_TILE
    n_col = nq + 1 + 4 * nr
    tm = _pick_tile(n, tm_target)
    row = lambda i, j: (i, 0)
    clip = lambda lo, cnt: (lambda i, j: (i, jnp.clip(j - lo, 0, cnt - 1)))
    tab = pl.BlockSpec((tm, LANES), row)
    vec = pl.BlockSpec((1, LANES), lambda i, j: (0, 0))
    return pl.pallas_call(
        functools.partial(_proj_kernel, nq, nr),
        grid=(n // tm, n_col),
        in_specs=[
            pl.BlockSpec((tm, d), row),
            pl.BlockSpec((1, d), lambda i, j: (0, 0)),
            pl.BlockSpec((d, COL_TILE), lambda i, j: (0, j)),
            pl.BlockSpec((LANES, LANES), lambda i, j: (0, 0)),
            vec, vec, tab, tab, tab, tab,
        ],
        out_specs=[
            pl.BlockSpec((tm, COL_TILE), clip(0, nq)),
            pl.BlockSpec((tm, d_kv), row),
            pl.BlockSpec((tm, d_kv), row),
            pl.BlockSpec((tm, COL_TILE), clip(nq + 1, nr)),
            pl.BlockSpec((tm, COL_TILE), clip(nq + 1 + nr, nr)),
            pl.BlockSpec((tm, COL_TILE), clip(nq + 1 + 2 * nr, nr)),
            pl.BlockSpec((tm, COL_TILE), clip(nq + 1 + 3 * nr, nr)),
        ],
        out_shape=[
            jax.ShapeDtypeStruct((n, d_attn), BF16),
            jax.ShapeDtypeStruct((n, d_kv), F32),
            jax.ShapeDtypeStruct((n, d_kv), F32),
            jax.ShapeDtypeStruct((n, d_ret), BF16),
            jax.ShapeDtypeStruct((n, d_ret), F32),
            jax.ShapeDtypeStruct((n, d_ret), BF16),
            jax.ShapeDtypeStruct((n, d_ret), F32),
        ],
        scratch_shapes=[pltpu.VMEM((tm, d), BF16)],
        compiler_params=_params("parallel", "arbitrary"),
        name="proj",
    )(x, ln, w, m128, gq, gk, ca, sa, cr, sr)


def _attend(items, valid, sink_ref, t, n_pairs, g_sz):
    lane = lax.broadcasted_iota(jnp.int32, (1, LANES), 1)
    slot_a = ((lane // HALF) % 2) == 0
    first_half = lane < HEAD_DIM
    zero = jnp.zeros((), BF16)
    jobs = [(i, p) for i in range(len(items)) for p in range(n_pairs)]
    scores = []
    for i, p in jobs:
        qa, k_all, _ = items[i]
        parts = []
        for g in range(g_sz):
            qb = qa[:, (p * g_sz + g) * LANES:(p * g_sz + g + 1) * LANES]
            parts.append(jnp.where(slot_a, qb, zero))
            parts.append(jnp.where(slot_a, zero, qb))
        qs = jnp.concatenate(parts, axis=0)
        scores.append(lax.dot_general(qs, k_all[:, p * LANES:(p + 1) * LANES], (((1,), (1,)), ((), ())),
                                      preferred_element_type=F32))
    probs = []
    for (i, p), s in zip(jobs, scores):
        if valid is not None:
            s = jnp.where(valid, s, -jnp.inf)
        sk = sink_ref[p][:, 0:1]
        m = jnp.maximum(jnp.max(s, axis=-1, keepdims=True), sk)
        e = jnp.exp(s - m)
        den = jnp.sum(e, axis=-1, keepdims=True) + jnp.exp(sk - m)
        probs.append((e / den).astype(BF16))
    outs = [[] for _ in items]
    for (i, p), pn in zip(jobs, probs):
        o = jnp.dot(pn, items[i][2][:, p * LANES:(p + 1) * LANES], preferred_element_type=F32)
        for g in range(g_sz):
            o_a = o[(2 * g) * t:(2 * g + 1) * t]
            o_b = o[(2 * g + 1) * t:(2 * g + 2) * t]
            outs[i].append(jnp.where(first_half, o_a, o_b))
    return outs


def _retain(items, dmat_ref, decin_ref, deckv_ref, dchunk_ref, gng, gnb, n_heads):
    nt = (((1,), (1,)), ((), ()))
    tn = (((0,), (0,)), ((), ()))
    jobs = [(i, h) for i in range(len(items)) for h in range(n_heads)]
    sl = lambda h: slice(h * RET_HEAD_DIM, (h + 1) * RET_HEAD_DIM)
    intra, inter, r_prevs = [], [], []
    for i, h in jobs:
        qr, kr, _, _, r_in_ref, _ = items[i]
        q = qr[:, sl(h)]
        r_prev = r_in_ref[h]
        r_prevs.append(r_prev)
        intra.append(lax.dot_general(q, kr[:, sl(h)].astype(BF16), nt, preferred_element_type=F32) * dmat_ref[h])
        inter.append(jnp.dot(q, r_prev.astype(BF16), preferred_element_type=F32) * decin_ref[h])
    for (i, h), r_prev in zip(jobs, r_prevs):
        _, kr, vr, _, _, r_out_ref = items[i]
        kd = (kr[:, sl(h)] * deckv_ref[h]).astype(BF16)
        kv = lax.dot_general(kd, vr[:, sl(h)], tn, preferred_element_type=F32)
        r_out_ref[h] = r_prev * dchunk_ref[h] + kv
    outs = [[] for _ in items]
    for (i, h), s, o_inter in zip(jobs, intra, inter):
        _, _, vr, gate, _, _ = items[i]
        o = jnp.dot(s.astype(BF16), vr[:, sl(h)], preferred_element_type=F32) + o_inter
        mu = jnp.mean(o, axis=-1, keepdims=True)
        dev = o - mu
        var = jnp.mean(dev * dev, axis=-1, keepdims=True)
        r = dev * lax.rsqrt(var + GN_EPS) * gng[:, sl(h)] + gnb[:, sl(h)]
        gt = gate[:, sl(h)]
        outs[i].append(gt * jax.nn.sigmoid(gt) * r)
    return outs


def _store_merged(merged_ref, ao, ro, d_attn):
    for b, blk in enumerate(ao):
        merged_ref[:, b * LANES:(b + 1) * LANES] = blk.astype(BF16)
    for h, blk in enumerate(ro):
        merged_ref[:, d_attn + h * RET_HEAD_DIM:d_attn + (h + 1) * RET_HEAD_DIM] = blk.astype(BF16)


def _mixer_prompt_kernel(nc, t, batch, n_pairs, g_sz, n_heads, d_attn,
                         dchunk_ref, qa_ref, *rest):
    n_win = WINDOW // t + 1
    k_refs, v_refs = rest[:n_win], rest[n_win:2 * n_win]
    (qr_ref, kr_ref, vr_ref, gate_ref, sink_ref, dmat_ref, decin_ref, deckv_ref, gng_ref, gnb_ref,
     merged_ref, rout_ref, r_ref) = rest[2 * n_win:]
    c = pl.program_id(0)

    @pl.when(c == 0)
    def _reset():
        r_ref[...] = jnp.zeros_like(r_ref)

    kidx = lax.broadcasted_iota(jnp.int32, (1, n_win * t), 1)
    valid = kidx >= (n_win - 1 - c) * t
    att_items = [(qa_ref[b],
                  jnp.concatenate([r[b].astype(BF16) for r in k_refs], axis=0),
                  jnp.concatenate([r[b].astype(BF16) for r in v_refs], axis=0)) for b in range(batch)]
    ret_items = [(qr_ref[b], kr_ref[b], vr_ref[b], gate_ref[b], r_ref.at[b], r_ref.at[b]) for b in range(batch)]
    ao = _attend(att_items, valid, sink_ref, t, n_pairs, g_sz)
    ro = _retain(ret_items, dmat_ref, decin_ref, deckv_ref, dchunk_ref, gng_ref[...], gnb_ref[...], n_heads)
    for b in range(batch):
        _store_merged(merged_ref.at[b], ao[b], ro[b], d_attn)

    @pl.when(c == nc - 1)
    def _emit_state():
        rout_ref[...] = r_ref[...]


def _mixer_sample_kernel(t, n_pairs, g_sz, n_heads, d_attn,
                         dchunk_ref, qa_ref, ka_ref, va_ref, qr_ref, kr_ref, vr_ref, gate_ref,
                         ck_ref, cv_ref, r0_ref,
                         sink_ref, dmat_ref, decin_ref, deckv_ref, gng_ref, gnb_ref,
                         merged_ref, rout_ref):
    k_all = jnp.concatenate([ck_ref[...], ka_ref[...].astype(BF16)], axis=0)
    v_all = jnp.concatenate([cv_ref[...], va_ref[...].astype(BF16)], axis=0)
    ao = _attend([(qa_ref[...], k_all, v_all)], None, sink_ref, t, n_pairs, g_sz)
    ro = _retain([(qr_ref[...], kr_ref[...], vr_ref[...], gate_ref[...], r0_ref, rout_ref)],
                 dmat_ref, decin_ref, deckv_ref, dchunk_ref, gng_ref[...], gnb_ref[...], n_heads)
    _store_merged(merged_ref, ao[0], ro[0], d_attn)


def _sink_table(sinks, sink_heads, t):
    per_row = jnp.repeat(sinks.astype(F32)[sink_heads], t, axis=1)
    return jnp.broadcast_to(per_row[:, :, None], per_row.shape + (LANES,))


def _mixer_prompt(proj_out, sinks, sink_heads, gng, gnb, batch, seq, n_kv, n_q, n_heads):
    qa, ka, va, qr, kr, vr, gate = proj_out
    t = CHUNK
    nc = seq // t
    d_attn, d_kv, d_ret = n_q * HEAD_DIM, n_kv * HEAD_DIM, n_heads * RET_HEAD_DIM
    n_pairs, g_sz = n_kv // 2, n_q // n_kv
    dmat, decin, deckv, dchunk = _decay_tables(n_heads, t)
    sink_tab = _sink_table(sinks, sink_heads, t)
    seq3 = lambda a: a.reshape(batch, seq, a.shape[-1])
    qa, ka, va, qr, kr, vr, gate = map(seq3, proj_out)
    row = lambda c, *_: (0, c, 0)
    back = lambda k: (lambda c, *_: (0, jnp.maximum(c - k, 0), 0))
    const3 = lambda c, *_: (0, 0, 0)
    const2 = lambda c, *_: (0, 0)
    n_win = WINDOW // t + 1
    kv_specs = [pl.BlockSpec((batch, t, d_kv), back(n_win - 1 - s)) for s in range(n_win)]
    state_shape = (batch, n_heads, RET_HEAD_DIM, RET_HEAD_DIM)
    grid_spec = pltpu.PrefetchScalarGridSpec(
        num_scalar_prefetch=1,
        grid=(nc,),
        in_specs=[
            pl.BlockSpec((batch, t, d_attn), row), *kv_specs, *kv_specs,
            pl.BlockSpec((batch, t, d_ret), row), pl.BlockSpec((batch, t, d_ret), row),
            pl.BlockSpec((batch, t, d_ret), row), pl.BlockSpec((batch, t, d_ret), row),
            pl.BlockSpec(sink_tab.shape, const3), pl.BlockSpec(dmat.shape, const3),
            pl.BlockSpec(decin.shape, const3), pl.BlockSpec(deckv.shape, const3),
            pl.BlockSpec((1, d_ret), const2), pl.BlockSpec((1, d_ret), const2),
        ],
        out_specs=[
            pl.BlockSpec((batch, t, d_attn + d_ret), row),
            pl.BlockSpec(state_shape, lambda c, *_: (0, 0, 0, 0)),
        ],
        scratch_shapes=[pltpu.VMEM(state_shape, F32)],
    )
    merged, r_fin = pl.pallas_call(
        functools.partial(_mixer_prompt_kernel, nc, t, batch, n_pairs, g_sz, n_heads, d_attn),
        grid_spec=grid_spec,
        out_shape=[jax.ShapeDtypeStruct((batch, seq, d_attn + d_ret), BF16),
                   jax.ShapeDtypeStruct(state_shape, F32)],
        compiler_params=_params("arbitrary"),
        name="mixer_prompt",
    )(dchunk, qa, *([ka] * n_win), *([va] * n_win), qr, kr, vr, gate, sink_tab, dmat, decin, deckv, gng, gnb)
    return merged.reshape(batch * seq, d_attn + d_ret), r_fin


def _mixer_sample(proj_out, ck, cv, r0, sinks, sink_heads, gng, gnb, row0, dbatch, t, n_kv, n_q, n_heads):
    qa, ka, va, qr, kr, vr, gate = proj_out
    d_attn, d_kv, d_ret = n_q * HEAD_DIM, n_kv * HEAD_DIM, n_heads * RET_HEAD_DIM
    n_pairs, g_sz = n_kv // 2, n_q // n_kv
    dmat, decin, deckv, dchunk = _decay_tables(n_heads, t)
    sink_tab = _sink_table(sinks, sink_heads, t)
    blk0 = row0 // t
    row = lambda b, *_: (blk0 + b, 0)
    const3 = lambda b, *_: (0, 0, 0)
    const2 = lambda b, *_: (0, 0)
    state_spec = pl.BlockSpec((None, n_heads, RET_HEAD_DIM, RET_HEAD_DIM), lambda b, *_: (b, 0, 0, 0))
    n_hist = ck.shape[1]
    grid_spec = pltpu.PrefetchScalarGridSpec(
        num_scalar_prefetch=1,
        grid=(dbatch,),
        in_specs=[
            pl.BlockSpec((t, d_attn), row), pl.BlockSpec((t, d_kv), row), pl.BlockSpec((t, d_kv), row),
            pl.BlockSpec((t, d_ret), row), pl.BlockSpec((t, d_ret), row), pl.BlockSpec((t, d_ret), row),
            pl.BlockSpec((t, d_ret), row),
            pl.BlockSpec((None, n_hist, d_kv), lambda b, *_: (b, 0, 0)),
            pl.BlockSpec((None, n_hist, d_kv), lambda b, *_: (b, 0, 0)),
            state_spec,
            pl.BlockSpec(sink_tab.shape, const3), pl.BlockSpec(dmat.shape, const3),
            pl.BlockSpec(decin.shape, const3), pl.BlockSpec(deckv.shape, const3),
            pl.BlockSpec((1, d_ret), const2), pl.BlockSpec((1, d_ret), const2),
        ],
        out_specs=[pl.BlockSpec((t, d_attn + d_ret), lambda b, *_: (b, 0)), state_spec],
    )
    return pl.pallas_call(
        functools.partial(_mixer_sample_kernel, t, n_pairs, g_sz, n_heads, d_attn),
        grid_spec=grid_spec,
        out_shape=[jax.ShapeDtypeStruct((dbatch * t, d_attn + d_ret), BF16),
                   jax.ShapeDtypeStruct(r0.shape, F32)],
        compiler_params=_params("parallel"),
        name="mixer_sample",
    )(dchunk, qa, ka, va, qr, kr, vr, gate, ck, cv, r0, sink_tab, dmat, decin, deckv, gng, gnb)


def _outproj_kernel(m_ref, w_ref, x_ref, o_ref):
    o_ref[...] = x_ref[...] + jnp.dot(m_ref[...], w_ref[...], preferred_element_type=F32)


def _outproj(merged, w, x):
    n, d = x.shape
    k = merged.shape[1]
    tm = _pick_tile(n, 1024)
    tn = _pick_tile(d, 1024, LANES)
    return pl.pallas_call(
        _outproj_kernel,
        grid=(n // tm, d // tn),
        in_specs=[pl.BlockSpec((tm, k), lambda i, j: (i, 0)),
                  pl.BlockSpec((k, tn), lambda i, j: (0, j)),
                  pl.BlockSpec((tm, tn), lambda i, j: (i, j))],
        out_specs=pl.BlockSpec((tm, tn), lambda i, j: (i, j)),
        out_shape=jax.ShapeDtypeStruct((n, d), F32),
        compiler_params=_params("parallel", "arbitrary"),
        name="outproj",
    )(merged, w, x)


def _swiglu_step(xn, wg, wu, wd):
    g = jnp.dot(xn, wg, preferred_element_type=F32)
    u = jnp.dot(xn, wu, preferred_element_type=F32)
    h = (g * jax.nn.sigmoid(g) * u).astype(BF16)
    return jnp.dot(h, wd, preferred_element_type=F32)


def _ffn_kernel(x_ref, ln_ref, wg_ref, wu_ref, wd_ref, o_ref, xn_ref):
    f = pl.program_id(1)

    @pl.when(f == 0)
    def _norm():
        x = x_ref[...]
        ms = jnp.mean(x * x, axis=-1, keepdims=True)
        xn_ref[...] = (x * lax.rsqrt(ms + RMS_EPS) * ln_ref[...]).astype(BF16)
        o_ref[...] = x

    o_ref[...] += _swiglu_step(xn_ref[...], wg_ref[...], wu_ref[...], wd_ref[...])


def _ffn(x, ln, wg, wu, wd):
    n, d = x.shape
    fp = wg.shape[1]
    tf = _pick_tile(fp, 512, LANES)
    tm = _pick_tile(n, 1024)
    return pl.pallas_call(
        _ffn_kernel,
        grid=(n // tm, fp // tf),
        in_specs=[pl.BlockSpec((tm, d), lambda i, f: (i, 0)),
                  pl.BlockSpec((1, d), lambda i, f: (0, 0)),
                  pl.BlockSpec((d, tf), lambda i, f: (0, f)),
                  pl.BlockSpec((d, tf), lambda i, f: (0, f)),
                  pl.BlockSpec((tf, d), lambda i, f: (f, 0))],
        out_specs=pl.BlockSpec((tm, d), lambda i, f: (i, 0)),
        out_shape=jax.ShapeDtypeStruct((n, d), F32),
        scratch_shapes=[pltpu.VMEM((tm, d), BF16)],
        compiler_params=_params("parallel", "arbitrary"),
        name="ffn",
    )(x, ln, wg, wu, wd)


def _router_kernel(n_exp, x_ref, ln_ref, wr_hi_ref, wr_lo_ref, hn_ref, idx_ref, gate_ref):
    x = x_ref[...]
    ms = jnp.mean(x * x, axis=-1, keepdims=True)
    hf = x * lax.rsqrt(ms + RMS_EPS) * ln_ref[...]
    hn = hf.astype(BF16)
    bits = lax.bitcast_convert_type(hn.astype(F32), jnp.uint32)
    half = bits.shape[1] // 2
    hn_ref[...] = (bits[:, :half] >> 16) | (bits[:, half:] & jnp.uint32(0xFFFF0000))
    h_lo = (hf - hn.astype(F32)).astype(BF16)
    logits = (jnp.dot(hn, wr_hi_ref[...], preferred_element_type=F32)
              + jnp.dot(h_lo, wr_hi_ref[...], preferred_element_type=F32)
              + jnp.dot(hn, wr_lo_ref[...], preferred_element_type=F32))
    lane = lax.broadcasted_iota(jnp.int32, logits.shape, 1)
    logits = jnp.where(lane < n_exp, logits, -jnp.inf)
    m1 = jnp.max(logits, axis=-1, keepdims=True)
    i1 = jnp.min(jnp.where(logits == m1, lane, LANES), axis=-1, keepdims=True)
    rest = jnp.where(lane == i1, -jnp.inf, logits)
    m2 = jnp.max(rest, axis=-1, keepdims=True)
    i2 = jnp.min(jnp.where(rest == m2, lane, LANES), axis=-1, keepdims=True)
    e2 = jnp.exp(m2 - m1)
    den = 1.0 + e2
    idx_ref[...] = jnp.where(lane == 0, i1, jnp.where(lane == 1, i2, 0))
    gate_ref[...] = jnp.where(lane == 0, 1.0 / den, jnp.where(lane == 1, e2 / den, 0.0))


def _router(x, ln, wr_hi, wr_lo, n_exp):
    n, d = x.shape
    tm = _pick_tile(n, 512)
    row = lambda i: (i, 0)
    return pl.pallas_call(
        functools.partial(_router_kernel, n_exp),
        grid=(n // tm,),
        in_specs=[pl.BlockSpec((tm, d), row), pl.BlockSpec((1, d), lambda i: (0, 0)),
                  pl.BlockSpec((d, LANES), lambda i: (0, 0)), pl.BlockSpec((d, LANES), lambda i: (0, 0))],
        out_specs=[pl.BlockSpec((tm, d // 2), row), pl.BlockSpec((tm, LANES), row), pl.BlockSpec((tm, LANES), row)],
        out_shape=[jax.ShapeDtypeStruct((n, d // 2), jnp.uint32), jax.ShapeDtypeStruct((n, LANES), jnp.int32),
                   jax.ShapeDtypeStruct((n, LANES), F32)],
        compiler_params=_params("parallel"),
        name="router",
    )(x, ln, wr_hi, wr_lo)


def _experts_kernel(tm, te_ref, act_ref, tok_ref, hn_hbm, wg_ref, wu_ref, wd_ref, o_ref, rows_ref, x_ref, sem):
    i = pl.program_id(0)
    f = pl.program_id(1)
    active = act_ref[i] == 1

    @pl.when(f == 0)
    def _init():
        o_ref[...] = jnp.zeros_like(o_ref)

    @pl.when(active & (f == 0))
    def _gather():
        def start_row(r, carry):
            pltpu.make_async_copy(hn_hbm.at[pl.ds(tok_ref[0, r], 1)], rows_ref.at[pl.ds(r, 1)], sem).start()
            return carry

        lax.fori_loop(0, tm, start_row, 0, unroll=8)
        pltpu.make_async_copy(hn_hbm.at[pl.ds(0, tm)], rows_ref, sem).wait()
        words = rows_ref[...]
        half = words.shape[1]
        x_ref[:, :half] = lax.bitcast_convert_type(words << 16, F32).astype(BF16)
        x_ref[:, half:] = lax.bitcast_convert_type(words & jnp.uint32(0xFFFF0000), F32).astype(BF16)

    @pl.when(active)
    def _step():
        o_ref[...] += _swiglu_step(x_ref[...], wg_ref[...].astype(BF16), wu_ref[...].astype(BF16),
                                   wd_ref[...].astype(BF16))


def _experts(hn32, src_tok, wg, wu, wd, tile_expert, tile_active, tm):
    n, half = hn32.shape
    d = 2 * half
    assert n >= tm
    n_tiles = src_tok.shape[0] // tm
    n_exp, _, fe = wg.shape
    tf = _pick_tile(fe, 512, LANES)
    nf = fe // tf
    fidx = lambda i, f, te, act: f * act[i] + (nf - 1) * (1 - act[i])
    grid_spec = pltpu.PrefetchScalarGridSpec(
        num_scalar_prefetch=2,
        grid=(n_tiles, nf),
        in_specs=[pl.BlockSpec((None, 1, tm), lambda i, f, te, act: (i, 0, 0), memory_space=pltpu.SMEM),
                  pl.BlockSpec(memory_space=pl.ANY),
                  pl.BlockSpec((None, d, tf), lambda i, f, te, act: (te[i], 0, fidx(i, f, te, act))),
                  pl.BlockSpec((None, d, tf), lambda i, f, te, act: (te[i], 0, fidx(i, f, te, act))),
                  pl.BlockSpec((None, tf, d), lambda i, f, te, act: (te[i], fidx(i, f, te, act), 0))],
        out_specs=pl.BlockSpec((tm, d), lambda i, f, te, act: (i, 0), pipeline_mode=pl.Buffered(1)),
        scratch_shapes=[pltpu.VMEM((tm, half), jnp.uint32), pltpu.VMEM((tm, d), BF16), pltpu.SemaphoreType.DMA],
    )
    return pl.pallas_call(
        functools.partial(_experts_kernel, tm),
        grid_spec=grid_spec,
        out_shape=jax.ShapeDtypeStruct((n_tiles * tm, d), F32),
        compiler_params=_params("arbitrary", "arbitrary"),
        name="experts",
    )(tile_expert, tile_active, src_tok.reshape(n_tiles, 1, tm), hn32, wg, wu, wd)


def _combine_kernel(x_ref, y0_ref, y1_ref, g_ref, o_ref):
    g = g_ref[...]
    o_ref[...] = x_ref[...] + (g[:, 0:1] * y0_ref[...] + g[:, 1:2] * y1_ref[...])


def _combine(x, y0, y1, gate, row0):
    n, d = x.shape
    tm = _pick_tile(n, 256)
    assert row0 % tm == 0
    blk0 = row0 // tm
    own = lambda i: (i, 0)
    shifted = lambda i: (blk0 + i, 0)
    return pl.pallas_call(
        _combine_kernel,
        grid=(n // tm,),
        in_specs=[pl.BlockSpec((tm, d), own), pl.BlockSpec((tm, d), shifted), pl.BlockSpec((tm, d), shifted),
                  pl.BlockSpec((tm, LANES), own)],
        out_specs=pl.BlockSpec((tm, d), own),
        out_shape=jax.ShapeDtypeStruct((n, d), F32),
        compiler_params=_params("parallel"),
        name="combine",
    )(x, y0, y1, gate)


def _moe(x_p, x_s, ln, w_router, wg, wu, wd):
    n_p, d = x_p.shape
    n = n_p + x_s.shape[0]
    n_exp = w_router.shape[1]
    wr = jnp.pad(w_router.astype(F32), ((0, 0), (0, LANES - n_exp)))
    wr_hi = wr.astype(BF16)
    wr_lo = (wr - wr_hi.astype(F32)).astype(BF16)
    hn_p, ridx_p, rgate_p = _router(x_p, ln, wr_hi, wr_lo, n_exp)
    hn_s, ridx_s, rgate_s = _router(x_s, ln, wr_hi, wr_lo, n_exp)
    hn = jnp.concatenate([hn_p, hn_s], axis=0)
    tm = 1024 if TOP_K * n >= 8192 else 64
    n_asg = TOP_K * n
    n_tiles = -(-(n_asg + n_exp * (tm - 1)) // tm)
    flat_e = jnp.concatenate([ridx_p[:, :TOP_K], ridx_s[:, :TOP_K]], axis=0).reshape(-1)
    onehot = (flat_e[:, None] == jnp.arange(n_exp, dtype=jnp.int32)[None, :]).astype(jnp.int32)
    csum = jnp.cumsum(onehot, axis=0)
    counts = csum[-1]
    rank = jnp.take_along_axis(csum, flat_e[:, None], axis=1)[:, 0] - 1
    padded = ((counts + tm - 1) // tm) * tm
    ends = jnp.cumsum(padded)
    pos = (ends - padded)[flat_e] + rank
    src_tok = jnp.zeros((n_tiles * tm,), jnp.int32).at[pos].set(jnp.arange(n_asg, dtype=jnp.int32) // TOP_K)
    tile_start = jnp.arange(n_tiles, dtype=jnp.int32) * tm
    te = jnp.sum((tile_start[:, None] >= ends[None, :]).astype(jnp.int32), axis=1)
    tile_active = (te < n_exp).astype(jnp.int32)
    tile_expert = jnp.minimum(te, n_exp - 1)
    ys = _experts(hn, src_tok, wg, wu, wd, tile_expert, tile_active, tm)
    pos2 = pos.reshape(n, TOP_K)
    y0 = jnp.take(ys, pos2[:, 0], axis=0)
    y1 = jnp.take(ys, pos2[:, 1], axis=0)
    return _combine(x_p, y0, y1, rgate_p, 0), _combine(x_s, y0, y1, rgate_s, n_p)


def kernel(x_prompt, x_sample, cache_attn_k, cache_attn_v, state_ret, ln_mix, w_in, q_norm, k_norm, attn_sinks,
           ret_gn_g, ret_gn_b, w_out, ln_ffn, ffn_w_gate, ffn_w_up, ffn_w_down, moe_router, moe_w_gate, moe_w_up,
           moe_w_down):
    batch, seq, d = x_prompt.shape
    dbatch, t_s, _ = x_sample.shape
    depth = ln_mix.shape[0]
    n_kv = cache_attn_k.shape[3]
    n_heads = state_ret.shape[2]
    d_kv = n_kv * HEAD_DIM
    d_ret = n_heads * RET_HEAD_DIM
    d_attn = w_out.shape[1] - d_ret
    n_q = d_attn // HEAD_DIM
    assert cache_attn_k.shape[4] == HEAD_DIM and state_ret.shape[3] == RET_HEAD_DIM and n_kv % 2 == 0
    assert seq % CHUNK == 0 and WINDOW % CHUNK == 0 and seq >= WINDOW and cache_attn_k.shape[2] == WINDOW
    assert w_in.shape[2] == d_attn + 2 * d_kv + 4 * d_ret
    n_p, n_s = batch * seq, dbatch * t_s
    keep = min(WINDOW, seq)

    perm_q, perm_k, perm_ao, sink_heads = _layout_perms(n_q, n_kv)
    inv_k = np.argsort(perm_k)
    col_perm = np.concatenate([perm_q, d_attn + perm_k, np.arange(d_attn + d_kv, w_in.shape[2])])
    row_perm = np.concatenate([perm_ao, np.arange(d_attn, d_attn + d_ret)])

    tabs_p = _rope_tables(jnp.tile(jnp.arange(seq), batch))
    tabs_s = _rope_tables(PAST_LEN + jnp.tile(jnp.arange(t_s), dbatch))
    lane = np.arange(LANES)
    m128 = jnp.asarray(((lane[:, None] // HALF) % 2) == ((lane[None, :] // HALF) % 2), BF16)
    head_gain = lambda g: jnp.tile(g.astype(F32).reshape(2, HALF), (1, 2)).reshape(1, LANES)

    x_p, x_s = x_prompt.reshape(n_p, d), x_sample.reshape(n_s, d)
    nk_p, nv_p, nr_p, nk_s, nv_s, nr_s = [], [], [], [], [], []
    for l in range(depth):
        w_in_l = w_in[l][:, col_perm].astype(BF16)
        w_out_l = w_out[l][row_perm, :].astype(BF16)
        ln, gq, gk = ln_mix[l][None, :], head_gain(q_norm[l]), head_gain(k_norm[l])
        gng, gnb = ret_gn_g[l][None, :], ret_gn_b[l][None, :]
        proj_p = _proj(x_p, ln, w_in_l, m128, gq, gk, *tabs_p, d_attn, d_kv, d_ret, 512)
        proj_s = _proj(x_s, ln, w_in_l, m128, gq, gk, *tabs_s, d_attn, d_kv, d_ret, 256)
        merged_p, r_p = _mixer_prompt(proj_p, attn_sinks[l], sink_heads, gng, gnb, batch, seq, n_kv, n_q, n_heads)
        ck = cache_attn_k[l].reshape(dbatch, WINDOW, d_kv)[:, :, perm_k].astype(BF16)
        cv = cache_attn_v[l].reshape(dbatch, WINDOW, d_kv).astype(BF16)
        merged_s, r_s = _mixer_sample(proj_s, ck, cv, state_ret[l], attn_sinks[l], sink_heads, gng, gnb,
                                      0, dbatch, t_s, n_kv, n_q, n_heads)
        x_p = _outproj(merged_p, w_out_l, x_p)
        x_s = _outproj(merged_s, w_out_l, x_s)

        ka_last = proj_p[1].reshape(batch, seq, d_kv)[:, seq - keep:]
        nk_p.append(ka_last[:, :, inv_k].reshape(batch, keep, n_kv, HEAD_DIM))
        nv_p.append(proj_p[2].reshape(batch, seq, n_kv, HEAD_DIM)[:, seq - keep:])
        nk_s.append(proj_s[1][:, inv_k].reshape(dbatch, t_s, n_kv, HEAD_DIM))
        nv_s.append(proj_s[2].reshape(dbatch, t_s, n_kv, HEAD_DIM))
        nr_p.append(r_p)
        nr_s.append(r_s)

        i = l // 2
        lf = ln_ffn[l][None, :]
        if l % 2 == 0:
            fpad = (-ffn_w_gate.shape[2]) % COL_TILE
            wg = jnp.pad(ffn_w_gate[i].astype(BF16), ((0, 0), (0, fpad)))
            wu = jnp.pad(ffn_w_up[i].astype(BF16), ((0, 0), (0, fpad)))
            wd = jnp.pad(ffn_w_down[i].astype(BF16), ((0, fpad), (0, 0)))
            x_p, x_s = _ffn(x_p, lf, wg, wu, wd), _ffn(x_s, lf, wg, wu, wd)
        else:
            x_p, x_s = _moe(x_p, x_s, lf, moe_router[i], moe_w_gate[i], moe_w_up[i], moe_w_down[i])

    return (x_p.reshape(batch, seq, d), x_s.reshape(dbatch, t_s, d),
            jnp.stack(nk_p), jnp.stack(nv_p), jnp.stack(nr_p), jnp.stack(nk_s), jnp.stack(nv_s), jnp.stack(nr_s))
```

```python
import functools

import numpy as np
import jax
import jax.numpy as jnp
from jax import lax
from jax.experimental import pallas as pl
from jax.experimental.pallas import tpu as pltpu

CHUNK = 64
WINDOW = 128
HEAD_DIM = 64
RET_HEAD_DIM = 128
PAST_LEN = 2048
ROPE_THETA = 10000.0
TOP_K = 2
RMS_EPS = 1e-6
GN_EPS = 1e-5

LANES = 128
HALF = HEAD_DIM // 2
VMEM_LIMIT = 56 * 1024 * 1024
COL_TILE = 512
EXPERT_WEIGHT_STREAMS = 4
F32 = jnp.float32
BF16 = jnp.bfloat16


def _pick_tile(n, target, mult=16):
    best = None
    for t in range(mult, min(n, target) + 1, mult):
        if n % t == 0:
            best = t
    assert best is not None, (n, target)
    return best


def _params(*sem):
    return pltpu.CompilerParams(dimension_semantics=sem, vmem_limit_bytes=VMEM_LIMIT)


def _layout_perms(n_q, n_kv):
    g_sz = n_q // n_kv
    n_pairs = n_kv // 2
    perm_k = np.zeros(n_kv * HEAD_DIM, np.int32)
    perm_q = np.zeros(n_q * HEAD_DIM, np.int32)
    perm_ao = np.zeros(n_q * HEAD_DIM, np.int32)
    sink_heads = np.zeros((n_pairs, 2 * g_sz), np.int32)
    for p in range(n_pairs):
        for hf in range(2):
            for s in range(2):
                for r in range(HALF):
                    perm_k[p * LANES + hf * 64 + s * HALF + r] = (2 * p + s) * HEAD_DIM + hf * HALF + r
        for g in range(g_sz):
            bq = p * g_sz + g
            for s in range(2):
                head = g_sz * (2 * p + s) + g
                sink_heads[p, 2 * g + s] = head
                for hf in range(2):
                    for r in range(HALF):
                        perm_q[bq * LANES + hf * 64 + s * HALF + r] = head * HEAD_DIM + hf * HALF + r
                for d in range(HEAD_DIM):
                    perm_ao[bq * LANES + s * HEAD_DIM + d] = head * HEAD_DIM + d
    return perm_q, perm_k, perm_ao, sink_heads


def _rope_tables(pos):
    pf = pos.astype(F32)[:, None]
    inv_a = ROPE_THETA ** (-jnp.arange(0, HEAD_DIM, 2, dtype=F32) / HEAD_DIM)
    ang = pf * inv_a[None, :]
    c, s = jnp.cos(ang), jnp.sin(ang)
    ca = jnp.concatenate([c, c, c, c], axis=-1)
    sa = jnp.concatenate([-s, -s, s, s], axis=-1)
    inv_r = 10000.0 ** (-jnp.linspace(0.0, 1.0, RET_HEAD_DIM // 2, dtype=F32))
    ang = pf * inv_r[None, :]
    c, s = jnp.cos(ang), jnp.sin(ang)
    cr = jnp.concatenate([c, c], axis=-1)
    sr = jnp.concatenate([-s, s], axis=-1)
    return ca, sa, cr, sr


def _decay_tables(n_heads, chunk):
    log_g = jnp.log1p(-jnp.exp2(-5.0 - jnp.arange(n_heads, dtype=F32)))
    idx = jnp.arange(chunk, dtype=F32)
    dmat = jnp.exp(log_g[:, None, None] * jnp.abs(idx[:, None] - idx[None, :]))
    dec_in = jnp.exp(log_g[None, :] * (idx[:, None] + 1.0))
    dec_kv = jnp.exp(log_g[None, :] * (chunk - 1.0 - idx[:, None]))
    dec_chunk = jnp.exp(log_g * chunk)
    bc = lambda t: jnp.broadcast_to(t.T[:, :, None], (n_heads, chunk, LANES))
    return dmat, bc(dec_in), bc(dec_kv), dec_chunk


def _rope128(y, c, s):
    return y * c + pltpu.roll(y, 64, 1) * s


def _proj_kernel(nq, nr, x_ref, ln_ref, w_ref, m_ref, gq_ref, gk_ref, ca_ref, sa_ref, cr_ref, sr_ref,
                 qa_ref, ka_ref, va_ref, qr_ref, kr_ref, vr_ref, gate_ref, xn_ref):
    j = pl.program_id(1)

    @pl.when(j == 0)
    def _norm():
        x = x_ref[...]
        ms = jnp.mean(x * x, axis=-1, keepdims=True)
        xn_ref[...] = (x * lax.rsqrt(ms + RMS_EPS) * ln_ref[...]).astype(BF16)

    y = jnp.dot(xn_ref[...], w_ref[j], preferred_element_type=F32)
    n_sub = COL_TILE // LANES

    def head_norm_rope(yc, g):
        y2 = yc * yc
        hi = y2.astype(BF16)
        lo = (y2 - hi.astype(F32)).astype(BF16)
        ss = (jnp.dot(hi, m_ref[...], preferred_element_type=F32)
              + jnp.dot(lo, m_ref[...], preferred_element_type=F32))
        yn = yc * lax.rsqrt(ss * (1.0 / HEAD_DIM) + RMS_EPS) * g
        return _rope128(yn, ca_ref[...], sa_ref[...])

    @pl.when(j < nq)
    def _qa():
        for c in range(n_sub):
            sl = slice(c * LANES, (c + 1) * LANES)
            qa_ref[:, sl] = (head_norm_rope(y[:, sl], gq_ref[...]) * (HEAD_DIM ** -0.5)).astype(BF16)

    @pl.when(j == nq)
    def _kv():
        half = COL_TILE // 2
        for c in range(half // LANES):
            sl = slice(c * LANES, (c + 1) * LANES)
            ka_ref[:, sl] = head_norm_rope(y[:, sl], gk_ref[...])
        va_ref[...] = y[:, half:]

    @pl.when((j > nq) & (j <= nq + nr))
    def _qr():
        for c in range(n_sub):
            sl = slice(c * LANES, (c + 1) * LANES)
            qr_ref[:, sl] = _rope128(y[:, sl], cr_ref[...], sr_ref[...]).astype(BF16)

    @pl.when((j > nq + nr) & (j <= nq + 2 * nr))
    def _kr():
        for c in range(n_sub):
            sl = slice(c * LANES, (c + 1) * LANES)
            kr_ref[:, sl] = _rope128(y[:, sl], cr_ref[...], sr_ref[...]) * (RET_HEAD_DIM ** -0.5)

    @pl.when((j > nq + 2 * nr) & (j <= nq + 3 * nr))
    def _vr():
        vr_ref[...] = y.astype(BF16)

    @pl.when(j > nq + 3 * nr)
    def _gate():
        gate_ref[...] = y


def _proj(x, ln, w, m128, gq, gk, ca, sa, cr, sr, d_attn, d_kv, d_ret, tm_target):
    n, d = x.shape
    assert d_attn % COL_TILE == 0 and 2 * d_kv == COL_TILE and d_ret % COL_TILE == 0
    nq, nr = d_attn // COL_TILE, d_ret // COL_TILE
    n_col = nq + 1 + 4 * nr
    tm = _pick_tile(n, tm_target)
    row = lambda i, j: (i, 0)
    clip = lambda lo, cnt: (lambda i, j: (i, jnp.clip(j - lo, 0, cnt - 1)))
    tab = pl.BlockSpec((tm, LANES), row)
    vec = pl.BlockSpec((1, LANES), lambda i, j: (0, 0))
    return pl.pallas_call(
        functools.partial(_proj_kernel, nq, nr),
        grid=(n // tm, n_col),
        in_specs=[
            pl.BlockSpec((tm, d), row),
            pl.BlockSpec((1, d), lambda i, j: (0, 0)),
            pl.BlockSpec((n_col, d, COL_TILE), lambda i, j: (0, 0, 0), pipeline_mode=pl.Buffered(1)),
            pl.BlockSpec((LANES, LANES), lambda i, j: (0, 0)),
            vec, vec, tab, tab, tab, tab,
        ],
        out_specs=[
            pl.BlockSpec((tm, COL_TILE), clip(0, nq)),
            pl.BlockSpec((tm, d_kv), row),
            pl.BlockSpec((tm, d_kv), row),
            pl.BlockSpec((tm, COL_TILE), clip(nq + 1, nr)),
            pl.BlockSpec((tm, COL_TILE), clip(nq + 1 + nr, nr)),
            pl.BlockSpec((tm, COL_TILE), clip(nq + 1 + 2 * nr, nr)),
            pl.BlockSpec((tm, COL_TILE), clip(nq + 1 + 3 * nr, nr)),
        ],
        out_shape=[
            jax.ShapeDtypeStruct((n, d_attn), BF16),
            jax.ShapeDtypeStruct((n, d_kv), F32),
            jax.ShapeDtypeStruct((n, d_kv), F32),
            jax.ShapeDtypeStruct((n, d_ret), BF16),
            jax.ShapeDtypeStruct((n, d_ret), F32),
            jax.ShapeDtypeStruct((n, d_ret), BF16),
            jax.ShapeDtypeStruct((n, d_ret), F32),
        ],
        scratch_shapes=[pltpu.VMEM((tm, d), BF16)],
        compiler_params=_params("parallel", "arbitrary"),
        name="proj",
    )(x, ln, w, m128, gq, gk, ca, sa, cr, sr)


def _attend(items, valid, sink_ref, t, n_pairs, g_sz):
    lane = lax.broadcasted_iota(jnp.int32, (1, LANES), 1)
    slot_a = ((lane // HALF) % 2) == 0
    first_half = lane < HEAD_DIM
    zero = jnp.zeros((), BF16)
    jobs = [(i, p) for i in range(len(items)) for p in range(n_pairs)]
    scores = []
    for i, p in jobs:
        qa, k_all, _ = items[i]
        parts = []
        for g in range(g_sz):
            qb = qa[:, (p * g_sz + g) * LANES:(p * g_sz + g + 1) * LANES]
            parts.append(jnp.where(slot_a, qb, zero))
            parts.append(jnp.where(slot_a, zero, qb))
        qs = jnp.concatenate(parts, axis=0)
        scores.append(lax.dot_general(qs, k_all[:, p * LANES:(p + 1) * LANES], (((1,), (1,)), ((), ())),
                                      preferred_element_type=F32))
    probs = []
    for (i, p), s in zip(jobs, scores):
        if valid is not None:
            s = jnp.where(valid, s, -jnp.inf)
        sk = sink_ref[p][:, 0:1]
        m = jnp.maximum(jnp.max(s, axis=-1, keepdims=True), sk)
        e = jnp.exp(s - m)
        den = jnp.sum(e, axis=-1, keepdims=True) + jnp.exp(sk - m)
        probs.append((e / den).astype(BF16))
    outs = [[] for _ in items]
    for (i, p), pn in zip(jobs, probs):
        o = jnp.dot(pn, items[i][2][:, p * LANES:(p + 1) * LANES], preferred_element_type=F32)
        for g in range(g_sz):
            o_a = o[(2 * g) * t:(2 * g + 1) * t]
            o_b = o[(2 * g + 1) * t:(2 * g + 2) * t]
            outs[i].append(jnp.where(first_half, o_a, o_b))
    return outs


def _retain(items, dmat_ref, decin_ref, deckv_ref, dchunk_ref, gng, gnb, n_heads):
    nt = (((1,), (1,)), ((), ()))
    tn = (((0,), (0,)), ((), ()))
    jobs = [(i, h) for i in range(len(items)) for h in range(n_heads)]
    sl = lambda h: slice(h * RET_HEAD_DIM, (h + 1) * RET_HEAD_DIM)
    intra, inter, r_prevs = [], [], []
    for i, h in jobs:
        qr, kr, _, _, r_in_ref, _ = items[i]
        q = qr[:, sl(h)]
        r_prev = r_in_ref[h]
        r_prevs.append(r_prev)
        intra.append(lax.dot_general(q, kr[:, sl(h)].astype(BF16), nt, preferred_element_type=F32) * dmat_ref[h])
        inter.append(jnp.dot(q, r_prev.astype(BF16), preferred_element_type=F32) * decin_ref[h])
    for (i, h), r_prev in zip(jobs, r_prevs):
        _, kr, vr, _, _, r_out_ref = items[i]
        kd = (kr[:, sl(h)] * deckv_ref[h]).astype(BF16)
        kv = lax.dot_general(kd, vr[:, sl(h)], tn, preferred_element_type=F32)
        r_out_ref[h] = r_prev * dchunk_ref[h] + kv
    outs = [[] for _ in items]
    for (i, h), s, o_inter in zip(jobs, intra, inter):
        _, _, vr, gate, _, _ = items[i]
        o = jnp.dot(s.astype(BF16), vr[:, sl(h)], preferred_element_type=F32) + o_inter
        mu = jnp.mean(o, axis=-1, keepdims=True)
        dev = o - mu
        var = jnp.mean(dev * dev, axis=-1, keepdims=True)
        r = dev * lax.rsqrt(var + GN_EPS) * gng[:, sl(h)] + gnb[:, sl(h)]
        gt = gate[:, sl(h)]
        outs[i].append(gt * jax.nn.sigmoid(gt) * r)
    return outs


def _store_merged(merged_ref, ao, ro, d_attn):
    for b, blk in enumerate(ao):
        merged_ref[:, b * LANES:(b + 1) * LANES] = blk.astype(BF16)
    for h, blk in enumerate(ro):
        merged_ref[:, d_attn + h * RET_HEAD_DIM:d_attn + (h + 1) * RET_HEAD_DIM] = blk.astype(BF16)


def _mixer_prompt_kernel(nc, t, batch, n_pairs, g_sz, n_heads, d_attn,
                         dchunk_ref, qa_ref, *rest):
    n_win = WINDOW // t + 1
    k_refs, v_refs = rest[:n_win], rest[n_win:2 * n_win]
    (qr_ref, kr_ref, vr_ref, gate_ref, sink_ref, dmat_ref, decin_ref, deckv_ref, gng_ref, gnb_ref,
     merged_ref, rout_ref, r_ref) = rest[2 * n_win:]
    c = pl.program_id(0)

    @pl.when(c == 0)
    def _reset():
        r_ref[...] = jnp.zeros_like(r_ref)

    kidx = lax.broadcasted_iota(jnp.int32, (1, n_win * t), 1)
    valid = kidx >= (n_win - 1 - c) * t
    att_items = [(qa_ref[b],
                  jnp.concatenate([r[b].astype(BF16) for r in k_refs], axis=0),
                  jnp.concatenate([r[b].astype(BF16) for r in v_refs], axis=0)) for b in range(batch)]
    ret_items = [(qr_ref[b], kr_ref[b], vr_ref[b], gate_ref[b], r_ref.at[b], r_ref.at[b]) for b in range(batch)]
    ao = _attend(att_items, valid, sink_ref, t, n_pairs, g_sz)
    ro = _retain(ret_items, dmat_ref, decin_ref, deckv_ref, dchunk_ref, gng_ref[...], gnb_ref[...], n_heads)
    for b in range(batch):
        _store_merged(merged_ref.at[b], ao[b], ro[b], d_attn)

    @pl.when(c == nc - 1)
    def _emit_state():
        rout_ref[...] = r_ref[...]


def _mixer_sample_kernel(t, n_pairs, g_sz, n_heads, d_attn,
                         dchunk_ref, qa_ref, ka_ref, va_ref, qr_ref, kr_ref, vr_ref, gate_ref,
                         ck_ref, cv_ref, r0_ref,
                         sink_ref, dmat_ref, decin_ref, deckv_ref, gng_ref, gnb_ref,
                         merged_ref, rout_ref):
    k_all = jnp.concatenate([ck_ref[...], ka_ref[...].astype(BF16)], axis=0)
    v_all = jnp.concatenate([cv_ref[...], va_ref[...].astype(BF16)], axis=0)
    ao = _attend([(qa_ref[...], k_all, v_all)], None, sink_ref, t, n_pairs, g_sz)
    ro = _retain([(qr_ref[...], kr_ref[...], vr_ref[...], gate_ref[...], r0_ref, rout_ref)],
                 dmat_ref, decin_ref, deckv_ref, dchunk_ref, gng_ref[...], gnb_ref[...], n_heads)
    _store_merged(merged_ref, ao[0], ro[0], d_attn)


def _sink_table(sinks, sink_heads, t):
    per_row = jnp.repeat(sinks.astype(F32)[sink_heads], t, axis=1)
    return jnp.broadcast_to(per_row[:, :, None], per_row.shape + (LANES,))


def _mixer_prompt(proj_out, sinks, sink_heads, gng, gnb, batch, seq, n_kv, n_q, n_heads):
    qa, ka, va, qr, kr, vr, gate = proj_out
    t = CHUNK
    nc = seq // t
    d_attn, d_kv, d_ret = n_q * HEAD_DIM, n_kv * HEAD_DIM, n_heads * RET_HEAD_DIM
    n_pairs, g_sz = n_kv // 2, n_q // n_kv
    dmat, decin, deckv, dchunk = _decay_tables(n_heads, t)
    sink_tab = _sink_table(sinks, sink_heads, t)
    seq3 = lambda a: a.reshape(batch, seq, a.shape[-1])
    qa, ka, va, qr, kr, vr, gate = map(seq3, proj_out)
    row = lambda c, *_: (0, c, 0)
    back = lambda k: (lambda c, *_: (0, jnp.maximum(c - k, 0), 0))
    const3 = lambda c, *_: (0, 0, 0)
    const2 = lambda c, *_: (0, 0)
    n_win = WINDOW // t + 1
    kv_specs = [pl.BlockSpec((batch, t, d_kv), back(n_win - 1 - s)) for s in range(n_win)]
    state_shape = (batch, n_heads, RET_HEAD_DIM, RET_HEAD_DIM)
    grid_spec = pltpu.PrefetchScalarGridSpec(
        num_scalar_prefetch=1,
        grid=(nc,),
        in_specs=[
            pl.BlockSpec((batch, t, d_attn), row), *kv_specs, *kv_specs,
            pl.BlockSpec((batch, t, d_ret), row), pl.BlockSpec((batch, t, d_ret), row),
            pl.BlockSpec((batch, t, d_ret), row), pl.BlockSpec((batch, t, d_ret), row),
            pl.BlockSpec(sink_tab.shape, const3), pl.BlockSpec(dmat.shape, const3),
            pl.BlockSpec(decin.shape, const3), pl.BlockSpec(deckv.shape, const3),
            pl.BlockSpec((1, d_ret), const2), pl.BlockSpec((1, d_ret), const2),
        ],
        out_specs=[
            pl.BlockSpec((batch, t, d_attn + d_ret), row),
            pl.BlockSpec(state_shape, lambda c, *_: (0, 0, 0, 0)),
        ],
        scratch_shapes=[pltpu.VMEM(state_shape, F32)],
    )
    merged, r_fin = pl.pallas_call(
        functools.partial(_mixer_prompt_kernel, nc, t, batch, n_pairs, g_sz, n_heads, d_attn),
        grid_spec=grid_spec,
        out_shape=[jax.ShapeDtypeStruct((batch, seq, d_attn + d_ret), BF16),
                   jax.ShapeDtypeStruct(state_shape, F32)],
        compiler_params=_params("arbitrary"),
        name="mixer_prompt",
    )(dchunk, qa, *([ka] * n_win), *([va] * n_win), qr, kr, vr, gate, sink_tab, dmat, decin, deckv, gng, gnb)
    return merged.reshape(batch * seq, d_attn + d_ret), r_fin


def _mixer_sample(proj_out, ck, cv, r0, sinks, sink_heads, gng, gnb, row0, dbatch, t, n_kv, n_q, n_heads):
    qa, ka, va, qr, kr, vr, gate = proj_out
    d_attn, d_kv, d_ret = n_q * HEAD_DIM, n_kv * HEAD_DIM, n_heads * RET_HEAD_DIM
    n_pairs, g_sz = n_kv // 2, n_q // n_kv
    dmat, decin, deckv, dchunk = _decay_tables(n_heads, t)
    sink_tab = _sink_table(sinks, sink_heads, t)
    blk0 = row0 // t
    row = lambda b, *_: (blk0 + b, 0)
    const3 = lambda b, *_: (0, 0, 0)
    const2 = lambda b, *_: (0, 0)
    state_spec = pl.BlockSpec((None, n_heads, RET_HEAD_DIM, RET_HEAD_DIM), lambda b, *_: (b, 0, 0, 0))
    n_hist = ck.shape[1]
    grid_spec = pltpu.PrefetchScalarGridSpec(
        num_scalar_prefetch=1,
        grid=(dbatch,),
        in_specs=[
            pl.BlockSpec((t, d_attn), row), pl.BlockSpec((t, d_kv), row), pl.BlockSpec((t, d_kv), row),
            pl.BlockSpec((t, d_ret), row), pl.BlockSpec((t, d_ret), row), pl.BlockSpec((t, d_ret), row),
            pl.BlockSpec((t, d_ret), row),
            pl.BlockSpec((None, n_hist, d_kv), lambda b, *_: (b, 0, 0)),
            pl.BlockSpec((None, n_hist, d_kv), lambda b, *_: (b, 0, 0)),
            state_spec,
            pl.BlockSpec(sink_tab.shape, const3), pl.BlockSpec(dmat.shape, const3),
            pl.BlockSpec(decin.shape, const3), pl.BlockSpec(deckv.shape, const3),
            pl.BlockSpec((1, d_ret), const2), pl.BlockSpec((1, d_ret), const2),
        ],
        out_specs=[pl.BlockSpec((t, d_attn + d_ret), lambda b, *_: (b, 0)), state_spec],
    )
    return pl.pallas_call(
        functools.partial(_mixer_sample_kernel, t, n_pairs, g_sz, n_heads, d_attn),
        grid_spec=grid_spec,
        out_shape=[jax.ShapeDtypeStruct((dbatch * t, d_attn + d_ret), BF16),
                   jax.ShapeDtypeStruct(r0.shape, F32)],
        compiler_params=_params("parallel"),
        name="mixer_sample",
    )(dchunk, qa, ka, va, qr, kr, vr, gate, ck, cv, r0, sink_tab, dmat, decin, deckv, gng, gnb)


def _outproj_kernel(m_ref, w_ref, x_ref, o_ref):
    o_ref[...] = x_ref[...] + jnp.dot(m_ref[...], w_ref[...], preferred_element_type=F32)


def _outproj(merged, w, x):
    n, d = x.shape
    k = merged.shape[1]
    tm = _pick_tile(n, 1024)
    tn = _pick_tile(d, 1024, LANES)
    return pl.pallas_call(
        _outproj_kernel,
        grid=(n // tm, d // tn),
        in_specs=[pl.BlockSpec((tm, k), lambda i, j: (i, 0)),
                  pl.BlockSpec((k, tn), lambda i, j: (0, j)),
                  pl.BlockSpec((tm, tn), lambda i, j: (i, j))],
        out_specs=pl.BlockSpec((tm, tn), lambda i, j: (i, j)),
        out_shape=jax.ShapeDtypeStruct((n, d), F32),
        compiler_params=_params("parallel", "arbitrary"),
        name="outproj",
    )(merged, w, x)


def _swiglu_step(xn, wg, wu, wd):
    g = jnp.dot(xn, wg, preferred_element_type=F32)
    u = jnp.dot(xn, wu, preferred_element_type=F32)
    h = (g * jax.nn.sigmoid(g) * u).astype(BF16)
    return jnp.dot(h, wd, preferred_element_type=F32)


def _ffn_kernel(x_ref, ln_ref, wg_ref, wu_ref, wd_ref, o_ref, xn_ref):
    f = pl.program_id(1)

    @pl.when(f == 0)
    def _norm():
        x = x_ref[...]
        ms = jnp.mean(x * x, axis=-1, keepdims=True)
        xn_ref[...] = (x * lax.rsqrt(ms + RMS_EPS) * ln_ref[...]).astype(BF16)
        o_ref[...] = x

    o_ref[...] += _swiglu_step(xn_ref[...], wg_ref[...], wu_ref[...], wd_ref[...])


def _ffn(x, ln, wg, wu, wd):
    n, d = x.shape
    fp = wg.shape[1]
    tf = _pick_tile(fp, 512, LANES)
    tm = _pick_tile(n, 1024)
    return pl.pallas_call(
        _ffn_kernel,
        grid=(n // tm, fp // tf),
        in_specs=[pl.BlockSpec((tm, d), lambda i, f: (i, 0)),
                  pl.BlockSpec((1, d), lambda i, f: (0, 0)),
                  pl.BlockSpec((d, tf), lambda i, f: (0, f)),
                  pl.BlockSpec((d, tf), lambda i, f: (0, f)),
                  pl.BlockSpec((tf, d), lambda i, f: (f, 0))],
        out_specs=pl.BlockSpec((tm, d), lambda i, f: (i, 0)),
        out_shape=jax.ShapeDtypeStruct((n, d), F32),
        scratch_shapes=[pltpu.VMEM((tm, d), BF16)],
        compiler_params=_params("parallel", "arbitrary"),
        name="ffn",
    )(x, ln, wg, wu, wd)


def _router_kernel(n_exp, x_ref, ln_ref, wr_hi_ref, wr_lo_ref, hn_ref, idx_ref, gate_ref):
    x = x_ref[...]
    ms = jnp.mean(x * x, axis=-1, keepdims=True)
    hf = x * lax.rsqrt(ms + RMS_EPS) * ln_ref[...]
    hn = hf.astype(BF16)
    bits = lax.bitcast_convert_type(hn.astype(F32), jnp.uint32)
    half = bits.shape[1] // 2
    hn_ref[...] = (bits[:, :half] >> 16) | (bits[:, half:] & jnp.uint32(0xFFFF0000))
    h_lo = (hf - hn.astype(F32)).astype(BF16)
    logits = (jnp.dot(hn, wr_hi_ref[...], preferred_element_type=F32)
              + jnp.dot(h_lo, wr_hi_ref[...], preferred_element_type=F32)
              + jnp.dot(hn, wr_lo_ref[...], preferred_element_type=F32))
    lane = lax.broadcasted_iota(jnp.int32, logits.shape, 1)
    logits = jnp.where(lane < n_exp, logits, -jnp.inf)
    m1 = jnp.max(logits, axis=-1, keepdims=True)
    i1 = jnp.min(jnp.where(logits == m1, lane, LANES), axis=-1, keepdims=True)
    rest = jnp.where(lane == i1, -jnp.inf, logits)
    m2 = jnp.max(rest, axis=-1, keepdims=True)
    i2 = jnp.min(jnp.where(rest == m2, lane, LANES), axis=-1, keepdims=True)
    e2 = jnp.exp(m2 - m1)
    den = 1.0 + e2
    idx_ref[...] = jnp.where(lane == 0, i1, jnp.where(lane == 1, i2, 0))
    gate_ref[...] = jnp.where(lane == 0, 1.0 / den, jnp.where(lane == 1, e2 / den, 0.0))


def _router(x, ln, wr_hi, wr_lo, n_exp):
    n, d = x.shape
    tm = _pick_tile(n, 512)
    row = lambda i: (i, 0)
    return pl.pallas_call(
        functools.partial(_router_kernel, n_exp),
        grid=(n // tm,),
        in_specs=[pl.BlockSpec((tm, d), row), pl.BlockSpec((1, d), lambda i: (0, 0)),
                  pl.BlockSpec((d, LANES), lambda i: (0, 0)), pl.BlockSpec((d, LANES), lambda i: (0, 0))],
        out_specs=[pl.BlockSpec((tm, d // 2), row), pl.BlockSpec((tm, LANES), row), pl.BlockSpec((tm, LANES), row)],
        out_shape=[jax.ShapeDtypeStruct((n, d // 2), jnp.uint32), jax.ShapeDtypeStruct((n, LANES), jnp.int32),
                   jax.ShapeDtypeStruct((n, LANES), F32)],
        compiler_params=_params("parallel"),
        name="router",
    )(x, ln, wr_hi, wr_lo)


def _experts_kernel(tm, n_split, te_ref, act_ref, tok_ref, hn_hbm, *rest):
    wg_refs, wu_refs, wd_refs = rest[:n_split], rest[n_split:2 * n_split], rest[2 * n_split:3 * n_split]
    o_ref, rows_ref, x_ref, wgb_ref, wub_ref, wdb_ref, sem = rest[3 * n_split:]
    i = pl.program_id(0)
    f = pl.program_id(1)
    active = act_ref[i] == 1

    @pl.when(f == 0)
    def _init():
        o_ref[...] = jnp.zeros_like(o_ref)

    @pl.when(active & (f == 0))
    def _gather():
        def start_row(r, carry):
            pltpu.make_async_copy(hn_hbm.at[pl.ds(tok_ref[0, r], 1)], rows_ref.at[pl.ds(r, 1)], sem).start()
            return carry

        lax.fori_loop(0, tm, start_row, 0, unroll=8)
        pltpu.make_async_copy(hn_hbm.at[pl.ds(0, tm)], rows_ref, sem).wait()
        words = rows_ref[...]
        half = words.shape[1]
        x_ref[:, :half] = lax.bitcast_convert_type(words << 16, F32).astype(BF16)
        x_ref[:, half:] = lax.bitcast_convert_type(words & jnp.uint32(0xFFFF0000), F32).astype(BF16)

    @pl.when(active)
    def _step():
        for slabs, dst in ((wg_refs, wgb_ref), (wu_refs, wub_ref), (wd_refs, wdb_ref)):
            rows = dst.shape[0] // n_split
            for k, slab in enumerate(slabs):
                dst[k * rows:(k + 1) * rows, :] = slab[...].astype(BF16)
        o_ref[...] += _swiglu_step(x_ref[...], wgb_ref[...], wub_ref[...], wdb_ref[...])


def _experts(hn32, src_tok, wg, wu, wd, tile_expert, tile_active, tm):
    n, half = hn32.shape
    d = 2 * half
    assert n >= tm
    n_tiles = src_tok.shape[0] // tm
    n_exp, _, fe = wg.shape
    tf = _pick_tile(fe, 512, LANES)
    nf = fe // tf
    fidx = lambda i, f, te, act: f * act[i] + (nf - 1) * (1 - act[i])
    n_split = EXPERT_WEIGHT_STREAMS
    in_slab = lambda k: (lambda i, f, te, act: (te[i], k, fidx(i, f, te, act)))
    out_slab = lambda k: (lambda i, f, te, act: (te[i], n_split * fidx(i, f, te, act) + k, 0))
    in_specs = [pl.BlockSpec((None, d // n_split, tf), in_slab(k)) for k in range(n_split)]
    out_specs_w = [pl.BlockSpec((None, tf // n_split, d), out_slab(k)) for k in range(n_split)]
    grid_spec = pltpu.PrefetchScalarGridSpec(
        num_scalar_prefetch=2,
        grid=(n_tiles, nf),
        in_specs=[pl.BlockSpec((None, 1, tm), lambda i, f, te, act: (i, 0, 0), memory_space=pltpu.SMEM),
                  pl.BlockSpec(memory_space=pl.ANY),
                  *in_specs, *in_specs, *out_specs_w],
        out_specs=pl.BlockSpec((tm, d), lambda i, f, te, act: (i, 0), pipeline_mode=pl.Buffered(1)),
        scratch_shapes=[pltpu.VMEM((tm, half), jnp.uint32), pltpu.VMEM((tm, d), BF16),
                        pltpu.VMEM((d, tf), BF16), pltpu.VMEM((d, tf), BF16), pltpu.VMEM((tf, d), BF16),
                        pltpu.SemaphoreType.DMA],
    )
    return pl.pallas_call(
        functools.partial(_experts_kernel, tm, n_split),
        grid_spec=grid_spec,
        out_shape=jax.ShapeDtypeStruct((n_tiles * tm, d), F32),
        compiler_params=_params("arbitrary", "arbitrary"),
        name="experts",
    )(tile_expert, tile_active, src_tok.reshape(n_tiles, 1, tm), hn32,
      *([wg] * n_split), *([wu] * n_split), *([wd] * n_split))


def _combine_kernel(x_ref, y0_ref, y1_ref, g_ref, o_ref):
    g = g_ref[...]
    o_ref[...] = x_ref[...] + (g[:, 0:1] * y0_ref[...] + g[:, 1:2] * y1_ref[...])


def _combine(x, y0, y1, gate, row0):
    n, d = x.shape
    tm = _pick_tile(n, 256)
    assert row0 % tm == 0
    blk0 = row0 // tm
    own = lambda i: (i, 0)
    shifted = lambda i: (blk0 + i, 0)
    return pl.pallas_call(
        _combine_kernel,
        grid=(n // tm,),
        in_specs=[pl.BlockSpec((tm, d), own), pl.BlockSpec((tm, d), shifted), pl.BlockSpec((tm, d), shifted),
                  pl.BlockSpec((tm, LANES), own)],
        out_specs=pl.BlockSpec((tm, d), own),
        out_shape=jax.ShapeDtypeStruct((n, d), F32),
        compiler_params=_params("parallel"),
        name="combine",
    )(x, y0, y1, gate)


def _moe(x_p, x_s, ln, w_router, wg, wu, wd):
    n_p, d = x_p.shape
    n = n_p + x_s.shape[0]
    n_exp = w_router.shape[1]
    wr = jnp.pad(w_router.astype(F32), ((0, 0), (0, LANES - n_exp)))
    wr_hi = wr.astype(BF16)
    wr_lo = (wr - wr_hi.astype(F32)).astype(BF16)
    hn_p, ridx_p, rgate_p = _router(x_p, ln, wr_hi, wr_lo, n_exp)
    hn_s, ridx_s, rgate_s = _router(x_s, ln, wr_hi, wr_lo, n_exp)
    hn = jnp.concatenate([hn_p, hn_s], axis=0)
    tm = 1024 if TOP_K * n >= 8192 else 64
    n_asg = TOP_K * n
    n_tiles = -(-(n_asg + n_exp * (tm - 1)) // tm)
    flat_e = jnp.concatenate([ridx_p[:, :TOP_K], ridx_s[:, :TOP_K]], axis=0).reshape(-1)
    onehot = (flat_e[:, None] == jnp.arange(n_exp, dtype=jnp.int32)[None, :]).astype(jnp.int32)
    csum = jnp.cumsum(onehot, axis=0)
    counts = csum[-1]
    rank = jnp.take_along_axis(csum, flat_e[:, None], axis=1)[:, 0] - 1
    padded = ((counts + tm - 1) // tm) * tm
    ends = jnp.cumsum(padded)
    pos = (ends - padded)[flat_e] + rank
    src_tok = jnp.zeros((n_tiles * tm,), jnp.int32).at[pos].set(jnp.arange(n_asg, dtype=jnp.int32) // TOP_K)
    tile_start = jnp.arange(n_tiles, dtype=jnp.int32) * tm
    te = jnp.sum((tile_start[:, None] >= ends[None, :]).astype(jnp.int32), axis=1)
    tile_active = (te < n_exp).astype(jnp.int32)
    tile_expert = jnp.minimum(te, n_exp - 1)
    ys = _experts(hn, src_tok, wg, wu, wd, tile_expert, tile_active, tm)
    pos2 = pos.reshape(n, TOP_K)
    y0 = jnp.take(ys, pos2[:, 0], axis=0)
    y1 = jnp.take(ys, pos2[:, 1], axis=0)
    return _combine(x_p, y0, y1, rgate_p, 0), _combine(x_s, y0, y1, rgate_s, n_p)


def kernel(x_prompt, x_sample, cache_attn_k, cache_attn_v, state_ret, ln_mix, w_in, q_norm, k_norm, attn_sinks,
           ret_gn_g, ret_gn_b, w_out, ln_ffn, ffn_w_gate, ffn_w_up, ffn_w_down, moe_router, moe_w_gate, moe_w_up,
           moe_w_down):
    batch, seq, d = x_prompt.shape
    dbatch, t_s, _ = x_sample.shape
    depth = ln_mix.shape[0]
    n_kv = cache_attn_k.shape[3]
    n_heads = state_ret.shape[2]
    d_kv = n_kv * HEAD_DIM
    d_ret = n_heads * RET_HEAD_DIM
    d_attn = w_out.shape[1] - d_ret
    n_q = d_attn // HEAD_DIM
    assert cache_attn_k.shape[4] == HEAD_DIM and state_ret.shape[3] == RET_HEAD_DIM and n_kv % 2 == 0
    assert seq % CHUNK == 0 and WINDOW % CHUNK == 0 and seq >= WINDOW and cache_attn_k.shape[2] == WINDOW
    assert w_in.shape[2] == d_attn + 2 * d_kv + 4 * d_ret
    n_p, n_s = batch * seq, dbatch * t_s
    keep = min(WINDOW, seq)

    perm_q, perm_k, perm_ao, sink_heads = _layout_perms(n_q, n_kv)
    inv_k = np.argsort(perm_k)
    col_perm = np.concatenate([perm_q, d_attn + perm_k, np.arange(d_attn + d_kv, w_in.shape[2])])
    row_perm = np.concatenate([perm_ao, np.arange(d_attn, d_attn + d_ret)])

    tabs_p = _rope_tables(jnp.tile(jnp.arange(seq), batch))
    tabs_s = _rope_tables(PAST_LEN + jnp.tile(jnp.arange(t_s), dbatch))
    lane = np.arange(LANES)
    m128 = jnp.asarray(((lane[:, None] // HALF) % 2) == ((lane[None, :] // HALF) % 2), BF16)
    head_gain = lambda g: jnp.tile(g.astype(F32).reshape(2, HALF), (1, 2)).reshape(1, LANES)

    x_p, x_s = x_prompt.reshape(n_p, d), x_sample.reshape(n_s, d)
    nk_p, nv_p, nr_p, nk_s, nv_s, nr_s = [], [], [], [], [], []
    for l in range(depth):
        w_in_l = w_in[l][:, col_perm].astype(BF16).reshape(d, -1, COL_TILE).transpose(1, 0, 2)
        w_out_l = w_out[l][row_perm, :].astype(BF16)
        ln, gq, gk = ln_mix[l][None, :], head_gain(q_norm[l]), head_gain(k_norm[l])
        gng, gnb = ret_gn_g[l][None, :], ret_gn_b[l][None, :]
        proj_p = _proj(x_p, ln, w_in_l, m128, gq, gk, *tabs_p, d_attn, d_kv, d_ret, 512)
        proj_s = _proj(x_s, ln, w_in_l, m128, gq, gk, *tabs_s, d_attn, d_kv, d_ret, 256)
        merged_p, r_p = _mixer_prompt(proj_p, attn_sinks[l], sink_heads, gng, gnb, batch, seq, n_kv, n_q, n_heads)
        ck = cache_attn_k[l].reshape(dbatch, WINDOW, d_kv)[:, :, perm_k].astype(BF16)
        cv = cache_attn_v[l].reshape(dbatch, WINDOW, d_kv).astype(BF16)
        merged_s, r_s = _mixer_sample(proj_s, ck, cv, state_ret[l], attn_sinks[l], sink_heads, gng, gnb,
                                      0, dbatch, t_s, n_kv, n_q, n_heads)
        x_p = _outproj(merged_p, w_out_l, x_p)
        x_s = _outproj(merged_s, w_out_l, x_s)

        ka_last = proj_p[1].reshape(batch, seq, d_kv)[:, seq - keep:]
        nk_p.append(ka_last[:, :, inv_k].reshape(batch, keep, n_kv, HEAD_DIM))
        nv_p.append(proj_p[2].reshape(batch, seq, n_kv, HEAD_DIM)[:, seq - keep:])
        nk_s.append(proj_s[1][:, inv_k].reshape(dbatch, t_s, n_kv, HEAD_DIM))
        nv_s.append(proj_s[2].reshape(dbatch, t_s, n_kv, HEAD_DIM))
        nr_p.append(r_p)
        nr_s.append(r_s)

        i = l // 2
        lf = ln_ffn[l][None, :]
        if l % 2 == 0:
            fpad = (-ffn_w_gate.shape[2]) % COL_TILE
            wg = jnp.pad(ffn_w_gate[i].astype(BF16), ((0, 0), (0, fpad)))
            wu = jnp.pad(ffn_w_up[i].astype(BF16), ((0, 0), (0, fpad)))
            wd = jnp.pad(ffn_w_down[i].astype(BF16), ((0, fpad), (0, 0)))
            x_p, x_s = _ffn(x_p, lf, wg, wu, wd), _ffn(x_s, lf, wg, wu, wd)
        else:
            x_p, x_s = _moe(x_p, x_s, lf, moe_router[i], moe_w_gate[i], moe_w_up[i], moe_w_down[i])

    return (x_p.reshape(batch, seq, d), x_s.reshape(dbatch, t_s, d),
            jnp.stack(nk_p), jnp.stack(nv_p), jnp.stack(nr_p), jnp.stack(nk_s), jnp.stack(nv_s), jnp.stack(nr_s))
```

```python
import functools

import numpy as np
import jax
import jax.numpy as jnp
from jax import lax
from jax.experimental import pallas as pl
from jax.experimental.pallas import tpu as pltpu

CHUNK = 64
WINDOW = 128
HEAD_DIM = 64
RET_HEAD_DIM = 128
PAST_LEN = 2048
ROPE_THETA = 10000.0
TOP_K = 2
RMS_EPS = 1e-6
GN_EPS = 1e-5

LANES = 128
HALF = HEAD_DIM // 2
VMEM_LIMIT = 56 * 1024 * 1024
COL_TILE = 512
EXPERT_WEIGHT_STREAMS = 1
F32 = jnp.float32
BF16 = jnp.bfloat16


def _pick_tile(n, target, mult=16):
    best = None
    for t in range(mult, min(n, target) + 1, mult):
        if n % t == 0:
            best = t
    assert best is not None, (n, target)
    return best


def _params(*sem):
    return pltpu.CompilerParams(dimension_semantics=sem, vmem_limit_bytes=VMEM_LIMIT)


def _layout_perms(n_q, n_kv):
    g_sz = n_q // n_kv
    n_pairs = n_kv // 2
    perm_k = np.zeros(n_kv * HEAD_DIM, np.int32)
    perm_q = np.zeros(n_q * HEAD_DIM, np.int32)
    perm_ao = np.zeros(n_q * HEAD_DIM, np.int32)
    sink_heads = np.zeros((n_pairs, 2 * g_sz), np.int32)
    for p in range(n_pairs):
        for hf in range(2):
            for s in range(2):
                for r in range(HALF):
                    perm_k[p * LANES + hf * 64 + s * HALF + r] = (2 * p + s) * HEAD_DIM + hf * HALF + r
        for g in range(g_sz):
            bq = p * g_sz + g
            for s in range(2):
                head = g_sz * (2 * p + s) + g
                sink_heads[p, 2 * g + s] = head
                for hf in range(2):
                    for r in range(HALF):
                        perm_q[bq * LANES + hf * 64 + s * HALF + r] = head * HEAD_DIM + hf * HALF + r
                for d in range(HEAD_DIM):
                    perm_ao[bq * LANES + s * HEAD_DIM + d] = head * HEAD_DIM + d
    return perm_q, perm_k, perm_ao, sink_heads


def _rope_tables(pos):
    pf = pos.astype(F32)[:, None]
    inv_a = ROPE_THETA ** (-jnp.arange(0, HEAD_DIM, 2, dtype=F32) / HEAD_DIM)
    ang = pf * inv_a[None, :]
    c, s = jnp.cos(ang), jnp.sin(ang)
    ca = jnp.concatenate([c, c, c, c], axis=-1)
    sa = jnp.concatenate([-s, -s, s, s], axis=-1)
    inv_r = 10000.0 ** (-jnp.linspace(0.0, 1.0, RET_HEAD_DIM // 2, dtype=F32))
    ang = pf * inv_r[None, :]
    c, s = jnp.cos(ang), jnp.sin(ang)
    cr = jnp.concatenate([c, c], axis=-1)
    sr = jnp.concatenate([-s, s], axis=-1)
    return ca, sa, cr, sr


def _decay_tables(n_heads, chunk):
    log_g = jnp.log1p(-jnp.exp2(-5.0 - jnp.arange(n_heads, dtype=F32)))
    idx = jnp.arange(chunk, dtype=F32)
    dmat = jnp.exp(log_g[:, None, None] * jnp.abs(idx[:, None] - idx[None, :]))
    dec_in = jnp.exp(log_g[None, :] * (idx[:, None] + 1.0))
    dec_kv = jnp.exp(log_g[None, :] * (chunk - 1.0 - idx[:, None]))
    dec_chunk = jnp.exp(log_g * chunk)
    bc = lambda t: jnp.broadcast_to(t.T[:, :, None], (n_heads, chunk, LANES))
    return dmat, bc(dec_in), bc(dec_kv), dec_chunk


def _rope128(y, c, s):
    return y * c + pltpu.roll(y, 64, 1) * s


def _proj_kernel(nq, nr, x_ref, ln_ref, w_ref, m_ref, gq_ref, gk_ref, ca_ref, sa_ref, cr_ref, sr_ref,
                 qa_ref, ka_ref, va_ref, qr_ref, kr_ref, vr_ref, gate_ref, xn_ref):
    j = pl.program_id(1)

    @pl.when(j == 0)
    def _norm():
        x = x_ref[...]
        ms = jnp.mean(x * x, axis=-1, keepdims=True)
        xn_ref[...] = (x * lax.rsqrt(ms + RMS_EPS) * ln_ref[...]).astype(BF16)

    y = jnp.dot(xn_ref[...], w_ref[j], preferred_element_type=F32)
    n_sub = COL_TILE // LANES

    def head_norm_rope(yc, g):
        y2 = yc * yc
        hi = y2.astype(BF16)
        lo = (y2 - hi.astype(F32)).astype(BF16)
        ss = (jnp.dot(hi, m_ref[...], preferred_element_type=F32)
              + jnp.dot(lo, m_ref[...], preferred_element_type=F32))
        yn = yc * lax.rsqrt(ss * (1.0 / HEAD_DIM) + RMS_EPS) * g
        return _rope128(yn, ca_ref[...], sa_ref[...])

    @pl.when(j < nq)
    def _qa():
        for c in range(n_sub):
            sl = slice(c * LANES, (c + 1) * LANES)
            qa_ref[:, sl] = (head_norm_rope(y[:, sl], gq_ref[...]) * (HEAD_DIM ** -0.5)).astype(BF16)

    @pl.when(j == nq)
    def _kv():
        half = COL_TILE // 2
        for c in range(half // LANES):
            sl = slice(c * LANES, (c + 1) * LANES)
            ka_ref[:, sl] = head_norm_rope(y[:, sl], gk_ref[...])
        va_ref[...] = y[:, half:]

    @pl.when((j > nq) & (j <= nq + nr))
    def _qr():
        for c in range(n_sub):
            sl = slice(c * LANES, (c + 1) * LANES)
            qr_ref[:, sl] = _rope128(y[:, sl], cr_ref[...], sr_ref[...]).astype(BF16)

    @pl.when((j > nq + nr) & (j <= nq + 2 * nr))
    def _kr():
        for c in range(n_sub):
            sl = slice(c * LANES, (c + 1) * LANES)
            kr_ref[:, sl] = _rope128(y[:, sl], cr_ref[...], sr_ref[...]) * (RET_HEAD_DIM ** -0.5)

    @pl.when((j > nq + 2 * nr) & (j <= nq + 3 * nr))
    def _vr():
        vr_ref[...] = y.astype(BF16)

    @pl.when(j > nq + 3 * nr)
    def _gate():
        gate_ref[...] = y


def _proj(x, ln, w, m128, gq, gk, ca, sa, cr, sr, d_attn, d_kv, d_ret, tm_target):
    n, d = x.shape
    assert d_attn % COL_TILE == 0 and 2 * d_kv == COL_TILE and d_ret % COL_TILE == 0
    nq, nr = d_attn // COL_TILE, d_ret // COL_TILE
    n_col = nq + 1 + 4 * nr
    tab_rows = ca.shape[0]
    assert n % tab_rows == 0
    tm = _pick_tile(tab_rows, tm_target)
    row = lambda i, j: (i, 0)
    clip = lambda lo, cnt: (lambda i, j: (i, jnp.clip(j - lo, 0, cnt - 1)))
    tab = pl.BlockSpec((tm, LANES), lambda i, j: (i % (tab_rows // tm), 0))
    vec = pl.BlockSpec((1, LANES), lambda i, j: (0, 0))
    return pl.pallas_call(
        functools.partial(_proj_kernel, nq, nr),
        grid=(n // tm, n_col),
        in_specs=[
            pl.BlockSpec((tm, d), row),
            pl.BlockSpec((1, d), lambda i, j: (0, 0)),
            pl.BlockSpec((n_col, d, COL_TILE), lambda i, j: (0, 0, 0), pipeline_mode=pl.Buffered(1)),
            pl.BlockSpec((LANES, LANES), lambda i, j: (0, 0)),
            vec, vec, tab, tab, tab, tab,
        ],
        out_specs=[
            pl.BlockSpec((tm, COL_TILE), clip(0, nq)),
            pl.BlockSpec((tm, d_kv), row),
            pl.BlockSpec((tm, d_kv), row),
            pl.BlockSpec((tm, COL_TILE), clip(nq + 1, nr)),
            pl.BlockSpec((tm, COL_TILE), clip(nq + 1 + nr, nr)),
            pl.BlockSpec((tm, COL_TILE), clip(nq + 1 + 2 * nr, nr)),
            pl.BlockSpec((tm, COL_TILE), clip(nq + 1 + 3 * nr, nr)),
        ],
        out_shape=[
            jax.ShapeDtypeStruct((n, d_attn), BF16),
            jax.ShapeDtypeStruct((n, d_kv), F32),
            jax.ShapeDtypeStruct((n, d_kv), F32),
            jax.ShapeDtypeStruct((n, d_ret), BF16),
            jax.ShapeDtypeStruct((n, d_ret), F32),
            jax.ShapeDtypeStruct((n, d_ret), BF16),
            jax.ShapeDtypeStruct((n, d_ret), F32),
        ],
        scratch_shapes=[pltpu.VMEM((tm, d), BF16)],
        compiler_params=_params("parallel", "arbitrary"),
        name="proj",
    )(x, ln, w, m128, gq, gk, ca, sa, cr, sr)


def _attend(items, valid, sink_ref, t, n_pairs, g_sz):
    lane = lax.broadcasted_iota(jnp.int32, (1, LANES), 1)
    slot_a = ((lane // HALF) % 2) == 0
    first_half = lane < HEAD_DIM
    zero = jnp.zeros((), BF16)
    jobs = [(i, p) for i in range(len(items)) for p in range(n_pairs)]
    scores = []
    for i, p in jobs:
        qa, k_all, _ = items[i]
        parts = []
        for g in range(g_sz):
            qb = qa[:, (p * g_sz + g) * LANES:(p * g_sz + g + 1) * LANES]
            parts.append(jnp.where(slot_a, qb, zero))
            parts.append(jnp.where(slot_a, zero, qb))
        qs = jnp.concatenate(parts, axis=0)
        scores.append(lax.dot_general(qs, k_all[:, p * LANES:(p + 1) * LANES], (((1,), (1,)), ((), ())),
                                      preferred_element_type=F32))
    probs = []
    for (i, p), s in zip(jobs, scores):
        if valid is not None:
            s = jnp.where(valid, s, -jnp.inf)
        sk = sink_ref[p][:, 0:1]
        m = jnp.maximum(jnp.max(s, axis=-1, keepdims=True), sk)
        e = jnp.exp(s - m)
        den = jnp.sum(e, axis=-1, keepdims=True) + jnp.exp(sk - m)
        probs.append((e / den).astype(BF16))
    outs = [[] for _ in items]
    for (i, p), pn in zip(jobs, probs):
        o = jnp.dot(pn, items[i][2][:, p * LANES:(p + 1) * LANES], preferred_element_type=F32)
        for g in range(g_sz):
            o_a = o[(2 * g) * t:(2 * g + 1) * t]
            o_b = o[(2 * g + 1) * t:(2 * g + 2) * t]
            outs[i].append(jnp.where(first_half, o_a, o_b))
    return outs


def _retain(items, dmat_ref, decin_ref, deckv_ref, dchunk_ref, gng, gnb, n_heads):
    nt = (((1,), (1,)), ((), ()))
    tn = (((0,), (0,)), ((), ()))
    jobs = [(i, h) for i in range(len(items)) for h in range(n_heads)]
    sl = lambda h: slice(h * RET_HEAD_DIM, (h + 1) * RET_HEAD_DIM)
    intra, inter, r_prevs = [], [], []
    for i, h in jobs:
        qr, kr, _, _, r_in_ref, _ = items[i]
        q = qr[:, sl(h)]
        r_prev = r_in_ref[h]
        r_prevs.append(r_prev)
        intra.append(lax.dot_general(q, kr[:, sl(h)].astype(BF16), nt, preferred_element_type=F32) * dmat_ref[h])
        inter.append(jnp.dot(q, r_prev.astype(BF16), preferred_element_type=F32) * decin_ref[h])
    for (i, h), r_prev in zip(jobs, r_prevs):
        _, kr, vr, _, _, r_out_ref = items[i]
        kd = (kr[:, sl(h)] * deckv_ref[h]).astype(BF16)
        kv = lax.dot_general(kd, vr[:, sl(h)], tn, preferred_element_type=F32)
        r_out_ref[h] = r_prev * dchunk_ref[h] + kv
    outs = [[] for _ in items]
    for (i, h), s, o_inter in zip(jobs, intra, inter):
        _, _, vr, gate, _, _ = items[i]
        o = jnp.dot(s.astype(BF16), vr[:, sl(h)], preferred_element_type=F32) + o_inter
        mu = jnp.mean(o, axis=-1, keepdims=True)
        dev = o - mu
        var = jnp.mean(dev * dev, axis=-1, keepdims=True)
        r = dev * lax.rsqrt(var + GN_EPS) * gng[:, sl(h)] + gnb[:, sl(h)]
        gt = gate[:, sl(h)]
        outs[i].append(gt * jax.nn.sigmoid(gt) * r)
    return outs


def _store_merged(merged_ref, ao, ro, d_attn):
    for b, blk in enumerate(ao):
        merged_ref[:, b * LANES:(b + 1) * LANES] = blk.astype(BF16)
    for h, blk in enumerate(ro):
        merged_ref[:, d_attn + h * RET_HEAD_DIM:d_attn + (h + 1) * RET_HEAD_DIM] = blk.astype(BF16)


def _mixer_prompt_kernel(nc, t, batch, n_pairs, g_sz, n_heads, d_attn,
                         dchunk_ref, qa_ref, *rest):
    n_win = WINDOW // t + 1
    k_refs, v_refs = rest[:n_win], rest[n_win:2 * n_win]
    (qr_ref, kr_ref, vr_ref, gate_ref, sink_ref, dmat_ref, decin_ref, deckv_ref, gng_ref, gnb_ref,
     merged_ref, rout_ref, r_ref) = rest[2 * n_win:]
    c = pl.program_id(0)

    @pl.when(c == 0)
    def _reset():
        r_ref[...] = jnp.zeros_like(r_ref)

    kidx = lax.broadcasted_iota(jnp.int32, (1, n_win * t), 1)
    valid = kidx >= (n_win - 1 - c) * t
    att_items = [(qa_ref[b],
                  jnp.concatenate([r[b].astype(BF16) for r in k_refs], axis=0),
                  jnp.concatenate([r[b].astype(BF16) for r in v_refs], axis=0)) for b in range(batch)]
    ret_items = [(qr_ref[b], kr_ref[b], vr_ref[b], gate_ref[b], r_ref.at[b], r_ref.at[b]) for b in range(batch)]
    ao = _attend(att_items, valid, sink_ref, t, n_pairs, g_sz)
    ro = _retain(ret_items, dmat_ref, decin_ref, deckv_ref, dchunk_ref, gng_ref[...], gnb_ref[...], n_heads)
    for b in range(batch):
        _store_merged(merged_ref.at[b], ao[b], ro[b], d_attn)

    @pl.when(c == nc - 1)
    def _emit_state():
        rout_ref[...] = r_ref[...]


def _mixer_sample_kernel(t, n_pairs, g_sz, n_heads, d_attn,
                         dchunk_ref, qa_ref, ka_ref, va_ref, qr_ref, kr_ref, vr_ref, gate_ref,
                         ck_ref, cv_ref, r0_ref,
                         sink_ref, dmat_ref, decin_ref, deckv_ref, gng_ref, gnb_ref,
                         merged_ref, rout_ref):
    k_all = jnp.concatenate([ck_ref[...], ka_ref[...].astype(BF16)], axis=0)
    v_all = jnp.concatenate([cv_ref[...], va_ref[...].astype(BF16)], axis=0)
    ao = _attend([(qa_ref[...], k_all, v_all)], None, sink_ref, t, n_pairs, g_sz)
    ro = _retain([(qr_ref[...], kr_ref[...], vr_ref[...], gate_ref[...], r0_ref, rout_ref)],
                 dmat_ref, decin_ref, deckv_ref, dchunk_ref, gng_ref[...], gnb_ref[...], n_heads)
    _store_merged(merged_ref, ao[0], ro[0], d_attn)


def _sink_table(sinks, sink_heads, t):
    per_row = jnp.repeat(sinks.astype(F32)[sink_heads], t, axis=1)
    return jnp.broadcast_to(per_row[:, :, None], per_row.shape + (LANES,))


def _mixer_prompt(proj_out, sinks, sink_heads, gng, gnb, batch, seq, n_kv, n_q, n_heads):
    qa, ka, va, qr, kr, vr, gate = proj_out
    t = CHUNK
    nc = seq // t
    d_attn, d_kv, d_ret = n_q * HEAD_DIM, n_kv * HEAD_DIM, n_heads * RET_HEAD_DIM
    n_pairs, g_sz = n_kv // 2, n_q // n_kv
    dmat, decin, deckv, dchunk = _decay_tables(n_heads, t)
    sink_tab = _sink_table(sinks, sink_heads, t)
    seq3 = lambda a: a.reshape(batch, seq, a.shape[-1])
    qa, ka, va, qr, kr, vr, gate = map(seq3, proj_out)
    row = lambda c, *_: (0, c, 0)
    back = lambda k: (lambda c, *_: (0, jnp.maximum(c - k, 0), 0))
    const3 = lambda c, *_: (0, 0, 0)
    const2 = lambda c, *_: (0, 0)
    n_win = WINDOW // t + 1
    kv_specs = [pl.BlockSpec((batch, t, d_kv), back(n_win - 1 - s)) for s in range(n_win)]
    state_shape = (batch, n_heads, RET_HEAD_DIM, RET_HEAD_DIM)
    grid_spec = pltpu.PrefetchScalarGridSpec(
        num_scalar_prefetch=1,
        grid=(nc,),
        in_specs=[
            pl.BlockSpec((batch, t, d_attn), row), *kv_specs, *kv_specs,
            pl.BlockSpec((batch, t, d_ret), row), pl.BlockSpec((batch, t, d_ret), row),
            pl.BlockSpec((batch, t, d_ret), row), pl.BlockSpec((batch, t, d_ret), row),
            pl.BlockSpec(sink_tab.shape, const3), pl.BlockSpec(dmat.shape, const3),
            pl.BlockSpec(decin.shape, const3), pl.BlockSpec(deckv.shape, const3),
            pl.BlockSpec((1, d_ret), const2), pl.BlockSpec((1, d_ret), const2),
        ],
        out_specs=[
            pl.BlockSpec((batch, t, d_attn + d_ret), row),
            pl.BlockSpec(state_shape, lambda c, *_: (0, 0, 0, 0)),
        ],
        scratch_shapes=[pltpu.VMEM(state_shape, F32)],
    )
    merged, r_fin = pl.pallas_call(
        functools.partial(_mixer_prompt_kernel, nc, t, batch, n_pairs, g_sz, n_heads, d_attn),
        grid_spec=grid_spec,
        out_shape=[jax.ShapeDtypeStruct((batch, seq, d_attn + d_ret), BF16),
                   jax.ShapeDtypeStruct(state_shape, F32)],
        compiler_params=_params("arbitrary"),
        name="mixer_prompt",
    )(dchunk, qa, *([ka] * n_win), *([va] * n_win), qr, kr, vr, gate, sink_tab, dmat, decin, deckv, gng, gnb)
    return merged.reshape(batch * seq, d_attn + d_ret), r_fin


def _mixer_sample(proj_out, ck, cv, r0, sinks, sink_heads, gng, gnb, row0, dbatch, t, n_kv, n_q, n_heads):
    qa, ka, va, qr, kr, vr, gate = proj_out
    d_attn, d_kv, d_ret = n_q * HEAD_DIM, n_kv * HEAD_DIM, n_heads * RET_HEAD_DIM
    n_pairs, g_sz = n_kv // 2, n_q // n_kv
    dmat, decin, deckv, dchunk = _decay_tables(n_heads, t)
    sink_tab = _sink_table(sinks, sink_heads, t)
    blk0 = row0 // t
    row = lambda b, *_: (blk0 + b, 0)
    const3 = lambda b, *_: (0, 0, 0)
    const2 = lambda b, *_: (0, 0)
    state_spec = pl.BlockSpec((None, n_heads, RET_HEAD_DIM, RET_HEAD_DIM), lambda b, *_: (b, 0, 0, 0))
    n_hist = ck.shape[1]
    grid_spec = pltpu.PrefetchScalarGridSpec(
        num_scalar_prefetch=1,
        grid=(dbatch,),
        in_specs=[
            pl.BlockSpec((t, d_attn), row), pl.BlockSpec((t, d_kv), row), pl.BlockSpec((t, d_kv), row),
            pl.BlockSpec((t, d_ret), row), pl.BlockSpec((t, d_ret), row), pl.BlockSpec((t, d_ret), row),
            pl.BlockSpec((t, d_ret), row),
            pl.BlockSpec((None, n_hist, d_kv), lambda b, *_: (b, 0, 0)),
            pl.BlockSpec((None, n_hist, d_kv), lambda b, *_: (b, 0, 0)),
            state_spec,
            pl.BlockSpec(sink_tab.shape, const3), pl.BlockSpec(dmat.shape, const3),
            pl.BlockSpec(decin.shape, const3), pl.BlockSpec(deckv.shape, const3),
            pl.BlockSpec((1, d_ret), const2), pl.BlockSpec((1, d_ret), const2),
        ],
        out_specs=[pl.BlockSpec((t, d_attn + d_ret), lambda b, *_: (b, 0)), state_spec],
    )
    return pl.pallas_call(
        functools.partial(_mixer_sample_kernel, t, n_pairs, g_sz, n_heads, d_attn),
        grid_spec=grid_spec,
        out_shape=[jax.ShapeDtypeStruct((dbatch * t, d_attn + d_ret), BF16),
                   jax.ShapeDtypeStruct(r0.shape, F32)],
        compiler_params=_params("parallel"),
        name="mixer_sample",
    )(dchunk, qa, ka, va, qr, kr, vr, gate, ck, cv, r0, sink_tab, dmat, decin, deckv, gng, gnb)


def _outproj_kernel(m_ref, w_ref, x_ref, o_ref):
    o_ref[...] = x_ref[...] + jnp.dot(m_ref[...], w_ref[...], preferred_element_type=F32)


def _outproj(merged, w, x):
    n, d = x.shape
    k = merged.shape[1]
    tm = _pick_tile(n, 1024)
    tn = _pick_tile(d, 1024, LANES)
    return pl.pallas_call(
        _outproj_kernel,
        grid=(n // tm, d // tn),
        in_specs=[pl.BlockSpec((tm, k), lambda i, j: (i, 0)),
                  pl.BlockSpec((k, tn), lambda i, j: (0, j)),
                  pl.BlockSpec((tm, tn), lambda i, j: (i, j))],
        out_specs=pl.BlockSpec((tm, tn), lambda i, j: (i, j)),
        out_shape=jax.ShapeDtypeStruct((n, d), F32),
        compiler_params=_params("parallel", "arbitrary"),
        name="outproj",
    )(merged, w, x)


def _swiglu_step(xn, wg, wu, wd):
    g = jnp.dot(xn, wg, preferred_element_type=F32)
    u = jnp.dot(xn, wu, preferred_element_type=F32)
    h = (g * jax.nn.sigmoid(g) * u).astype(BF16)
    return jnp.dot(h, wd, preferred_element_type=F32)


def _ffn_kernel(x_ref, ln_ref, wg_ref, wu_ref, wd_ref, o_ref, xn_ref):
    f = pl.program_id(1)

    @pl.when(f == 0)
    def _norm():
        x = x_ref[...]
        ms = jnp.mean(x * x, axis=-1, keepdims=True)
        xn_ref[...] = (x * lax.rsqrt(ms + RMS_EPS) * ln_ref[...]).astype(BF16)
        o_ref[...] = x

    o_ref[...] += _swiglu_step(xn_ref[...], wg_ref[...], wu_ref[...], wd_ref[...])


def _ffn(x, ln, wg, wu, wd):
    n, d = x.shape
    fp = wg.shape[1]
    tf = _pick_tile(fp, 512, LANES)
    tm = _pick_tile(n, 1024)
    return pl.pallas_call(
        _ffn_kernel,
        grid=(n // tm, fp // tf),
        in_specs=[pl.BlockSpec((tm, d), lambda i, f: (i, 0)),
                  pl.BlockSpec((1, d), lambda i, f: (0, 0)),
                  pl.BlockSpec((d, tf), lambda i, f: (0, f)),
                  pl.BlockSpec((d, tf), lambda i, f: (0, f)),
                  pl.BlockSpec((tf, d), lambda i, f: (f, 0))],
        out_specs=pl.BlockSpec((tm, d), lambda i, f: (i, 0)),
        out_shape=jax.ShapeDtypeStruct((n, d), F32),
        scratch_shapes=[pltpu.VMEM((tm, d), BF16)],
        compiler_params=_params("parallel", "arbitrary"),
        name="ffn",
    )(x, ln, wg, wu, wd)


def _router_kernel(n_exp, x_ref, ln_ref, wr_hi_ref, wr_lo_ref, hn_ref, idx_ref, gate_ref):
    x = x_ref[...]
    ms = jnp.mean(x * x, axis=-1, keepdims=True)
    hf = x * lax.rsqrt(ms + RMS_EPS) * ln_ref[...]
    hn = hf.astype(BF16)
    bits = lax.bitcast_convert_type(hn.astype(F32), jnp.uint32)
    half = bits.shape[1] // 2
    hn_ref[...] = (bits[:, :half] >> 16) | (bits[:, half:] & jnp.uint32(0xFFFF0000))
    h_lo = (hf - hn.astype(F32)).astype(BF16)
    logits = (jnp.dot(hn, wr_hi_ref[...], preferred_element_type=F32)
              + jnp.dot(h_lo, wr_hi_ref[...], preferred_element_type=F32)
              + jnp.dot(hn, wr_lo_ref[...], preferred_element_type=F32))
    lane = lax.broadcasted_iota(jnp.int32, logits.shape, 1)
    logits = jnp.where(lane < n_exp, logits, -jnp.inf)
    m1 = jnp.max(logits, axis=-1, keepdims=True)
    i1 = jnp.min(jnp.where(logits == m1, lane, LANES), axis=-1, keepdims=True)
    rest = jnp.where(lane == i1, -jnp.inf, logits)
    m2 = jnp.max(rest, axis=-1, keepdims=True)
    i2 = jnp.min(jnp.where(rest == m2, lane, LANES), axis=-1, keepdims=True)
    e2 = jnp.exp(m2 - m1)
    den = 1.0 + e2
    idx_ref[...] = jnp.where(lane == 0, i1, jnp.where(lane == 1, i2, 0))
    gate_ref[...] = jnp.where(lane == 0, 1.0 / den, jnp.where(lane == 1, e2 / den, 0.0))


def _router(x, ln, wr_hi, wr_lo, n_exp):
    n, d = x.shape
    tm = _pick_tile(n, 512)
    row = lambda i: (i, 0)
    return pl.pallas_call(
        functools.partial(_router_kernel, n_exp),
        grid=(n // tm,),
        in_specs=[pl.BlockSpec((tm, d), row), pl.BlockSpec((1, d), lambda i: (0, 0)),
                  pl.BlockSpec((d, LANES), lambda i: (0, 0)), pl.BlockSpec((d, LANES), lambda i: (0, 0))],
        out_specs=[pl.BlockSpec((tm, d // 2), row), pl.BlockSpec((tm, LANES), row), pl.BlockSpec((tm, LANES), row)],
        out_shape=[jax.ShapeDtypeStruct((n, d // 2), jnp.uint32), jax.ShapeDtypeStruct((n, LANES), jnp.int32),
                   jax.ShapeDtypeStruct((n, LANES), F32)],
        compiler_params=_params("parallel"),
        name="router",
    )(x, ln, wr_hi, wr_lo)


def _experts_kernel(tm, n_split, te_ref, act_ref, tok_ref, hn_hbm, *rest):
    wg_refs, wu_refs, wd_refs = rest[:n_split], rest[n_split:2 * n_split], rest[2 * n_split:3 * n_split]
    o_ref, rows_ref, x_ref, wgb_ref, wub_ref, wdb_ref, sem = rest[3 * n_split:]
    i = pl.program_id(0)
    f = pl.program_id(1)
    active = act_ref[i] > 0

    @pl.when(f == 0)
    def _init():
        o_ref[...] = jnp.zeros_like(o_ref)

    @pl.when(active & (f == 0))
    def _gather():
        def start_row(r, carry):
            pltpu.make_async_copy(hn_hbm.at[pl.ds(tok_ref[0, r], 1)], rows_ref.at[pl.ds(r, 1)], sem).start()
            return carry

        lax.fori_loop(0, tm, start_row, 0, unroll=8)
        pltpu.make_async_copy(hn_hbm.at[pl.ds(0, tm)], rows_ref, sem).wait()
        words = rows_ref[...]
        half = words.shape[1]
        x_ref[:, :half] = lax.bitcast_convert_type(words << 16, F32).astype(BF16)
        x_ref[:, half:] = lax.bitcast_convert_type(words & jnp.uint32(0xFFFF0000), F32).astype(BF16)

    def round_weights():
        for slabs, dst in ((wg_refs, wgb_ref), (wu_refs, wub_ref), (wd_refs, wdb_ref)):
            rows = dst.shape[0] // n_split
            for k, slab in enumerate(slabs):
                dst[k * rows:(k + 1) * rows, :] = slab[...].astype(BF16)

    @pl.when(act_ref[i] == 1)
    def _step():
        round_weights()
        o_ref[...] += _swiglu_step(x_ref[...], wgb_ref[...], wub_ref[...], wdb_ref[...])

    @pl.when(act_ref[i] == 2)
    def _half_step():
        round_weights()
        o_ref[:tm // 2, :] += _swiglu_step(x_ref[:tm // 2, :], wgb_ref[...], wub_ref[...], wdb_ref[...])


def _experts(hn32, src_tok, wg, wu, wd, tile_expert, tile_active, tm):
    n, half = hn32.shape
    d = 2 * half
    assert n >= tm
    n_tiles = src_tok.shape[0] // tm
    n_exp, _, fe = wg.shape
    tf = _pick_tile(fe, 512, LANES)
    nf = fe // tf
    fidx = lambda i, f, te, act: jnp.where(act[i] > 0, f, nf - 1)
    n_split = EXPERT_WEIGHT_STREAMS
    in_slab = lambda k: (lambda i, f, te, act: (te[i], k, fidx(i, f, te, act)))
    out_slab = lambda k: (lambda i, f, te, act: (te[i], n_split * fidx(i, f, te, act) + k, 0))
    in_specs = [pl.BlockSpec((None, d // n_split, tf), in_slab(k)) for k in range(n_split)]
    out_specs_w = [pl.BlockSpec((None, tf // n_split, d), out_slab(k)) for k in range(n_split)]
    grid_spec = pltpu.PrefetchScalarGridSpec(
        num_scalar_prefetch=2,
        grid=(n_tiles, nf),
        in_specs=[pl.BlockSpec((None, 1, tm), lambda i, f, te, act: (i, 0, 0), memory_space=pltpu.SMEM),
                  pl.BlockSpec(memory_space=pl.ANY),
                  *in_specs, *in_specs, *out_specs_w],
        out_specs=pl.BlockSpec((tm, d), lambda i, f, te, act: (i, 0), pipeline_mode=pl.Buffered(1)),
        scratch_shapes=[pltpu.VMEM((tm, half), jnp.uint32), pltpu.VMEM((tm, d), BF16),
                        pltpu.VMEM((d, tf), BF16), pltpu.VMEM((d, tf), BF16), pltpu.VMEM((tf, d), BF16),
                        pltpu.SemaphoreType.DMA],
    )
    return pl.pallas_call(
        functools.partial(_experts_kernel, tm, n_split),
        grid_spec=grid_spec,
        out_shape=jax.ShapeDtypeStruct((n_tiles * tm, d), F32),
        compiler_params=_params("arbitrary", "arbitrary"),
        name="experts",
    )(tile_expert, tile_active, src_tok.reshape(n_tiles, 1, tm), hn32,
      *([wg] * n_split), *([wu] * n_split), *([wd] * n_split))


def _combine_kernel(x_ref, y0_ref, y1_ref, g_ref, o_ref):
    g = g_ref[...]
    o_ref[...] = x_ref[...] + (g[:, 0:1] * y0_ref[...] + g[:, 1:2] * y1_ref[...])


def _combine(x, y0, y1, gate, row0):
    n, d = x.shape
    tm = _pick_tile(n, 256)
    assert row0 % tm == 0
    blk0 = row0 // tm
    own = lambda i: (i, 0)
    shifted = lambda i: (blk0 + i, 0)
    return pl.pallas_call(
        _combine_kernel,
        grid=(n // tm,),
        in_specs=[pl.BlockSpec((tm, d), own), pl.BlockSpec((tm, d), shifted), pl.BlockSpec((tm, d), shifted),
                  pl.BlockSpec((tm, LANES), own)],
        out_specs=pl.BlockSpec((tm, d), own),
        out_shape=jax.ShapeDtypeStruct((n, d), F32),
        compiler_params=_params("parallel"),
        name="combine",
    )(x, y0, y1, gate)


def _moe(x_p, x_s, ln, w_router, wg, wu, wd):
    n_p, d = x_p.shape
    n = n_p + x_s.shape[0]
    n_exp = w_router.shape[1]
    wr = jnp.pad(w_router.astype(F32), ((0, 0), (0, LANES - n_exp)))
    wr_hi = wr.astype(BF16)
    wr_lo = (wr - wr_hi.astype(F32)).astype(BF16)
    hn_p, ridx_p, rgate_p = _router(x_p, ln, wr_hi, wr_lo, n_exp)
    hn_s, ridx_s, rgate_s = _router(x_s, ln, wr_hi, wr_lo, n_exp)
    hn = jnp.concatenate([hn_p, hn_s], axis=0)
    tm = 1024 if TOP_K * n >= 8192 else 64
    n_asg = TOP_K * n
    n_tiles = -(-(n_asg + n_exp * (tm - 1)) // tm)
    flat_e = jnp.concatenate([ridx_p[:, :TOP_K], ridx_s[:, :TOP_K]], axis=0).reshape(-1)
    onehot = (flat_e[:, None] == jnp.arange(n_exp, dtype=jnp.int32)[None, :]).astype(jnp.int32)
    csum = jnp.cumsum(onehot, axis=0)
    counts = csum[-1]
    rank = jnp.take_along_axis(csum, flat_e[:, None], axis=1)[:, 0] - 1
    padded = ((counts + tm - 1) // tm) * tm
    ends = jnp.cumsum(padded)
    pos = (ends - padded)[flat_e] + rank
    src_tok = jnp.zeros((n_tiles * tm,), jnp.int32).at[pos].set(jnp.arange(n_asg, dtype=jnp.int32) // TOP_K)
    tile_start = jnp.arange(n_tiles, dtype=jnp.int32) * tm
    te = jnp.sum((tile_start[:, None] >= ends[None, :]).astype(jnp.int32), axis=1)
    tile_expert = jnp.minimum(te, n_exp - 1)
    rows_in_tile = (ends - padded + counts)[tile_expert] - tile_start
    tile_active = jnp.where(te >= n_exp, 0, jnp.where(rows_in_tile <= tm // 2, 2, 1)).astype(jnp.int32)
    ys = _experts(hn, src_tok, wg, wu, wd, tile_expert, tile_active, tm)
    pos2 = pos.reshape(n, TOP_K)
    y0 = jnp.take(ys, pos2[:, 0], axis=0, mode="clip")
    y1 = jnp.take(ys, pos2[:, 1], axis=0, mode="clip")
    return _combine(x_p, y0, y1, rgate_p, 0), _combine(x_s, y0, y1, rgate_s, n_p)


def kernel(x_prompt, x_sample, cache_attn_k, cache_attn_v, state_ret, ln_mix, w_in, q_norm, k_norm, attn_sinks,
           ret_gn_g, ret_gn_b, w_out, ln_ffn, ffn_w_gate, ffn_w_up, ffn_w_down, moe_router, moe_w_gate, moe_w_up,
           moe_w_down):
    batch, seq, d = x_prompt.shape
    dbatch, t_s, _ = x_sample.shape
    depth = ln_mix.shape[0]
    n_kv = cache_attn_k.shape[3]
    n_heads = state_ret.shape[2]
    d_kv = n_kv * HEAD_DIM
    d_ret = n_heads * RET_HEAD_DIM
    d_attn = w_out.shape[1] - d_ret
    n_q = d_attn // HEAD_DIM
    assert cache_attn_k.shape[4] == HEAD_DIM and state_ret.shape[3] == RET_HEAD_DIM and n_kv % 2 == 0
    assert seq % CHUNK == 0 and WINDOW % CHUNK == 0 and seq >= WINDOW and cache_attn_k.shape[2] == WINDOW
    assert w_in.shape[2] == d_attn + 2 * d_kv + 4 * d_ret
    n_p, n_s = batch * seq, dbatch * t_s
    keep = min(WINDOW, seq)

    perm_q, perm_k, perm_ao, sink_heads = _layout_perms(n_q, n_kv)
    inv_k = np.argsort(perm_k)
    qk_perm = np.concatenate([perm_q, d_attn + perm_k])

    tabs_p = _rope_tables(jnp.arange(seq))
    tabs_s = _rope_tables(PAST_LEN + jnp.tile(jnp.arange(t_s), dbatch))
    lane = np.arange(LANES)
    m128 = jnp.asarray(((lane[:, None] // HALF) % 2) == ((lane[None, :] // HALF) % 2), BF16)
    head_gain = lambda g: jnp.tile(g.astype(F32).reshape(2, HALF), (1, 2)).reshape(1, LANES)

    x_p, x_s = x_prompt.reshape(n_p, d), x_sample.reshape(n_s, d)
    nk_p, nv_p, nr_p, nk_s, nv_s, nr_s = [], [], [], [], [], []
    for l in range(depth):
        w_in_l = jnp.concatenate([w_in[l][:, :d_attn + d_kv][:, qk_perm], w_in[l][:, d_attn + d_kv:]], axis=1)
        w_in_l = w_in_l.astype(BF16).reshape(d, -1, COL_TILE).transpose(1, 0, 2)
        w_out_l = jnp.concatenate([w_out[l][:d_attn][perm_ao], w_out[l][d_attn:]], axis=0).astype(BF16)
        ln, gq, gk = ln_mix[l][None, :], head_gain(q_norm[l]), head_gain(k_norm[l])
        gng, gnb = ret_gn_g[l][None, :], ret_gn_b[l][None, :]
        proj_p = _proj(x_p, ln, w_in_l, m128, gq, gk, *tabs_p, d_attn, d_kv, d_ret, 512)
        proj_s = _proj(x_s, ln, w_in_l, m128, gq, gk, *tabs_s, d_attn, d_kv, d_ret, 256)
        merged_p, r_p = _mixer_prompt(proj_p, attn_sinks[l], sink_heads, gng, gnb, batch, seq, n_kv, n_q, n_heads)
        ck = cache_attn_k[l].reshape(dbatch, WINDOW, d_kv)[:, :, perm_k].astype(BF16)
        cv = cache_attn_v[l].reshape(dbatch, WINDOW, d_kv).astype(BF16)
        merged_s, r_s = _mixer_sample(proj_s, ck, cv, state_ret[l], attn_sinks[l], sink_heads, gng, gnb,
                                      0, dbatch, t_s, n_kv, n_q, n_heads)
        x_p = _outproj(merged_p, w_out_l, x_p)
        x_s = _outproj(merged_s, w_out_l, x_s)

        ka_last = proj_p[1].reshape(batch, seq, d_kv)[:, seq - keep:]
        nk_p.append(ka_last[:, :, inv_k].reshape(batch, keep, n_kv, HEAD_DIM))
        nv_p.append(proj_p[2].reshape(batch, seq, d_kv)[:, seq - keep:].reshape(batch, keep, n_kv, HEAD_DIM))
        nk_s.append(proj_s[1][:, inv_k].reshape(dbatch, t_s, n_kv, HEAD_DIM))
        nv_s.append(proj_s[2].reshape(dbatch, t_s, n_kv, HEAD_DIM))
        nr_p.append(r_p)
        nr_s.append(r_s)

        i = l // 2
        lf = ln_ffn[l][None, :]
        if l % 2 == 0:
            fpad = (-ffn_w_gate.shape[2]) % COL_TILE
            wg = jnp.pad(ffn_w_gate[i].astype(BF16), ((0, 0), (0, fpad)))
            wu = jnp.pad(ffn_w_up[i].astype(BF16), ((0, 0), (0, fpad)))
            wd = jnp.pad(ffn_w_down[i].astype(BF16), ((0, fpad), (0, 0)))
            x_p, x_s = _ffn(x_p, lf, wg, wu, wd), _ffn(x_s, lf, wg, wu, wd)
        else:
            x_p, x_s = _moe(x_p, x_s, lf, moe_router[i], moe_w_gate[i], moe_w_up[i], moe_w_down[i])

    return (x_p.reshape(batch, seq, d), x_s.reshape(dbatch, t_s, d),
            jnp.stack(nk_p), jnp.stack(nv_p), jnp.stack(nr_p), jnp.stack(nk_s), jnp.stack(nv_s), jnp.stack(nr_s))
```

```python
import functools

import numpy as np
import jax
import jax.numpy as jnp
from jax import lax
from jax.experimental import pallas as pl
from jax.experimental.pallas import tpu as pltpu

CHUNK = 64
WINDOW = 128
HEAD_DIM = 64
RET_HEAD_DIM = 128
PAST_LEN = 2048
ROPE_THETA = 10000.0
TOP_K = 2
RMS_EPS = 1e-6
GN_EPS = 1e-5

LANES = 128
HALF = HEAD_DIM // 2
VMEM_LIMIT = 56 * 1024 * 1024
COL_TILE = 512
EXPERT_WEIGHT_STREAMS = 1
F32 = jnp.float32
BF16 = jnp.bfloat16


def _pick_tile(n, target, mult=16):
    best = None
    for t in range(mult, min(n, target) + 1, mult):
        if n % t == 0:
            best = t
    assert best is not None, (n, target)
    return best


def _params(*sem):
    return pltpu.CompilerParams(dimension_semantics=sem, vmem_limit_bytes=VMEM_LIMIT)


def _layout_perms(n_q, n_kv):
    g_sz = n_q // n_kv
    n_pairs = n_kv // 2
    perm_k = np.zeros(n_kv * HEAD_DIM, np.int32)
    perm_q = np.zeros(n_q * HEAD_DIM, np.int32)
    perm_ao = np.zeros(n_q * HEAD_DIM, np.int32)
    sink_heads = np.zeros((n_pairs, 2 * g_sz), np.int32)
    for p in range(n_pairs):
        for hf in range(2):
            for s in range(2):
                for r in range(HALF):
                    perm_k[p * LANES + hf * 64 + s * HALF + r] = (2 * p + s) * HEAD_DIM + hf * HALF + r
        for g in range(g_sz):
            bq = p * g_sz + g
            for s in range(2):
                head = g_sz * (2 * p + s) + g
                sink_heads[p, 2 * g + s] = head
                for hf in range(2):
                    for r in range(HALF):
                        perm_q[bq * LANES + hf * 64 + s * HALF + r] = head * HEAD_DIM + hf * HALF + r
                for d in range(HEAD_DIM):
                    perm_ao[bq * LANES + s * HEAD_DIM + d] = head * HEAD_DIM + d
    return perm_q, perm_k, perm_ao, sink_heads


def _rope_tables(pos):
    pf = pos.astype(F32)[:, None]
    inv_a = ROPE_THETA ** (-jnp.arange(0, HEAD_DIM, 2, dtype=F32) / HEAD_DIM)
    ang = pf * inv_a[None, :]
    c, s = jnp.cos(ang), jnp.sin(ang)
    ca = jnp.concatenate([c, c, c, c], axis=-1)
    sa = jnp.concatenate([-s, -s, s, s], axis=-1)
    inv_r = 10000.0 ** (-jnp.linspace(0.0, 1.0, RET_HEAD_DIM // 2, dtype=F32))
    ang = pf * inv_r[None, :]
    c, s = jnp.cos(ang), jnp.sin(ang)
    cr = jnp.concatenate([c, c], axis=-1)
    sr = jnp.concatenate([-s, s], axis=-1)
    return ca, sa, cr, sr


def _decay_tables(n_heads, chunk):
    log_g = jnp.log1p(-jnp.exp2(-5.0 - jnp.arange(n_heads, dtype=F32)))
    idx = jnp.arange(chunk, dtype=F32)
    dmat = jnp.exp(log_g[:, None, None] * jnp.abs(idx[:, None] - idx[None, :]))
    dec_in = jnp.exp(log_g[None, :] * (idx[:, None] + 1.0))
    dec_kv = jnp.exp(log_g[None, :] * (chunk - 1.0 - idx[:, None]))
    dec_chunk = jnp.exp(log_g * chunk)
    bc = lambda t: jnp.broadcast_to(t.T[:, :, None], (n_heads, chunk, LANES))
    return dmat, bc(dec_in), bc(dec_kv), dec_chunk


def _rope128(y, c, s):
    return y * c + pltpu.roll(y, 64, 1) * s


def _proj_kernel(nq, nr, x_ref, ln_ref, w_ref, m_ref, gq_ref, gk_ref, ca_ref, sa_ref, cr_ref, sr_ref,
                 qa_ref, ka_ref, va_ref, qr_ref, kr_ref, vr_ref, gate_ref, xn_ref):
    x = x_ref[...]
    ms = jnp.mean(x * x, axis=-1, keepdims=True)
    xn_ref[...] = (x * lax.rsqrt(ms + RMS_EPS) * ln_ref[...]).astype(BF16)
    n_sub = COL_TILE // LANES

    def col_tile(j):
        return jnp.dot(xn_ref[...], w_ref[j], preferred_element_type=F32)

    def lanes(j, c):
        return slice(j * COL_TILE + c * LANES, j * COL_TILE + (c + 1) * LANES)

    def head_norm_rope(yc, g):
        y2 = yc * yc
        hi = y2.astype(BF16)
        lo = (y2 - hi.astype(F32)).astype(BF16)
        ss = (jnp.dot(hi, m_ref[...], preferred_element_type=F32)
              + jnp.dot(lo, m_ref[...], preferred_element_type=F32))
        yn = yc * lax.rsqrt(ss * (1.0 / HEAD_DIM) + RMS_EPS) * g
        return _rope128(yn, ca_ref[...], sa_ref[...])

    for j in range(nq):
        y = col_tile(j)
        for c in range(n_sub):
            qa_ref[:, lanes(j, c)] = (head_norm_rope(y[:, lanes(0, c)], gq_ref[...])
                                      * (HEAD_DIM ** -0.5)).astype(BF16)

    y = col_tile(nq)
    half = COL_TILE // 2
    for c in range(half // LANES):
        ka_ref[:, lanes(0, c)] = head_norm_rope(y[:, lanes(0, c)], gk_ref[...])
    va_ref[...] = y[:, half:]

    for j in range(nr):
        y = col_tile(nq + 1 + j)
        for c in range(n_sub):
            qr_ref[:, lanes(j, c)] = _rope128(y[:, lanes(0, c)], cr_ref[...], sr_ref[...]).astype(BF16)
    for j in range(nr):
        y = col_tile(nq + 1 + nr + j)
        for c in range(n_sub):
            kr_ref[:, lanes(j, c)] = (_rope128(y[:, lanes(0, c)], cr_ref[...], sr_ref[...])
                                      * (RET_HEAD_DIM ** -0.5))
    for j in range(nr):
        vr_ref[:, j * COL_TILE:(j + 1) * COL_TILE] = col_tile(nq + 1 + 2 * nr + j).astype(BF16)
    for j in range(nr):
        gate_ref[:, j * COL_TILE:(j + 1) * COL_TILE] = col_tile(nq + 1 + 3 * nr + j)


def _proj(x, ln, w, m128, gq, gk, ca, sa, cr, sr, d_attn, d_kv, d_ret, tm_target):
    n, d = x.shape
    assert d_attn % COL_TILE == 0 and 2 * d_kv == COL_TILE and d_ret % COL_TILE == 0
    nq, nr = d_attn // COL_TILE, d_ret // COL_TILE
    n_col = nq + 1 + 4 * nr
    tab_rows = ca.shape[0]
    assert n % tab_rows == 0
    tm = _pick_tile(tab_rows, tm_target)
    row = lambda i: (i, 0)
    tab = pl.BlockSpec((tm, LANES), lambda i: (i % (tab_rows // tm), 0))
    vec = pl.BlockSpec((1, LANES), lambda i: (0, 0))
    return pl.pallas_call(
        functools.partial(_proj_kernel, nq, nr),
        grid=(n // tm,),
        in_specs=[
            pl.BlockSpec((tm, d), row),
            pl.BlockSpec((1, d), lambda i: (0, 0)),
            pl.BlockSpec((n_col, d, COL_TILE), lambda i: (0, 0, 0), pipeline_mode=pl.Buffered(1)),
            pl.BlockSpec((LANES, LANES), lambda i: (0, 0)),
            vec, vec, tab, tab, tab, tab,
        ],
        out_specs=[
            pl.BlockSpec((tm, d_attn), row),
            pl.BlockSpec((tm, d_kv), row),
            pl.BlockSpec((tm, d_kv), row),
            pl.BlockSpec((tm, d_ret), row),
            pl.BlockSpec((tm, d_ret), row),
            pl.BlockSpec((tm, d_ret), row),
            pl.BlockSpec((tm, d_ret), row),
        ],
        out_shape=[
            jax.ShapeDtypeStruct((n, d_attn), BF16),
            jax.ShapeDtypeStruct((n, d_kv), F32),
            jax.ShapeDtypeStruct((n, d_kv), F32),
            jax.ShapeDtypeStruct((n, d_ret), BF16),
            jax.ShapeDtypeStruct((n, d_ret), F32),
            jax.ShapeDtypeStruct((n, d_ret), BF16),
            jax.ShapeDtypeStruct((n, d_ret), F32),
        ],
        scratch_shapes=[pltpu.VMEM((tm, d), BF16)],
        compiler_params=_params("parallel"),
        name="proj",
    )(x, ln, w, m128, gq, gk, ca, sa, cr, sr)


def _attend(items, valid, sink_ref, t, n_pairs, g_sz):
    lane = lax.broadcasted_iota(jnp.int32, (1, LANES), 1)
    slot_a = ((lane // HALF) % 2) == 0
    first_half = lane < HEAD_DIM
    zero = jnp.zeros((), BF16)
    jobs = [(i, p) for i in range(len(items)) for p in range(n_pairs)]
    scores = []
    for i, p in jobs:
        qa, k_all, _ = items[i]
        parts = []
        for g in range(g_sz):
            qb = qa[:, (p * g_sz + g) * LANES:(p * g_sz + g + 1) * LANES]
            parts.append(jnp.where(slot_a, qb, zero))
            parts.append(jnp.where(slot_a, zero, qb))
        qs = jnp.concatenate(parts, axis=0)
        scores.append(lax.dot_general(qs, k_all[:, p * LANES:(p + 1) * LANES], (((1,), (1,)), ((), ())),
                                      preferred_element_type=F32))
    probs = []
    for (i, p), s in zip(jobs, scores):
        if valid is not None:
            s = jnp.where(valid, s, -jnp.inf)
        sk = sink_ref[p][:, 0:1]
        m = jnp.maximum(jnp.max(s, axis=-1, keepdims=True), sk)
        e = jnp.exp(s - m)
        den = jnp.sum(e, axis=-1, keepdims=True) + jnp.exp(sk - m)
        probs.append((e / den).astype(BF16))
    outs = [[] for _ in items]
    for (i, p), pn in zip(jobs, probs):
        o = jnp.dot(pn, items[i][2][:, p * LANES:(p + 1) * LANES], preferred_element_type=F32)
        for g in range(g_sz):
            o_a = o[(2 * g) * t:(2 * g + 1) * t]
            o_b = o[(2 * g + 1) * t:(2 * g + 2) * t]
            outs[i].append(jnp.where(first_half, o_a, o_b))
    return outs


def _retain(items, dmat_ref, decin_ref, deckv_ref, dchunk_ref, gng, gnb, n_heads):
    nt = (((1,), (1,)), ((), ()))
    tn = (((0,), (0,)), ((), ()))
    jobs = [(i, h) for i in range(len(items)) for h in range(n_heads)]
    sl = lambda h: slice(h * RET_HEAD_DIM, (h + 1) * RET_HEAD_DIM)
    intra, inter, r_prevs = [], [], []
    for i, h in jobs:
        qr, kr, _, _, r_in_ref, _ = items[i]
        q = qr[:, sl(h)]
        r_prev = r_in_ref[h]
        r_prevs.append(r_prev)
        intra.append(lax.dot_general(q, kr[:, sl(h)].astype(BF16), nt, preferred_element_type=F32) * dmat_ref[h])
        inter.append(jnp.dot(q, r_prev.astype(BF16), preferred_element_type=F32) * decin_ref[h])
    for (i, h), r_prev in zip(jobs, r_prevs):
        _, kr, vr, _, _, r_out_ref = items[i]
        kd = (kr[:, sl(h)] * deckv_ref[h]).astype(BF16)
        kv = lax.dot_general(kd, vr[:, sl(h)], tn, preferred_element_type=F32)
        r_out_ref[h] = r_prev * dchunk_ref[h] + kv
    outs = [[] for _ in items]
    for (i, h), s, o_inter in zip(jobs, intra, inter):
        _, _, vr, gate, _, _ = items[i]
        o = jnp.dot(s.astype(BF16), vr[:, sl(h)], preferred_element_type=F32) + o_inter
        mu = jnp.mean(o, axis=-1, keepdims=True)
        dev = o - mu
        var = jnp.mean(dev * dev, axis=-1, keepdims=True)
        r = dev * lax.rsqrt(var + GN_EPS) * gng[:, sl(h)] + gnb[:, sl(h)]
        gt = gate[:, sl(h)]
        outs[i].append(gt * jax.nn.sigmoid(gt) * r)
    return outs


def _store_merged(merged_ref, ao, ro, d_attn):
    for b, blk in enumerate(ao):
        merged_ref[:, b * LANES:(b + 1) * LANES] = blk.astype(BF16)
    for h, blk in enumerate(ro):
        merged_ref[:, d_attn + h * RET_HEAD_DIM:d_attn + (h + 1) * RET_HEAD_DIM] = blk.astype(BF16)


def _mixer_prompt_kernel(nc, t, batch, n_pairs, g_sz, n_heads, d_attn,
                         dchunk_ref, qa_ref, *rest):
    n_win = WINDOW // t + 1
    k_refs, v_refs = rest[:n_win], rest[n_win:2 * n_win]
    (qr_ref, kr_ref, vr_ref, gate_ref, sink_ref, dmat_ref, decin_ref, deckv_ref, gng_ref, gnb_ref,
     merged_ref, rout_ref, r_ref) = rest[2 * n_win:]
    c = pl.program_id(0)

    @pl.when(c == 0)
    def _reset():
        r_ref[...] = jnp.zeros_like(r_ref)

    kidx = lax.broadcasted_iota(jnp.int32, (1, n_win * t), 1)
    valid = kidx >= (n_win - 1 - c) * t
    att_items = [(qa_ref[b],
                  jnp.concatenate([r[b].astype(BF16) for r in k_refs], axis=0),
                  jnp.concatenate([r[b].astype(BF16) for r in v_refs], axis=0)) for b in range(batch)]
    ret_items = [(qr_ref[b], kr_ref[b], vr_ref[b], gate_ref[b], r_ref.at[b], r_ref.at[b]) for b in range(batch)]
    ao = _attend(att_items, valid, sink_ref, t, n_pairs, g_sz)
    ro = _retain(ret_items, dmat_ref, decin_ref, deckv_ref, dchunk_ref, gng_ref[...], gnb_ref[...], n_heads)
    for b in range(batch):
        _store_merged(merged_ref.at[b], ao[b], ro[b], d_attn)

    @pl.when(c == nc - 1)
    def _emit_state():
        rout_ref[...] = r_ref[...]


def _mixer_sample_kernel(t, n_pairs, g_sz, n_heads, d_attn,
                         dchunk_ref, qa_ref, ka_ref, va_ref, qr_ref, kr_ref, vr_ref, gate_ref,
                         ck_ref, cv_ref, r0_ref,
                         sink_ref, dmat_ref, decin_ref, deckv_ref, gng_ref, gnb_ref,
                         merged_ref, rout_ref):
    k_all = jnp.concatenate([ck_ref[...], ka_ref[...].astype(BF16)], axis=0)
    v_all = jnp.concatenate([cv_ref[...], va_ref[...].astype(BF16)], axis=0)
    ao = _attend([(qa_ref[...], k_all, v_all)], None, sink_ref, t, n_pairs, g_sz)
    ro = _retain([(qr_ref[...], kr_ref[...], vr_ref[...], gate_ref[...], r0_ref, rout_ref)],
                 dmat_ref, decin_ref, deckv_ref, dchunk_ref, gng_ref[...], gnb_ref[...], n_heads)
    _store_merged(merged_ref, ao[0], ro[0], d_attn)


def _sink_table(sinks, sink_heads, t):
    per_row = jnp.repeat(sinks.astype(F32)[sink_heads], t, axis=1)
    return jnp.broadcast_to(per_row[:, :, None], per_row.shape + (LANES,))


def _mixer_prompt(proj_out, sinks, sink_heads, gng, gnb, batch, seq, n_kv, n_q, n_heads):
    qa, ka, va, qr, kr, vr, gate = proj_out
    t = CHUNK
    nc = seq // t
    d_attn, d_kv, d_ret = n_q * HEAD_DIM, n_kv * HEAD_DIM, n_heads * RET_HEAD_DIM
    n_pairs, g_sz = n_kv // 2, n_q // n_kv
    dmat, decin, deckv, dchunk = _decay_tables(n_heads, t)
    sink_tab = _sink_table(sinks, sink_heads, t)
    seq3 = lambda a: a.reshape(batch, seq, a.shape[-1])
    qa, ka, va, qr, kr, vr, gate = map(seq3, proj_out)
    row = lambda c, *_: (0, c, 0)
    back = lambda k: (lambda c, *_: (0, jnp.maximum(c - k, 0), 0))
    const3 = lambda c, *_: (0, 0, 0)
    const2 = lambda c, *_: (0, 0)
    n_win = WINDOW // t + 1
    kv_specs = [pl.BlockSpec((batch, t, d_kv), back(n_win - 1 - s)) for s in range(n_win)]
    state_shape = (batch, n_heads, RET_HEAD_DIM, RET_HEAD_DIM)
    grid_spec = pltpu.PrefetchScalarGridSpec(
        num_scalar_prefetch=1,
        grid=(nc,),
        in_specs=[
            pl.BlockSpec((batch, t, d_attn), row), *kv_specs, *kv_specs,
            pl.BlockSpec((batch, t, d_ret), row), pl.BlockSpec((batch, t, d_ret), row),
            pl.BlockSpec((batch, t, d_ret), row), pl.BlockSpec((batch, t, d_ret), row),
            pl.BlockSpec(sink_tab.shape, const3), pl.BlockSpec(dmat.shape, const3),
            pl.BlockSpec(decin.shape, const3), pl.BlockSpec(deckv.shape, const3),
            pl.BlockSpec((1, d_ret), const2), pl.BlockSpec((1, d_ret), const2),
        ],
        out_specs=[
            pl.BlockSpec((batch, t, d_attn + d_ret), row),
            pl.BlockSpec(state_shape, lambda c, *_: (0, 0, 0, 0)),
        ],
        scratch_shapes=[pltpu.VMEM(state_shape, F32)],
    )
    merged, r_fin = pl.pallas_call(
        functools.partial(_mixer_prompt_kernel, nc, t, batch, n_pairs, g_sz, n_heads, d_attn),
        grid_spec=grid_spec,
        out_shape=[jax.ShapeDtypeStruct((batch, seq, d_attn + d_ret), BF16),
                   jax.ShapeDtypeStruct(state_shape, F32)],
        compiler_params=_params("arbitrary"),
        name="mixer_prompt",
    )(dchunk, qa, *([ka] * n_win), *([va] * n_win), qr, kr, vr, gate, sink_tab, dmat, decin, deckv, gng, gnb)
    return merged.reshape(batch * seq, d_attn + d_ret), r_fin


def _mixer_sample(proj_out, ck, cv, r0, sinks, sink_heads, gng, gnb, row0, dbatch, t, n_kv, n_q, n_heads):
    qa, ka, va, qr, kr, vr, gate = proj_out
    d_attn, d_kv, d_ret = n_q * HEAD_DIM, n_kv * HEAD_DIM, n_heads * RET_HEAD_DIM
    n_pairs, g_sz = n_kv // 2, n_q // n_kv
    dmat, decin, deckv, dchunk = _decay_tables(n_heads, t)
    sink_tab = _sink_table(sinks, sink_heads, t)
    blk0 = row0 // t
    row = lambda b, *_: (blk0 + b, 0)
    const3 = lambda b, *_: (0, 0, 0)
    const2 = lambda b, *_: (0, 0)
    state_spec = pl.BlockSpec((None, n_heads, RET_HEAD_DIM, RET_HEAD_DIM), lambda b, *_: (b, 0, 0, 0))
    n_hist = ck.shape[1]
    grid_spec = pltpu.PrefetchScalarGridSpec(
        num_scalar_prefetch=1,
        grid=(dbatch,),
        in_specs=[
            pl.BlockSpec((t, d_attn), row), pl.BlockSpec((t, d_kv), row), pl.BlockSpec((t, d_kv), row),
            pl.BlockSpec((t, d_ret), row), pl.BlockSpec((t, d_ret), row), pl.BlockSpec((t, d_ret), row),
            pl.BlockSpec((t, d_ret), row),
            pl.BlockSpec((None, n_hist, d_kv), lambda b, *_: (b, 0, 0)),
            pl.BlockSpec((None, n_hist, d_kv), lambda b, *_: (b, 0, 0)),
            state_spec,
            pl.BlockSpec(sink_tab.shape, const3), pl.BlockSpec(dmat.shape, const3),
            pl.BlockSpec(decin.shape, const3), pl.BlockSpec(deckv.shape, const3),
            pl.BlockSpec((1, d_ret), const2), pl.BlockSpec((1, d_ret), const2),
        ],
        out_specs=[pl.BlockSpec((t, d_attn + d_ret), lambda b, *_: (b, 0)), state_spec],
    )
    return pl.pallas_call(
        functools.partial(_mixer_sample_kernel, t, n_pairs, g_sz, n_heads, d_attn),
        grid_spec=grid_spec,
        out_shape=[jax.ShapeDtypeStruct((dbatch * t, d_attn + d_ret), BF16),
                   jax.ShapeDtypeStruct(r0.shape, F32)],
        compiler_params=_params("parallel"),
        name="mixer_sample",
    )(dchunk, qa, ka, va, qr, kr, vr, gate, ck, cv, r0, sink_tab, dmat, decin, deckv, gng, gnb)


def _outproj_kernel(m_ref, w_ref, x_ref, o_ref):
    o_ref[...] = x_ref[...] + jnp.dot(m_ref[...], w_ref[...], preferred_element_type=F32)


def _outproj(merged, w, x):
    n, d = x.shape
    k = merged.shape[1]
    tm = _pick_tile(n, 1024)
    tn = _pick_tile(d, 1024, LANES)
    return pl.pallas_call(
        _outproj_kernel,
        grid=(n // tm, d // tn),
        in_specs=[pl.BlockSpec((tm, k), lambda i, j: (i, 0)),
                  pl.BlockSpec((k, tn), lambda i, j: (0, j)),
                  pl.BlockSpec((tm, tn), lambda i, j: (i, j))],
        out_specs=pl.BlockSpec((tm, tn), lambda i, j: (i, j)),
        out_shape=jax.ShapeDtypeStruct((n, d), F32),
        compiler_params=_params("parallel", "arbitrary"),
        name="outproj",
    )(merged, w, x)


def _swiglu_step(xn, wg, wu, wd):
    g = jnp.dot(xn, wg, preferred_element_type=F32)
    u = jnp.dot(xn, wu, preferred_element_type=F32)
    h = (g * jax.nn.sigmoid(g) * u).astype(BF16)
    return jnp.dot(h, wd, preferred_element_type=F32)


def _ffn_kernel(x_ref, ln_ref, wg_ref, wu_ref, wd_ref, o_ref, xn_ref):
    f = pl.program_id(1)

    @pl.when(f == 0)
    def _norm():
        x = x_ref[...]
        ms = jnp.mean(x * x, axis=-1, keepdims=True)
        xn_ref[...] = (x * lax.rsqrt(ms + RMS_EPS) * ln_ref[...]).astype(BF16)
        o_ref[...] = x

    o_ref[...] += _swiglu_step(xn_ref[...], wg_ref[...], wu_ref[...], wd_ref[...])


def _ffn(x, ln, wg, wu, wd):
    n, d = x.shape
    fp = wg.shape[1]
    tf = _pick_tile(fp, 512, LANES)
    tm = _pick_tile(n, 1024)
    return pl.pallas_call(
        _ffn_kernel,
        grid=(n // tm, fp // tf),
        in_specs=[pl.BlockSpec((tm, d), lambda i, f: (i, 0)),
                  pl.BlockSpec((1, d), lambda i, f: (0, 0)),
                  pl.BlockSpec((d, tf), lambda i, f: (0, f)),
                  pl.BlockSpec((d, tf), lambda i, f: (0, f)),
                  pl.BlockSpec((tf, d), lambda i, f: (f, 0))],
        out_specs=pl.BlockSpec((tm, d), lambda i, f: (i, 0)),
        out_shape=jax.ShapeDtypeStruct((n, d), F32),
        scratch_shapes=[pltpu.VMEM((tm, d), BF16)],
        compiler_params=_params("parallel", "arbitrary"),
        name="ffn",
    )(x, ln, wg, wu, wd)


def _router_kernel(n_exp, x_ref, ln_ref, wr_hi_ref, wr_lo_ref, hn_ref, idx_ref, gate_ref):
    x = x_ref[...]
    ms = jnp.mean(x * x, axis=-1, keepdims=True)
    hf = x * lax.rsqrt(ms + RMS_EPS) * ln_ref[...]
    hn = hf.astype(BF16)
    bits = lax.bitcast_convert_type(hn.astype(F32), jnp.uint32)
    half = bits.shape[1] // 2
    hn_ref[...] = (bits[:, :half] >> 16) | (bits[:, half:] & jnp.uint32(0xFFFF0000))
    h_lo = (hf - hn.astype(F32)).astype(BF16)
    logits = (jnp.dot(hn, wr_hi_ref[...], preferred_element_type=F32)
              + jnp.dot(h_lo, wr_hi_ref[...], preferred_element_type=F32)
              + jnp.dot(hn, wr_lo_ref[...], preferred_element_type=F32))
    lane = lax.broadcasted_iota(jnp.int32, logits.shape, 1)
    logits = jnp.where(lane < n_exp, logits, -jnp.inf)
    m1 = jnp.max(logits, axis=-1, keepdims=True)
    i1 = jnp.min(jnp.where(logits == m1, lane, LANES), axis=-1, keepdims=True)
    rest = jnp.where(lane == i1, -jnp.inf, logits)
    m2 = jnp.max(rest, axis=-1, keepdims=True)
    i2 = jnp.min(jnp.where(rest == m2, lane, LANES), axis=-1, keepdims=True)
    e2 = jnp.exp(m2 - m1)
    den = 1.0 + e2
    idx_ref[...] = jnp.where(lane == 0, i1, jnp.where(lane == 1, i2, 0))
    gate_ref[...] = jnp.where(lane == 0, 1.0 / den, jnp.where(lane == 1, e2 / den, 0.0))


def _router(x, ln, wr_hi, wr_lo, n_exp):
    n, d = x.shape
    tm = _pick_tile(n, 512)
    row = lambda i: (i, 0)
    return pl.pallas_call(
        functools.partial(_router_kernel, n_exp),
        grid=(n // tm,),
        in_specs=[pl.BlockSpec((tm, d), row), pl.BlockSpec((1, d), lambda i: (0, 0)),
                  pl.BlockSpec((d, LANES), lambda i: (0, 0)), pl.BlockSpec((d, LANES), lambda i: (0, 0))],
        out_specs=[pl.BlockSpec((tm, d // 2), row), pl.BlockSpec((tm, LANES), row), pl.BlockSpec((tm, LANES), row)],
        out_shape=[jax.ShapeDtypeStruct((n, d // 2), jnp.uint32), jax.ShapeDtypeStruct((n, LANES), jnp.int32),
                   jax.ShapeDtypeStruct((n, LANES), F32)],
        compiler_params=_params("parallel"),
        name="router",
    )(x, ln, wr_hi, wr_lo)


def _experts_kernel(tm, n_split, te_ref, act_ref, tok_ref, hn_hbm, *rest):
    wg_refs, wu_refs, wd_refs = rest[:n_split], rest[n_split:2 * n_split], rest[2 * n_split:3 * n_split]
    o_ref, rows_ref, x_ref, wgb_ref, wub_ref, wdb_ref, sem = rest[3 * n_split:]
    i = pl.program_id(0)
    f = pl.program_id(1)
    active = act_ref[i] > 0

    @pl.when(f == 0)
    def _init():
        o_ref[...] = jnp.zeros_like(o_ref)

    @pl.when(active & (f == 0))
    def _gather():
        def start_row(r, carry):
            pltpu.make_async_copy(hn_hbm.at[pl.ds(tok_ref[0, r], 1)], rows_ref.at[pl.ds(r, 1)], sem).start()
            return carry

        lax.fori_loop(0, tm, start_row, 0, unroll=8)
        pltpu.make_async_copy(hn_hbm.at[pl.ds(0, tm)], rows_ref, sem).wait()
        words = rows_ref[...]
        half = words.shape[1]
        x_ref[:, :half] = lax.bitcast_convert_type(words << 16, F32).astype(BF16)
        x_ref[:, half:] = lax.bitcast_convert_type(words & jnp.uint32(0xFFFF0000), F32).astype(BF16)

    def round_weights():
        for slabs, dst in ((wg_refs, wgb_ref), (wu_refs, wub_ref), (wd_refs, wdb_ref)):
            rows = dst.shape[0] // n_split
            for k, slab in enumerate(slabs):
                dst[k * rows:(k + 1) * rows, :] = slab[...].astype(BF16)

    @pl.when(act_ref[i] == 1)
    def _step():
        round_weights()
        o_ref[...] += _swiglu_step(x_ref[...], wgb_ref[...], wub_ref[...], wdb_ref[...])

    @pl.when(act_ref[i] == 2)
    def _half_step():
        round_weights()
        o_ref[:tm // 2, :] += _swiglu_step(x_ref[:tm // 2, :], wgb_ref[...], wub_ref[...], wdb_ref[...])


def _experts(hn32, src_tok, wg, wu, wd, tile_expert, tile_active, tm):
    n, half = hn32.shape
    d = 2 * half
    assert n >= tm
    n_tiles = src_tok.shape[0] // tm
    n_exp, _, fe = wg.shape
    tf = _pick_tile(fe, 512, LANES)
    nf = fe // tf
    fidx = lambda i, f, te, act: jnp.where(act[i] > 0, f, nf - 1)
    n_split = EXPERT_WEIGHT_STREAMS
    in_slab = lambda k: (lambda i, f, te, act: (te[i], k, fidx(i, f, te, act)))
    out_slab = lambda k: (lambda i, f, te, act: (te[i], n_split * fidx(i, f, te, act) + k, 0))
    in_specs = [pl.BlockSpec((None, d // n_split, tf), in_slab(k)) for k in range(n_split)]
    out_specs_w = [pl.BlockSpec((None, tf // n_split, d), out_slab(k)) for k in range(n_split)]
    grid_spec = pltpu.PrefetchScalarGridSpec(
        num_scalar_prefetch=2,
        grid=(n_tiles, nf),
        in_specs=[pl.BlockSpec((None, 1, tm), lambda i, f, te, act: (i, 0, 0), memory_space=pltpu.SMEM),
                  pl.BlockSpec(memory_space=pl.ANY),
                  *in_specs, *in_specs, *out_specs_w],
        out_specs=pl.BlockSpec((tm, d), lambda i, f, te, act: (i, 0), pipeline_mode=pl.Buffered(1)),
        scratch_shapes=[pltpu.VMEM((tm, half), jnp.uint32), pltpu.VMEM((tm, d), BF16),
                        pltpu.VMEM((d, tf), BF16), pltpu.VMEM((d, tf), BF16), pltpu.VMEM((tf, d), BF16),
                        pltpu.SemaphoreType.DMA],
    )
    return pl.pallas_call(
        functools.partial(_experts_kernel, tm, n_split),
        grid_spec=grid_spec,
        out_shape=jax.ShapeDtypeStruct((n_tiles * tm, d), F32),
        compiler_params=_params("arbitrary", "arbitrary"),
        name="experts",
    )(tile_expert, tile_active, src_tok.reshape(n_tiles, 1, tm), hn32,
      *([wg] * n_split), *([wu] * n_split), *([wd] * n_split))


def _combine_kernel(x_ref, y0_ref, y1_ref, g_ref, o_ref):
    g = g_ref[...]
    o_ref[...] = x_ref[...] + (g[:, 0:1] * y0_ref[...] + g[:, 1:2] * y1_ref[...])


def _combine(x, y0, y1, gate, row0):
    n, d = x.shape
    tm = _pick_tile(n, 256)
    assert row0 % tm == 0
    blk0 = row0 // tm
    own = lambda i: (i, 0)
    shifted = lambda i: (blk0 + i, 0)
    return pl.pallas_call(
        _combine_kernel,
        grid=(n // tm,),
        in_specs=[pl.BlockSpec((tm, d), own), pl.BlockSpec((tm, d), shifted), pl.BlockSpec((tm, d), shifted),
                  pl.BlockSpec((tm, LANES), own)],
        out_specs=pl.BlockSpec((tm, d), own),
        out_shape=jax.ShapeDtypeStruct((n, d), F32),
        compiler_params=_params("parallel"),
        name="combine",
    )(x, y0, y1, gate)


def _moe(x_p, x_s, ln, w_router, wg, wu, wd):
    n_p, d = x_p.shape
    n = n_p + x_s.shape[0]
    n_exp = w_router.shape[1]
    wr = jnp.pad(w_router.astype(F32), ((0, 0), (0, LANES - n_exp)))
    wr_hi = wr.astype(BF16)
    wr_lo = (wr - wr_hi.astype(F32)).astype(BF16)
    hn_p, ridx_p, rgate_p = _router(x_p, ln, wr_hi, wr_lo, n_exp)
    hn_s, ridx_s, rgate_s = _router(x_s, ln, wr_hi, wr_lo, n_exp)
    hn = jnp.concatenate([hn_p, hn_s], axis=0)
    tm = 1024 if TOP_K * n >= 8192 else 64
    n_asg = TOP_K * n
    n_tiles = -(-(n_asg + n_exp * (tm - 1)) // tm)
    flat_e = jnp.concatenate([ridx_p[:, :TOP_K], ridx_s[:, :TOP_K]], axis=0).reshape(-1)
    onehot = (flat_e[:, None] == jnp.arange(n_exp, dtype=jnp.int32)[None, :]).astype(jnp.int32)
    csum = jnp.cumsum(onehot, axis=0)
    counts = csum[-1]
    rank = jnp.take_along_axis(csum, flat_e[:, None], axis=1)[:, 0] - 1
    padded = ((counts + tm - 1) // tm) * tm
    ends = jnp.cumsum(padded)
    pos = (ends - padded)[flat_e] + rank
    src_tok = jnp.zeros((n_tiles * tm,), jnp.int32).at[pos].set(
        jnp.arange(n_asg, dtype=jnp.int32) // TOP_K, unique_indices=True, mode="promise_in_bounds")
    tile_start = jnp.arange(n_tiles, dtype=jnp.int32) * tm
    te = jnp.sum((tile_start[:, None] >= ends[None, :]).astype(jnp.int32), axis=1)
    tile_expert = jnp.minimum(te, n_exp - 1)
    rows_in_tile = (ends - padded + counts)[tile_expert] - tile_start
    tile_active = jnp.where(te >= n_exp, 0, jnp.where(rows_in_tile <= tm // 2, 2, 1)).astype(jnp.int32)
    ys = _experts(hn, src_tok, wg, wu, wd, tile_expert, tile_active, tm)
    pos2 = pos.reshape(n, TOP_K)
    y0 = jnp.take(ys, pos2[:, 0], axis=0, mode="clip")
    y1 = jnp.take(ys, pos2[:, 1], axis=0, mode="clip")
    return _combine(x_p, y0, y1, rgate_p, 0), _combine(x_s, y0, y1, rgate_s, n_p)


def kernel(x_prompt, x_sample, cache_attn_k, cache_attn_v, state_ret, ln_mix, w_in, q_norm, k_norm, attn_sinks,
           ret_gn_g, ret_gn_b, w_out, ln_ffn, ffn_w_gate, ffn_w_up, ffn_w_down, moe_router, moe_w_gate, moe_w_up,
           moe_w_down):
    batch, seq, d = x_prompt.shape
    dbatch, t_s, _ = x_sample.shape
    depth = ln_mix.shape[0]
    n_kv = cache_attn_k.shape[3]
    n_heads = state_ret.shape[2]
    d_kv = n_kv * HEAD_DIM
    d_ret = n_heads * RET_HEAD_DIM
    d_attn = w_out.shape[1] - d_ret
    n_q = d_attn // HEAD_DIM
    assert cache_attn_k.shape[4] == HEAD_DIM and state_ret.shape[3] == RET_HEAD_DIM and n_kv % 2 == 0
    assert seq % CHUNK == 0 and WINDOW % CHUNK == 0 and seq >= WINDOW and cache_attn_k.shape[2] == WINDOW
    assert w_in.shape[2] == d_attn + 2 * d_kv + 4 * d_ret
    n_p, n_s = batch * seq, dbatch * t_s
    keep = min(WINDOW, seq)

    perm_q, perm_k, perm_ao, sink_heads = _layout_perms(n_q, n_kv)
    inv_k = np.argsort(perm_k)
    qk_perm = np.concatenate([perm_q, d_attn + perm_k])

    tabs_p = _rope_tables(jnp.arange(seq))
    tabs_s = _rope_tables(PAST_LEN + jnp.tile(jnp.arange(t_s), dbatch))
    lane = np.arange(LANES)
    m128 = jnp.asarray(((lane[:, None] // HALF) % 2) == ((lane[None, :] // HALF) % 2), BF16)
    head_gain = lambda g: jnp.tile(g.astype(F32).reshape(2, HALF), (1, 2)).reshape(1, LANES)

    x_p, x_s = x_prompt.reshape(n_p, d), x_sample.reshape(n_s, d)
    nk_p, nv_p, nr_p, nk_s, nv_s, nr_s = [], [], [], [], [], []
    for l in range(depth):
        w_in_l = jnp.concatenate([w_in[l][:, :d_attn + d_kv][:, qk_perm], w_in[l][:, d_attn + d_kv:]], axis=1)
        w_in_l = w_in_l.astype(BF16).reshape(d, -1, COL_TILE).transpose(1, 0, 2)
        w_out_l = jnp.concatenate([w_out[l][:d_attn][perm_ao], w_out[l][d_attn:]], axis=0).astype(BF16)
        ln, gq, gk = ln_mix[l][None, :], head_gain(q_norm[l]), head_gain(k_norm[l])
        gng, gnb = ret_gn_g[l][None, :], ret_gn_b[l][None, :]
        proj_p = _proj(x_p, ln, w_in_l, m128, gq, gk, *tabs_p, d_attn, d_kv, d_ret, 512)
        proj_s = _proj(x_s, ln, w_in_l, m128, gq, gk, *tabs_s, d_attn, d_kv, d_ret, 256)
        merged_p, r_p = _mixer_prompt(proj_p, attn_sinks[l], sink_heads, gng, gnb, batch, seq, n_kv, n_q, n_heads)
        ck = cache_attn_k[l].reshape(dbatch, WINDOW, d_kv)[:, :, perm_k].astype(BF16)
        cv = cache_attn_v[l].reshape(dbatch, WINDOW, d_kv).astype(BF16)
        merged_s, r_s = _mixer_sample(proj_s, ck, cv, state_ret[l], attn_sinks[l], sink_heads, gng, gnb,
                                      0, dbatch, t_s, n_kv, n_q, n_heads)
        x_p = _outproj(merged_p, w_out_l, x_p)
        x_s = _outproj(merged_s, w_out_l, x_s)

        ka_last = proj_p[1].reshape(batch, seq, d_kv)[:, seq - keep:]
        nk_p.append(ka_last[:, :, inv_k].reshape(batch, keep, n_kv, HEAD_DIM))
        nv_p.append(proj_p[2].reshape(batch, seq, d_kv)[:, seq - keep:].reshape(batch, keep, n_kv, HEAD_DIM))
        nk_s.append(proj_s[1][:, inv_k].reshape(dbatch, t_s, n_kv, HEAD_DIM))
        nv_s.append(proj_s[2].reshape(dbatch, t_s, n_kv, HEAD_DIM))
        nr_p.append(r_p)
        nr_s.append(r_s)

        i = l // 2
        lf = ln_ffn[l][None, :]
        if l % 2 == 0:
            fpad = (-ffn_w_gate.shape[2]) % COL_TILE
            wg = jnp.pad(ffn_w_gate[i].astype(BF16), ((0, 0), (0, fpad)))
            wu = jnp.pad(ffn_w_up[i].astype(BF16), ((0, 0), (0, fpad)))
            wd = jnp.pad(ffn_w_down[i].astype(BF16), ((0, fpad), (0, 0)))
            x_p, x_s = _ffn(x_p, lf, wg, wu, wd), _ffn(x_s, lf, wg, wu, wd)
        else:
            x_p, x_s = _moe(x_p, x_s, lf, moe_router[i], moe_w_gate[i], moe_w_up[i], moe_w_down[i])

    return (x_p.reshape(batch, seq, d), x_s.reshape(dbatch, t_s, d),
            jnp.stack(nk_p), jnp.stack(nv_p), jnp.stack(nr_p), jnp.stack(nk_s), jnp.stack(nv_s), jnp.stack(nr_s))
```

```python
import functools

import numpy as np
import jax
import jax.numpy as jnp
from jax import lax
from jax.experimental import pallas as pl
from jax.experimental.pallas import tpu as pltpu

CHUNK = 64
WINDOW = 128
HEAD_DIM = 64
RET_HEAD_DIM = 128
PAST_LEN = 2048
ROPE_THETA = 10000.0
TOP_K = 2
RMS_EPS = 1e-6
GN_EPS = 1e-5

LANES = 128
HALF = HEAD_DIM // 2
VMEM_LIMIT = 56 * 1024 * 1024
COL_TILE = 512
EXPERT_WEIGHT_STREAMS = 1
F32 = jnp.float32
BF16 = jnp.bfloat16


def _pick_tile(n, target, mult=16):
    best = None
    for t in range(mult, min(n, target) + 1, mult):
        if n % t == 0:
            best = t
    assert best is not None, (n, target)
    return best


def _params(*sem):
    return pltpu.CompilerParams(dimension_semantics=sem, vmem_limit_bytes=VMEM_LIMIT)


def _layout_perms(n_q, n_kv):
    g_sz = n_q // n_kv
    n_pairs = n_kv // 2
    perm_k = np.zeros(n_kv * HEAD_DIM, np.int32)
    perm_q = np.zeros(n_q * HEAD_DIM, np.int32)
    perm_ao = np.zeros(n_q * HEAD_DIM, np.int32)
    sink_heads = np.zeros((n_pairs, 2 * g_sz), np.int32)
    for p in range(n_pairs):
        for hf in range(2):
            for s in range(2):
                for r in range(HALF):
                    perm_k[p * LANES + hf * 64 + s * HALF + r] = (2 * p + s) * HEAD_DIM + hf * HALF + r
        for g in range(g_sz):
            bq = p * g_sz + g
            for s in range(2):
                head = g_sz * (2 * p + s) + g
                sink_heads[p, 2 * g + s] = head
                for hf in range(2):
                    for r in range(HALF):
                        perm_q[bq * LANES + hf * 64 + s * HALF + r] = head * HEAD_DIM + hf * HALF + r
                for d in range(HEAD_DIM):
                    perm_ao[bq * LANES + s * HEAD_DIM + d] = head * HEAD_DIM + d
    return perm_q, perm_k, perm_ao, sink_heads


def _rope_tables(pos):
    pf = pos.astype(F32)[:, None]
    inv_a = ROPE_THETA ** (-jnp.arange(0, HEAD_DIM, 2, dtype=F32) / HEAD_DIM)
    ang = pf * inv_a[None, :]
    c, s = jnp.cos(ang), jnp.sin(ang)
    ca = jnp.concatenate([c, c, c, c], axis=-1)
    sa = jnp.concatenate([-s, -s, s, s], axis=-1)
    inv_r = 10000.0 ** (-jnp.linspace(0.0, 1.0, RET_HEAD_DIM // 2, dtype=F32))
    ang = pf * inv_r[None, :]
    c, s = jnp.cos(ang), jnp.sin(ang)
    cr = jnp.concatenate([c, c], axis=-1)
    sr = jnp.concatenate([-s, s], axis=-1)
    return ca, sa, cr, sr


def _decay_tables(n_heads, chunk):
    log_g = jnp.log1p(-jnp.exp2(-5.0 - jnp.arange(n_heads, dtype=F32)))
    idx = jnp.arange(chunk, dtype=F32)
    dmat = jnp.exp(log_g[:, None, None] * jnp.abs(idx[:, None] - idx[None, :]))
    dec_in = jnp.exp(log_g[None, :] * (idx[:, None] + 1.0))
    dec_kv = jnp.exp(log_g[None, :] * (chunk - 1.0 - idx[:, None]))
    dec_chunk = jnp.exp(log_g * chunk)
    bc = lambda t: jnp.broadcast_to(t.T[:, :, None], (n_heads, chunk, LANES))
    return dmat, bc(dec_in), bc(dec_kv), dec_chunk


def _rope128(y, c, s):
    return y * c + pltpu.roll(y, 64, 1) * s


def _proj_kernel(nq, nr, x_ref, ln_ref, w_ref, m_ref, gq_ref, gk_ref, ca_ref, sa_ref, cr_ref, sr_ref,
                 qa_ref, ka_ref, va_ref, qr_ref, kr_ref, vr_ref, gate_ref, xn_ref):
    x = x_ref[...]
    ms = jnp.mean(x * x, axis=-1, keepdims=True)
    xn_ref[...] = (x * lax.rsqrt(ms + RMS_EPS) * ln_ref[...]).astype(BF16)
    n_sub = COL_TILE // LANES

    def col_tile(j):
        return jnp.dot(xn_ref[...], w_ref[j], preferred_element_type=F32)

    def lanes(j, c):
        return slice(j * COL_TILE + c * LANES, j * COL_TILE + (c + 1) * LANES)

    def head_norm_rope(yc, g):
        y2 = yc * yc
        hi = y2.astype(BF16)
        lo = (y2 - hi.astype(F32)).astype(BF16)
        ss = (jnp.dot(hi, m_ref[...], preferred_element_type=F32)
              + jnp.dot(lo, m_ref[...], preferred_element_type=F32))
        yn = yc * lax.rsqrt(ss * (1.0 / HEAD_DIM) + RMS_EPS) * g
        return _rope128(yn, ca_ref[...], sa_ref[...])

    for j in range(nq):
        y = col_tile(j)
        for c in range(n_sub):
            qa_ref[:, lanes(j, c)] = (head_norm_rope(y[:, lanes(0, c)], gq_ref[...])
                                      * (HEAD_DIM ** -0.5)).astype(BF16)

    y = col_tile(nq)
    half = COL_TILE // 2
    for c in range(half // LANES):
        ka_ref[:, lanes(0, c)] = head_norm_rope(y[:, lanes(0, c)], gk_ref[...])
    va_ref[...] = y[:, half:]

    for j in range(nr):
        y = col_tile(nq + 1 + j)
        for c in range(n_sub):
            qr_ref[:, lanes(j, c)] = _rope128(y[:, lanes(0, c)], cr_ref[...], sr_ref[...]).astype(BF16)
    for j in range(nr):
        y = col_tile(nq + 1 + nr + j)
        for c in range(n_sub):
            kr_ref[:, lanes(j, c)] = (_rope128(y[:, lanes(0, c)], cr_ref[...], sr_ref[...])
                                      * (RET_HEAD_DIM ** -0.5))
    for j in range(nr):
        vr_ref[:, j * COL_TILE:(j + 1) * COL_TILE] = col_tile(nq + 1 + 2 * nr + j).astype(BF16)
    for j in range(nr):
        gate_ref[:, j * COL_TILE:(j + 1) * COL_TILE] = col_tile(nq + 1 + 3 * nr + j)


def _proj(x, ln, w, m128, gq, gk, ca, sa, cr, sr, d_attn, d_kv, d_ret, tm_target):
    n, d = x.shape
    assert d_attn % COL_TILE == 0 and 2 * d_kv == COL_TILE and d_ret % COL_TILE == 0
    nq, nr = d_attn // COL_TILE, d_ret // COL_TILE
    n_col = nq + 1 + 4 * nr
    tab_rows = ca.shape[0]
    assert n % tab_rows == 0
    tm = _pick_tile(tab_rows, tm_target)
    row = lambda i: (i, 0)
    tab = pl.BlockSpec((tm, LANES), lambda i: (i % (tab_rows // tm), 0))
    vec = pl.BlockSpec((1, LANES), lambda i: (0, 0))
    return pl.pallas_call(
        functools.partial(_proj_kernel, nq, nr),
        grid=(n // tm,),
        in_specs=[
            pl.BlockSpec((tm, d), row),
            pl.BlockSpec((1, d), lambda i: (0, 0)),
            pl.BlockSpec((n_col, d, COL_TILE), lambda i: (0, 0, 0), pipeline_mode=pl.Buffered(1)),
            pl.BlockSpec((LANES, LANES), lambda i: (0, 0)),
            vec, vec, tab, tab, tab, tab,
        ],
        out_specs=[
            pl.BlockSpec((tm, d_attn), row),
            pl.BlockSpec((tm, d_kv), row),
            pl.BlockSpec((tm, d_kv), row),
            pl.BlockSpec((tm, d_ret), row),
            pl.BlockSpec((tm, d_ret), row),
            pl.BlockSpec((tm, d_ret), row),
            pl.BlockSpec((tm, d_ret), row),
        ],
        out_shape=[
            jax.ShapeDtypeStruct((n, d_attn), BF16),
            jax.ShapeDtypeStruct((n, d_kv), F32),
            jax.ShapeDtypeStruct((n, d_kv), F32),
            jax.ShapeDtypeStruct((n, d_ret), BF16),
            jax.ShapeDtypeStruct((n, d_ret), F32),
            jax.ShapeDtypeStruct((n, d_ret), BF16),
            jax.ShapeDtypeStruct((n, d_ret), F32),
        ],
        scratch_shapes=[pltpu.VMEM((tm, d), BF16)],
        compiler_params=_params("parallel"),
        name="proj",
    )(x, ln, w, m128, gq, gk, ca, sa, cr, sr)


def _attend(items, valid, sink_ref, t, n_pairs, g_sz):
    lane = lax.broadcasted_iota(jnp.int32, (1, LANES), 1)
    slot_a = ((lane // HALF) % 2) == 0
    first_half = lane < HEAD_DIM
    zero = jnp.zeros((), BF16)
    jobs = [(i, p) for i in range(len(items)) for p in range(n_pairs)]
    scores = []
    for i, p in jobs:
        qa, k_all, _ = items[i]
        parts = []
        for g in range(g_sz):
            qb = qa[:, (p * g_sz + g) * LANES:(p * g_sz + g + 1) * LANES]
            parts.append(jnp.where(slot_a, qb, zero))
            parts.append(jnp.where(slot_a, zero, qb))
        qs = jnp.concatenate(parts, axis=0)
        scores.append(lax.dot_general(qs, k_all[:, p * LANES:(p + 1) * LANES], (((1,), (1,)), ((), ())),
                                      preferred_element_type=F32))
    probs = []
    for (i, p), s in zip(jobs, scores):
        if valid is not None:
            s = jnp.where(valid, s, -jnp.inf)
        sk = sink_ref[p][:, 0:1]
        m = jnp.maximum(jnp.max(s, axis=-1, keepdims=True), sk)
        e = jnp.exp(s - m)
        den = jnp.sum(e, axis=-1, keepdims=True) + jnp.exp(sk - m)
        probs.append((e / den).astype(BF16))
    outs = [[] for _ in items]
    for (i, p), pn in zip(jobs, probs):
        o = jnp.dot(pn, items[i][2][:, p * LANES:(p + 1) * LANES], preferred_element_type=F32)
        for g in range(g_sz):
            o_a = o[(2 * g) * t:(2 * g + 1) * t]
            o_b = o[(2 * g + 1) * t:(2 * g + 2) * t]
            outs[i].append(jnp.where(first_half, o_a, o_b))
    return outs


def _retain(items, dmat_ref, decin_ref, deckv_ref, dchunk_ref, gng, gnb, n_heads):
    nt = (((1,), (1,)), ((), ()))
    tn = (((0,), (0,)), ((), ()))
    jobs = [(i, h) for i in range(len(items)) for h in range(n_heads)]
    sl = lambda h: slice(h * RET_HEAD_DIM, (h + 1) * RET_HEAD_DIM)
    intra, inter, r_prevs = [], [], []
    for i, h in jobs:
        qr, kr, _, _, r_in_ref, _ = items[i]
        q = qr[:, sl(h)]
        r_prev = r_in_ref[h]
        r_prevs.append(r_prev)
        intra.append(lax.dot_general(q, kr[:, sl(h)].astype(BF16), nt, preferred_element_type=F32) * dmat_ref[h])
        inter.append(jnp.dot(q, r_prev.astype(BF16), preferred_element_type=F32) * decin_ref[h])
    for (i, h), r_prev in zip(jobs, r_prevs):
        _, kr, vr, _, _, r_out_ref = items[i]
        kd = (kr[:, sl(h)] * deckv_ref[h]).astype(BF16)
        kv = lax.dot_general(kd, vr[:, sl(h)], tn, preferred_element_type=F32)
        r_out_ref[h] = r_prev * dchunk_ref[h] + kv
    outs = [[] for _ in items]
    for (i, h), s, o_inter in zip(jobs, intra, inter):
        _, _, vr, gate, _, _ = items[i]
        o = jnp.dot(s.astype(BF16), vr[:, sl(h)], preferred_element_type=F32) + o_inter
        mu = jnp.mean(o, axis=-1, keepdims=True)
        dev = o - mu
        var = jnp.mean(dev * dev, axis=-1, keepdims=True)
        r = dev * lax.rsqrt(var + GN_EPS) * gng[:, sl(h)] + gnb[:, sl(h)]
        gt = gate[:, sl(h)]
        outs[i].append(gt * jax.nn.sigmoid(gt) * r)
    return outs


def _store_merged(merged_ref, ao, ro, d_attn):
    for b, blk in enumerate(ao):
        merged_ref[:, b * LANES:(b + 1) * LANES] = blk.astype(BF16)
    for h, blk in enumerate(ro):
        merged_ref[:, d_attn + h * RET_HEAD_DIM:d_attn + (h + 1) * RET_HEAD_DIM] = blk.astype(BF16)


def _mixer_prompt_kernel(nc, t, batch, n_pairs, g_sz, n_heads, d_attn,
                         dchunk_ref, qa_ref, *rest):
    n_win = WINDOW // t + 1
    k_refs, v_refs = rest[:n_win], rest[n_win:2 * n_win]
    (qr_ref, kr_ref, vr_ref, gate_ref, sink_ref, dmat_ref, decin_ref, deckv_ref, gng_ref, gnb_ref,
     merged_ref, rout_ref, r_ref) = rest[2 * n_win:]
    c = pl.program_id(0)

    @pl.when(c == 0)
    def _reset():
        r_ref[...] = jnp.zeros_like(r_ref)

    kidx = lax.broadcasted_iota(jnp.int32, (1, n_win * t), 1)
    valid = kidx >= (n_win - 1 - c) * t
    att_items = [(qa_ref[b],
                  jnp.concatenate([r[b].astype(BF16) for r in k_refs], axis=0),
                  jnp.concatenate([r[b].astype(BF16) for r in v_refs], axis=0)) for b in range(batch)]
    ret_items = [(qr_ref[b], kr_ref[b], vr_ref[b], gate_ref[b], r_ref.at[b], r_ref.at[b]) for b in range(batch)]
    ao = _attend(att_items, valid, sink_ref, t, n_pairs, g_sz)
    ro = _retain(ret_items, dmat_ref, decin_ref, deckv_ref, dchunk_ref, gng_ref[...], gnb_ref[...], n_heads)
    for b in range(batch):
        _store_merged(merged_ref.at[b], ao[b], ro[b], d_attn)

    @pl.when(c == nc - 1)
    def _emit_state():
        rout_ref[...] = r_ref[...]


def _mixer_sample_kernel(t, n_pairs, g_sz, n_heads, d_attn,
                         dchunk_ref, qa_ref, ka_ref, va_ref, qr_ref, kr_ref, vr_ref, gate_ref,
                         ck_ref, cv_ref, r0_ref,
                         sink_ref, dmat_ref, decin_ref, deckv_ref, gng_ref, gnb_ref,
                         merged_ref, rout_ref):
    k_all = jnp.concatenate([ck_ref[...], ka_ref[...].astype(BF16)], axis=0)
    v_all = jnp.concatenate([cv_ref[...], va_ref[...].astype(BF16)], axis=0)
    ao = _attend([(qa_ref[...], k_all, v_all)], None, sink_ref, t, n_pairs, g_sz)
    ro = _retain([(qr_ref[...], kr_ref[...], vr_ref[...], gate_ref[...], r0_ref, rout_ref)],
                 dmat_ref, decin_ref, deckv_ref, dchunk_ref, gng_ref[...], gnb_ref[...], n_heads)
    _store_merged(merged_ref, ao[0], ro[0], d_attn)


def _sink_table(sinks, sink_heads, t):
    per_row = jnp.repeat(sinks.astype(F32)[sink_heads], t, axis=1)
    return jnp.broadcast_to(per_row[:, :, None], per_row.shape + (LANES,))


def _mixer_prompt(proj_out, sinks, sink_heads, gng, gnb, batch, seq, n_kv, n_q, n_heads):
    qa, ka, va, qr, kr, vr, gate = proj_out
    t = CHUNK
    nc = seq // t
    d_attn, d_kv, d_ret = n_q * HEAD_DIM, n_kv * HEAD_DIM, n_heads * RET_HEAD_DIM
    n_pairs, g_sz = n_kv // 2, n_q // n_kv
    dmat, decin, deckv, dchunk = _decay_tables(n_heads, t)
    sink_tab = _sink_table(sinks, sink_heads, t)
    seq3 = lambda a: a.reshape(batch, seq, a.shape[-1])
    qa, ka, va, qr, kr, vr, gate = map(seq3, proj_out)
    row = lambda c, *_: (0, c, 0)
    back = lambda k: (lambda c, *_: (0, jnp.maximum(c - k, 0), 0))
    const3 = lambda c, *_: (0, 0, 0)
    const2 = lambda c, *_: (0, 0)
    n_win = WINDOW // t + 1
    kv_specs = [pl.BlockSpec((batch, t, d_kv), back(n_win - 1 - s)) for s in range(n_win)]
    state_shape = (batch, n_heads, RET_HEAD_DIM, RET_HEAD_DIM)
    grid_spec = pltpu.PrefetchScalarGridSpec(
        num_scalar_prefetch=1,
        grid=(nc,),
        in_specs=[
            pl.BlockSpec((batch, t, d_attn), row), *kv_specs, *kv_specs,
            pl.BlockSpec((batch, t, d_ret), row), pl.BlockSpec((batch, t, d_ret), row),
            pl.BlockSpec((batch, t, d_ret), row), pl.BlockSpec((batch, t, d_ret), row),
            pl.BlockSpec(sink_tab.shape, const3), pl.BlockSpec(dmat.shape, const3),
            pl.BlockSpec(decin.shape, const3), pl.BlockSpec(deckv.shape, const3),
            pl.BlockSpec((1, d_ret), const2), pl.BlockSpec((1, d_ret), const2),
        ],
        out_specs=[
            pl.BlockSpec((batch, t, d_attn + d_ret), row),
            pl.BlockSpec(state_shape, lambda c, *_: (0, 0, 0, 0)),
        ],
        scratch_shapes=[pltpu.VMEM(state_shape, F32)],
    )
    merged, r_fin = pl.pallas_call(
        functools.partial(_mixer_prompt_kernel, nc, t, batch, n_pairs, g_sz, n_heads, d_attn),
        grid_spec=grid_spec,
        out_shape=[jax.ShapeDtypeStruct((batch, seq, d_attn + d_ret), BF16),
                   jax.ShapeDtypeStruct(state_shape, F32)],
        compiler_params=_params("arbitrary"),
        name="mixer_prompt",
    )(dchunk, qa, *([ka] * n_win), *([va] * n_win), qr, kr, vr, gate, sink_tab, dmat, decin, deckv, gng, gnb)
    return merged.reshape(batch * seq, d_attn + d_ret), r_fin


def _mixer_sample(proj_out, ck, cv, r0, sinks, sink_heads, gng, gnb, row0, dbatch, t, n_kv, n_q, n_heads):
    qa, ka, va, qr, kr, vr, gate = proj_out
    d_attn, d_kv, d_ret = n_q * HEAD_DIM, n_kv * HEAD_DIM, n_heads * RET_HEAD_DIM
    n_pairs, g_sz = n_kv // 2, n_q // n_kv
    dmat, decin, deckv, dchunk = _decay_tables(n_heads, t)
    sink_tab = _sink_table(sinks, sink_heads, t)
    blk0 = row0 // t
    row = lambda b, *_: (blk0 + b, 0)
    const3 = lambda b, *_: (0, 0, 0)
    const2 = lambda b, *_: (0, 0)
    state_spec = pl.BlockSpec((None, n_heads, RET_HEAD_DIM, RET_HEAD_DIM), lambda b, *_: (b, 0, 0, 0))
    n_hist = ck.shape[1]
    grid_spec = pltpu.PrefetchScalarGridSpec(
        num_scalar_prefetch=1,
        grid=(dbatch,),
        in_specs=[
            pl.BlockSpec((t, d_attn), row), pl.BlockSpec((t, d_kv), row), pl.BlockSpec((t, d_kv), row),
            pl.BlockSpec((t, d_ret), row), pl.BlockSpec((t, d_ret), row), pl.BlockSpec((t, d_ret), row),
            pl.BlockSpec((t, d_ret), row),
            pl.BlockSpec((None, n_hist, d_kv), lambda b, *_: (b, 0, 0)),
            pl.BlockSpec((None, n_hist, d_kv), lambda b, *_: (b, 0, 0)),
            state_spec,
            pl.BlockSpec(sink_tab.shape, const3), pl.BlockSpec(dmat.shape, const3),
            pl.BlockSpec(decin.shape, const3), pl.BlockSpec(deckv.shape, const3),
            pl.BlockSpec((1, d_ret), const2), pl.BlockSpec((1, d_ret), const2),
        ],
        out_specs=[pl.BlockSpec((t, d_attn + d_ret), lambda b, *_: (b, 0)), state_spec],
    )
    return pl.pallas_call(
        functools.partial(_mixer_sample_kernel, t, n_pairs, g_sz, n_heads, d_attn),
        grid_spec=grid_spec,
        out_shape=[jax.ShapeDtypeStruct((dbatch * t, d_attn + d_ret), BF16),
                   jax.ShapeDtypeStruct(r0.shape, F32)],
        compiler_params=_params("parallel"),
        name="mixer_sample",
    )(dchunk, qa, ka, va, qr, kr, vr, gate, ck, cv, r0, sink_tab, dmat, decin, deckv, gng, gnb)


def _outproj_kernel(m_ref, w_ref, x_ref, o_ref):
    o_ref[...] = x_ref[...] + jnp.dot(m_ref[...], w_ref[...], preferred_element_type=F32)


def _outproj(merged, w, x):
    n, d = x.shape
    k = merged.shape[1]
    tm = _pick_tile(n, 1024)
    tn = _pick_tile(d, 1024, LANES)
    return pl.pallas_call(
        _outproj_kernel,
        grid=(n // tm, d // tn),
        in_specs=[pl.BlockSpec((tm, k), lambda i, j: (i, 0)),
                  pl.BlockSpec((k, tn), lambda i, j: (0, j)),
                  pl.BlockSpec((tm, tn), lambda i, j: (i, j))],
        out_specs=pl.BlockSpec((tm, tn), lambda i, j: (i, j)),
        out_shape=jax.ShapeDtypeStruct((n, d), F32),
        compiler_params=_params("parallel", "arbitrary"),
        name="outproj",
    )(merged, w, x)


def _swiglu_step(xn, wg, wu, wd):
    g = jnp.dot(xn, wg, preferred_element_type=F32)
    u = jnp.dot(xn, wu, preferred_element_type=F32)
    h = (g * jax.nn.sigmoid(g) * u).astype(BF16)
    return jnp.dot(h, wd, preferred_element_type=F32)


def _ffn_kernel(x_ref, ln_ref, wg_ref, wu_ref, wd_ref, o_ref, xn_ref):
    f = pl.program_id(1)

    @pl.when(f == 0)
    def _norm():
        x = x_ref[...]
        ms = jnp.mean(x * x, axis=-1, keepdims=True)
        xn_ref[...] = (x * lax.rsqrt(ms + RMS_EPS) * ln_ref[...]).astype(BF16)
        o_ref[...] = x

    o_ref[...] += _swiglu_step(xn_ref[...], wg_ref[...], wu_ref[...], wd_ref[...])


def _ffn(x, ln, wg, wu, wd):
    n, d = x.shape
    fp = wg.shape[1]
    tf = _pick_tile(fp, 512, LANES)
    tm = _pick_tile(n, 1024)
    return pl.pallas_call(
        _ffn_kernel,
        grid=(n // tm, fp // tf),
        in_specs=[pl.BlockSpec((tm, d), lambda i, f: (i, 0)),
                  pl.BlockSpec((1, d), lambda i, f: (0, 0)),
                  pl.BlockSpec((d, tf), lambda i, f: (0, f)),
                  pl.BlockSpec((d, tf), lambda i, f: (0, f)),
                  pl.BlockSpec((tf, d), lambda i, f: (f, 0))],
        out_specs=pl.BlockSpec((tm, d), lambda i, f: (i, 0)),
        out_shape=jax.ShapeDtypeStruct((n, d), F32),
        scratch_shapes=[pltpu.VMEM((tm, d), BF16)],
        compiler_params=_params("parallel", "arbitrary"),
        name="ffn",
    )(x, ln, wg, wu, wd)


def _router_kernel(n_exp, x_ref, ln_ref, wr_hi_ref, wr_lo_ref, hn_ref, idx_ref, gate_ref):
    x = x_ref[...]
    ms = jnp.mean(x * x, axis=-1, keepdims=True)
    hf = x * lax.rsqrt(ms + RMS_EPS) * ln_ref[...]
    hn = hf.astype(BF16)
    bits = lax.bitcast_convert_type(hn.astype(F32), jnp.uint32)
    half = bits.shape[1] // 2
    hn_ref[...] = (bits[:, :half] >> 16) | (bits[:, half:] & jnp.uint32(0xFFFF0000))
    h_lo = (hf - hn.astype(F32)).astype(BF16)
    logits = (jnp.dot(hn, wr_hi_ref[...], preferred_element_type=F32)
              + jnp.dot(h_lo, wr_hi_ref[...], preferred_element_type=F32)
              + jnp.dot(hn, wr_lo_ref[...], preferred_element_type=F32))
    lane = lax.broadcasted_iota(jnp.int32, logits.shape, 1)
    logits = jnp.where(lane < n_exp, logits, -jnp.inf)
    m1 = jnp.max(logits, axis=-1, keepdims=True)
    i1 = jnp.min(jnp.where(logits == m1, lane, LANES), axis=-1, keepdims=True)
    rest = jnp.where(lane == i1, -jnp.inf, logits)
    m2 = jnp.max(rest, axis=-1, keepdims=True)
    i2 = jnp.min(jnp.where(rest == m2, lane, LANES), axis=-1, keepdims=True)
    e2 = jnp.exp(m2 - m1)
    den = 1.0 + e2
    idx_ref[...] = jnp.where(lane == 0, i1, jnp.where(lane == 1, i2, 0))
    gate_ref[...] = jnp.where(lane == 0, 1.0 / den, jnp.where(lane == 1, e2 / den, 0.0))


def _router(x, ln, wr_hi, wr_lo, n_exp):
    n, d = x.shape
    tm = _pick_tile(n, 512)
    row = lambda i: (i, 0)
    return pl.pallas_call(
        functools.partial(_router_kernel, n_exp),
        grid=(n // tm,),
        in_specs=[pl.BlockSpec((tm, d), row), pl.BlockSpec((1, d), lambda i: (0, 0)),
                  pl.BlockSpec((d, LANES), lambda i: (0, 0)), pl.BlockSpec((d, LANES), lambda i: (0, 0))],
        out_specs=[pl.BlockSpec((tm, d // 2), row), pl.BlockSpec((tm, LANES), row), pl.BlockSpec((tm, LANES), row)],
        out_shape=[jax.ShapeDtypeStruct((n, d // 2), jnp.uint32), jax.ShapeDtypeStruct((n, LANES), jnp.int32),
                   jax.ShapeDtypeStruct((n, LANES), F32)],
        compiler_params=_params("parallel"),
        name="router",
    )(x, ln, wr_hi, wr_lo)


def _experts_kernel(tm, nf, n_split, te_ref, act_ref, tok_ref, nxt_ref, hn_hbm, *rest):
    wg_refs, wu_refs, wd_refs = rest[:n_split], rest[n_split:2 * n_split], rest[2 * n_split:3 * n_split]
    o_ref, rows_ref, x_ref, wgb_ref, wub_ref, wdb_ref, sem = rest[3 * n_split:]
    i = pl.program_id(0)
    f = pl.program_id(1)
    act = act_ref[i]
    slot = i % 2
    per_step = rows_ref.shape[1] // nf

    def start_row(tokens_ref, dst_slot, v):
        tok = tokens_ref[0, jnp.minimum(v, tm - 1)]
        pltpu.make_async_copy(hn_hbm.at[pl.ds(tok, 1)], rows_ref.at[dst_slot, pl.ds(v, 1)],
                              sem.at[dst_slot]).start()

    @pl.when(f == 0)
    def _init():
        o_ref[...] = jnp.zeros_like(o_ref)

    @pl.when((f == 0) & (i == 0) & (act > 0))
    def _first_tile_rows():
        def body(v, carry):
            start_row(tok_ref, 0, v)
            return carry

        lax.fori_loop(0, per_step * nf, body, 0, unroll=8)

    rows_started = jnp.where(i == 0, act, act_ref[jnp.maximum(i - 1, 0)]) > 0

    @pl.when((f == 0) & rows_started)
    def _rows_ready():
        pltpu.make_async_copy(hn_hbm.at[pl.ds(0, per_step * nf)], rows_ref.at[slot], sem.at[slot]).wait()

    @pl.when((f == 0) & (act > 0))
    def _unpack():
        words = rows_ref[slot, :tm, :]
        half = words.shape[1]
        x_ref[:, :half] = lax.bitcast_convert_type(words << 16, F32).astype(BF16)
        x_ref[:, half:] = lax.bitcast_convert_type(words & jnp.uint32(0xFFFF0000), F32).astype(BF16)

    def start_next_tile_rows():
        for r in range(per_step):
            start_row(nxt_ref, 1 - slot, f * per_step + r)

    def round_weights():
        for slabs, dst in ((wg_refs, wgb_ref), (wu_refs, wub_ref), (wd_refs, wdb_ref)):
            rows = dst.shape[0] // n_split
            for k, slab in enumerate(slabs):
                dst[k * rows:(k + 1) * rows, :] = slab[...].astype(BF16)

    @pl.when(act == 1)
    def _step():
        start_next_tile_rows()
        round_weights()
        o_ref[...] += _swiglu_step(x_ref[...], wgb_ref[...], wub_ref[...], wdb_ref[...])

    @pl.when(act == 2)
    def _half_step():
        start_next_tile_rows()
        round_weights()
        o_ref[:tm // 2, :] += _swiglu_step(x_ref[:tm // 2, :], wgb_ref[...], wub_ref[...], wdb_ref[...])


def _experts(hn32, src_tok, wg, wu, wd, tile_expert, tile_active, tm):
    n, half = hn32.shape
    d = 2 * half
    assert n >= tm
    n_tiles = src_tok.shape[0] // tm
    n_exp, _, fe = wg.shape
    tf = _pick_tile(fe, 512, LANES)
    nf = fe // tf
    fidx = lambda i, f, te, act: jnp.where(act[i] > 0, f, nf - 1)
    n_split = EXPERT_WEIGHT_STREAMS
    in_slab = lambda k: (lambda i, f, te, act: (te[i], k, fidx(i, f, te, act)))
    out_slab = lambda k: (lambda i, f, te, act: (te[i], n_split * fidx(i, f, te, act) + k, 0))
    in_specs = [pl.BlockSpec((None, d // n_split, tf), in_slab(k)) for k in range(n_split)]
    out_specs_w = [pl.BlockSpec((None, tf // n_split, d), out_slab(k)) for k in range(n_split)]
    buf_rows = next(r for r in range(tm, tm + 8 * nf + 1) if r % nf == 0 and r % 8 == 0)
    assert n >= buf_rows
    tokens = src_tok.reshape(n_tiles, 1, tm)
    tok_spec = lambda step: pl.BlockSpec((None, 1, tm), lambda i, f, te, act: (jnp.minimum(i + step, n_tiles - 1), 0, 0),
                                         memory_space=pltpu.SMEM)
    grid_spec = pltpu.PrefetchScalarGridSpec(
        num_scalar_prefetch=2,
        grid=(n_tiles, nf),
        in_specs=[tok_spec(0), tok_spec(1), pl.BlockSpec(memory_space=pl.ANY),
                  *in_specs, *in_specs, *out_specs_w],
        out_specs=pl.BlockSpec((tm, d), lambda i, f, te, act: (i, 0), pipeline_mode=pl.Buffered(1)),
        scratch_shapes=[pltpu.VMEM((2, buf_rows, half), jnp.uint32), pltpu.VMEM((tm, d), BF16),
                        pltpu.VMEM((d, tf), BF16), pltpu.VMEM((d, tf), BF16), pltpu.VMEM((tf, d), BF16),
                        pltpu.SemaphoreType.DMA((2,))],
    )
    return pl.pallas_call(
        functools.partial(_experts_kernel, tm, nf, n_split),
        grid_spec=grid_spec,
        out_shape=jax.ShapeDtypeStruct((n_tiles * tm, d), F32),
        compiler_params=_params("arbitrary", "arbitrary"),
        name="experts",
    )(tile_expert, tile_active, tokens, tokens, hn32,
      *([wg] * n_split), *([wu] * n_split), *([wd] * n_split))


def _combine_kernel(x_ref, y0_ref, y1_ref, g_ref, o_ref):
    g = g_ref[...]
    o_ref[...] = x_ref[...] + (g[:, 0:1] * y0_ref[...] + g[:, 1:2] * y1_ref[...])


def _combine(x, y0, y1, gate, row0):
    n, d = x.shape
    tm = _pick_tile(n, 256)
    assert row0 % tm == 0
    blk0 = row0 // tm
    own = lambda i: (i, 0)
    shifted = lambda i: (blk0 + i, 0)
    return pl.pallas_call(
        _combine_kernel,
        grid=(n // tm,),
        in_specs=[pl.BlockSpec((tm, d), own), pl.BlockSpec((tm, d), shifted), pl.BlockSpec((tm, d), shifted),
                  pl.BlockSpec((tm, LANES), own)],
        out_specs=pl.BlockSpec((tm, d), own),
        out_shape=jax.ShapeDtypeStruct((n, d), F32),
        compiler_params=_params("parallel"),
        name="combine",
    )(x, y0, y1, gate)


def _moe(x_p, x_s, ln, w_router, wg, wu, wd):
    n_p, d = x_p.shape
    n = n_p + x_s.shape[0]
    n_exp = w_router.shape[1]
    wr = jnp.pad(w_router.astype(F32), ((0, 0), (0, LANES - n_exp)))
    wr_hi = wr.astype(BF16)
    wr_lo = (wr - wr_hi.astype(F32)).astype(BF16)
    hn_p, ridx_p, rgate_p = _router(x_p, ln, wr_hi, wr_lo, n_exp)
    hn_s, ridx_s, rgate_s = _router(x_s, ln, wr_hi, wr_lo, n_exp)
    hn = jnp.concatenate([hn_p, hn_s], axis=0)
    tm = 1024 if TOP_K * n >= 8192 else 64
    n_asg = TOP_K * n
    n_tiles = -(-(n_asg + n_exp * (tm - 1)) // tm) + 1
    flat_e = jnp.concatenate([ridx_p[:, :TOP_K], ridx_s[:, :TOP_K]], axis=0).reshape(-1)
    onehot = (flat_e[:, None] == jnp.arange(n_exp, dtype=jnp.int32)[None, :]).astype(jnp.int32)
    csum = jnp.cumsum(onehot, axis=0)
    counts = csum[-1]
    rank = jnp.take_along_axis(csum, flat_e[:, None], axis=1)[:, 0] - 1
    padded = ((counts + tm - 1) // tm) * tm
    ends = jnp.cumsum(padded)
    pos = (ends - padded)[flat_e] + rank
    src_tok = jnp.zeros((n_tiles * tm,), jnp.int32).at[pos].set(
        jnp.arange(n_asg, dtype=jnp.int32) // TOP_K, unique_indices=True, mode="promise_in_bounds")
    tile_start = jnp.arange(n_tiles, dtype=jnp.int32) * tm
    te = jnp.sum((tile_start[:, None] >= ends[None, :]).astype(jnp.int32), axis=1)
    tile_expert = jnp.minimum(te, n_exp - 1)
    rows_in_tile = (ends - padded + counts)[tile_expert] - tile_start
    tile_active = jnp.where(te >= n_exp, 0, jnp.where(rows_in_tile <= tm // 2, 2, 1)).astype(jnp.int32)
    ys = _experts(hn, src_tok, wg, wu, wd, tile_expert, tile_active, tm)
    pos2 = pos.reshape(n, TOP_K)
    y0 = jnp.take(ys, pos2[:, 0], axis=0, mode="clip")
    y1 = jnp.take(ys, pos2[:, 1], axis=0, mode="clip")
    return _combine(x_p, y0, y1, rgate_p, 0), _combine(x_s, y0, y1, rgate_s, n_p)


def kernel(x_prompt, x_sample, cache_attn_k, cache_attn_v, state_ret, ln_mix, w_in, q_norm, k_norm, attn_sinks,
           ret_gn_g, ret_gn_b, w_out, ln_ffn, ffn_w_gate, ffn_w_up, ffn_w_down, moe_router, moe_w_gate, moe_w_up,
           moe_w_down):
    batch, seq, d = x_prompt.shape
    dbatch, t_s, _ = x_sample.shape
    depth = ln_mix.shape[0]
    n_kv = cache_attn_k.shape[3]
    n_heads = state_ret.shape[2]
    d_kv = n_kv * HEAD_DIM
    d_ret = n_heads * RET_HEAD_DIM
    d_attn = w_out.shape[1] - d_ret
    n_q = d_attn // HEAD_DIM
    assert cache_attn_k.shape[4] == HEAD_DIM and state_ret.shape[3] == RET_HEAD_DIM and n_kv % 2 == 0
    assert seq % CHUNK == 0 and WINDOW % CHUNK == 0 and seq >= WINDOW and cache_attn_k.shape[2] == WINDOW
    assert w_in.shape[2] == d_attn + 2 * d_kv + 4 * d_ret
    n_p, n_s = batch * seq, dbatch * t_s
    keep = min(WINDOW, seq)

    perm_q, perm_k, perm_ao, sink_heads = _layout_perms(n_q, n_kv)
    inv_k = np.argsort(perm_k)
    qk_perm = np.concatenate([perm_q, d_attn + perm_k])

    tabs_p = _rope_tables(jnp.arange(seq))
    tabs_s = _rope_tables(PAST_LEN + jnp.tile(jnp.arange(t_s), dbatch))
    lane = np.arange(LANES)
    m128 = jnp.asarray(((lane[:, None] // HALF) % 2) == ((lane[None, :] // HALF) % 2), BF16)
    head_gain = lambda g: jnp.tile(g.astype(F32).reshape(2, HALF), (1, 2)).reshape(1, LANES)

    x_p, x_s = x_prompt.reshape(n_p, d), x_sample.reshape(n_s, d)
    nk_p, nv_p, nr_p, nk_s, nv_s, nr_s = [], [], [], [], [], []
    for l in range(depth):
        w_in_l = jnp.concatenate([w_in[l][:, :d_attn + d_kv][:, qk_perm], w_in[l][:, d_attn + d_kv:]], axis=1)
        w_in_l = w_in_l.astype(BF16).reshape(d, -1, COL_TILE).transpose(1, 0, 2)
        w_out_l = jnp.concatenate([w_out[l][:d_attn][perm_ao], w_out[l][d_attn:]], axis=0).astype(BF16)
        ln, gq, gk = ln_mix[l][None, :], head_gain(q_norm[l]), head_gain(k_norm[l])
        gng, gnb = ret_gn_g[l][None, :], ret_gn_b[l][None, :]
        proj_p = _proj(x_p, ln, w_in_l, m128, gq, gk, *tabs_p, d_attn, d_kv, d_ret, 512)
        proj_s = _proj(x_s, ln, w_in_l, m128, gq, gk, *tabs_s, d_attn, d_kv, d_ret, 256)
        merged_p, r_p = _mixer_prompt(proj_p, attn_sinks[l], sink_heads, gng, gnb, batch, seq, n_kv, n_q, n_heads)
        ck = cache_attn_k[l].reshape(dbatch, WINDOW, d_kv)[:, :, perm_k].astype(BF16)
        cv = cache_attn_v[l].reshape(dbatch, WINDOW, d_kv).astype(BF16)
        merged_s, r_s = _mixer_sample(proj_s, ck, cv, state_ret[l], attn_sinks[l], sink_heads, gng, gnb,
                                      0, dbatch, t_s, n_kv, n_q, n_heads)
        x_p = _outproj(merged_p, w_out_l, x_p)
        x_s = _outproj(merged_s, w_out_l, x_s)

        ka_last = proj_p[1].reshape(batch, seq, d_kv)[:, seq - keep:]
        nk_p.append(ka_last[:, :, inv_k].reshape(batch, keep, n_kv, HEAD_DIM))
        nv_p.append(proj_p[2].reshape(batch, seq, d_kv)[:, seq - keep:].reshape(batch, keep, n_kv, HEAD_DIM))
        nk_s.append(proj_s[1][:, inv_k].reshape(dbatch, t_s, n_kv, HEAD_DIM))
        nv_s.append(proj_s[2].reshape(dbatch, t_s, n_kv, HEAD_DIM))
        nr_p.append(r_p)
        nr_s.append(r_s)

        i = l // 2
        lf = ln_ffn[l][None, :]
        if l % 2 == 0:
            fpad = (-ffn_w_gate.shape[2]) % COL_TILE
            wg = jnp.pad(ffn_w_gate[i].astype(BF16), ((0, 0), (0, fpad)))
            wu = jnp.pad(ffn_w_up[i].astype(BF16), ((0, 0), (0, fpad)))
            wd = jnp.pad(ffn_w_down[i].astype(BF16), ((0, fpad), (0, 0)))
            x_p, x_s = _ffn(x_p, lf, wg, wu, wd), _ffn(x_s, lf, wg, wu, wd)
        else:
            x_p, x_s = _moe(x_p, x_s, lf, moe_router[i], moe_w_gate[i], moe_w_up[i], moe_w_down[i])

    return (x_p.reshape(batch, seq, d), x_s.reshape(dbatch, t_s, d),
            jnp.stack(nk_p), jnp.stack(nv_p), jnp.stack(nr_p), jnp.stack(nk_s), jnp.stack(nv_s), jnp.stack(nr_s))
```

```python
import functools

import numpy as np
import jax
import jax.numpy as jnp
from jax import lax
from jax.experimental import pallas as pl
from jax.experimental.pallas import tpu as pltpu

CHUNK = 64
WINDOW = 128
HEAD_DIM = 64
RET_HEAD_DIM = 128
PAST_LEN = 2048
ROPE_THETA = 10000.0
TOP_K = 2
RMS_EPS = 1e-6
GN_EPS = 1e-5

LANES = 128
HALF = HEAD_DIM // 2
VMEM_LIMIT = 56 * 1024 * 1024
COL_TILE = 512
EXPERT_WEIGHT_STREAMS = 1
F32 = jnp.float32
BF16 = jnp.bfloat16


def _pick_tile(n, target, mult=16):
    best = None
    for t in range(mult, min(n, target) + 1, mult):
        if n % t == 0:
            best = t
    assert best is not None, (n, target)
    return best


def _params(*sem):
    return pltpu.CompilerParams(dimension_semantics=sem, vmem_limit_bytes=VMEM_LIMIT)


def _layout_perms(n_q, n_kv):
    g_sz = n_q // n_kv
    n_pairs = n_kv // 2
    perm_k = np.zeros(n_kv * HEAD_DIM, np.int32)
    perm_q = np.zeros(n_q * HEAD_DIM, np.int32)
    perm_ao = np.zeros(n_q * HEAD_DIM, np.int32)
    sink_heads = np.zeros((n_pairs, 2 * g_sz), np.int32)
    for p in range(n_pairs):
        for hf in range(2):
            for s in range(2):
                for r in range(HALF):
                    perm_k[p * LANES + hf * 64 + s * HALF + r] = (2 * p + s) * HEAD_DIM + hf * HALF + r
        for g in range(g_sz):
            bq = p * g_sz + g
            for s in range(2):
                head = g_sz * (2 * p + s) + g
                sink_heads[p, 2 * g + s] = head
                for hf in range(2):
                    for r in range(HALF):
                        perm_q[bq * LANES + hf * 64 + s * HALF + r] = head * HEAD_DIM + hf * HALF + r
                for d in range(HEAD_DIM):
                    perm_ao[bq * LANES + s * HEAD_DIM + d] = head * HEAD_DIM + d
    return perm_q, perm_k, perm_ao, sink_heads


def _rope_tables(pos):
    f32 = np.float32
    pf = np.asarray(pos).astype(f32)[:, None]
    inv_a = (f32(ROPE_THETA) ** (-np.arange(0, HEAD_DIM, 2, dtype=f32) / f32(HEAD_DIM))).astype(f32)
    ang = pf * inv_a[None, :]
    c, s = np.cos(ang), np.sin(ang)
    ca = np.concatenate([c, c, c, c], axis=-1)
    sa = np.concatenate([-s, -s, s, s], axis=-1)
    inv_r = (f32(10000.0) ** (-np.linspace(0.0, 1.0, RET_HEAD_DIM // 2, dtype=f32))).astype(f32)
    ang = pf * inv_r[None, :]
    c, s = np.cos(ang), np.sin(ang)
    cr = np.concatenate([c, c], axis=-1)
    sr = np.concatenate([-s, s], axis=-1)
    return tuple(jnp.asarray(t, F32) for t in (ca, sa, cr, sr))


def _decay_tables(n_heads, chunk):
    f32 = np.float32
    log_g = np.log1p(-np.exp2(f32(-5.0) - np.arange(n_heads, dtype=f32))).astype(f32)
    idx = np.arange(chunk, dtype=f32)
    dmat = np.exp(log_g[:, None, None] * np.abs(idx[:, None] - idx[None, :]))
    dec_in = np.exp(log_g[None, :] * (idx[:, None] + f32(1.0)))
    dec_kv = np.exp(log_g[None, :] * (f32(chunk - 1.0) - idx[:, None]))
    dec_chunk = np.exp(log_g * f32(chunk))
    bc = lambda t: np.broadcast_to(t.T[:, :, None], (n_heads, chunk, LANES))
    return tuple(jnp.asarray(t, F32) for t in (dmat, bc(dec_in), bc(dec_kv), dec_chunk))


def _rope128(y, c, s):
    return y * c + pltpu.roll(y, 64, 1) * s


def _proj_kernel(nq, nr, x_ref, ln_ref, w_ref, m_ref, gq_ref, gk_ref, ca_ref, sa_ref, cr_ref, sr_ref,
                 qa_ref, ka_ref, va_ref, qr_ref, kr_ref, vr_ref, gate_ref, xn_ref):
    x = x_ref[...]
    ms = jnp.mean(x * x, axis=-1, keepdims=True)
    xn_ref[...] = (x * lax.rsqrt(ms + RMS_EPS) * ln_ref[...]).astype(BF16)
    n_sub = COL_TILE // LANES

    def col_tile(j):
        return jnp.dot(xn_ref[...], w_ref[:, j * COL_TILE:(j + 1) * COL_TILE], preferred_element_type=F32)

    def lanes(j, c):
        return slice(j * COL_TILE + c * LANES, j * COL_TILE + (c + 1) * LANES)

    def head_norm_rope(yc, g):
        y2 = yc * yc
        hi = y2.astype(BF16)
        lo = (y2 - hi.astype(F32)).astype(BF16)
        ss = (jnp.dot(hi, m_ref[...], preferred_element_type=F32)
              + jnp.dot(lo, m_ref[...], preferred_element_type=F32))
        yn = yc * lax.rsqrt(ss * (1.0 / HEAD_DIM) + RMS_EPS) * g
        return _rope128(yn, ca_ref[...], sa_ref[...])

    for j in range(nq):
        y = col_tile(j)
        for c in range(n_sub):
            qa_ref[:, lanes(j, c)] = (head_norm_rope(y[:, lanes(0, c)], gq_ref[...])
                                      * (HEAD_DIM ** -0.5)).astype(BF16)

    y = col_tile(nq)
    half = COL_TILE // 2
    for c in range(half // LANES):
        ka_ref[:, lanes(0, c)] = head_norm_rope(y[:, lanes(0, c)], gk_ref[...])
    va_ref[...] = y[:, half:]

    for j in range(nr):
        y = col_tile(nq + 1 + j)
        for c in range(n_sub):
            qr_ref[:, lanes(j, c)] = _rope128(y[:, lanes(0, c)], cr_ref[...], sr_ref[...]).astype(BF16)
    for j in range(nr):
        y = col_tile(nq + 1 + nr + j)
        for c in range(n_sub):
            kr_ref[:, lanes(j, c)] = (_rope128(y[:, lanes(0, c)], cr_ref[...], sr_ref[...])
                                      * (RET_HEAD_DIM ** -0.5))
    for j in range(nr):
        vr_ref[:, j * COL_TILE:(j + 1) * COL_TILE] = col_tile(nq + 1 + 2 * nr + j).astype(BF16)
    for j in range(nr):
        gate_ref[:, j * COL_TILE:(j + 1) * COL_TILE] = col_tile(nq + 1 + 3 * nr + j)


def _proj(x, ln, w, m128, gq, gk, ca, sa, cr, sr, d_attn, d_kv, d_ret, tm_target):
    n, d = x.shape
    assert d_attn % COL_TILE == 0 and 2 * d_kv == COL_TILE and d_ret % COL_TILE == 0
    nq, nr = d_attn // COL_TILE, d_ret // COL_TILE
    n_col = nq + 1 + 4 * nr
    tab_rows = ca.shape[0]
    assert n % tab_rows == 0
    tm = _pick_tile(tab_rows, tm_target)
    row = lambda i: (i, 0)
    tab = pl.BlockSpec((tm, LANES), lambda i: (i % (tab_rows // tm), 0))
    vec = pl.BlockSpec((1, LANES), lambda i: (0, 0))
    return pl.pallas_call(
        functools.partial(_proj_kernel, nq, nr),
        grid=(n // tm,),
        in_specs=[
            pl.BlockSpec((tm, d), row),
            pl.BlockSpec((1, d), lambda i: (0, 0)),
            pl.BlockSpec((d, n_col * COL_TILE), lambda i: (0, 0), pipeline_mode=pl.Buffered(1)),
            pl.BlockSpec((LANES, LANES), lambda i: (0, 0)),
            vec, vec, tab, tab, tab, tab,
        ],
        out_specs=[
            pl.BlockSpec((tm, d_attn), row),
            pl.BlockSpec((tm, d_kv), row),
            pl.BlockSpec((tm, d_kv), row),
            pl.BlockSpec((tm, d_ret), row),
            pl.BlockSpec((tm, d_ret), row),
            pl.BlockSpec((tm, d_ret), row),
            pl.BlockSpec((tm, d_ret), row),
        ],
        out_shape=[
            jax.ShapeDtypeStruct((n, d_attn), BF16),
            jax.ShapeDtypeStruct((n, d_kv), F32),
            jax.ShapeDtypeStruct((n, d_kv), F32),
            jax.ShapeDtypeStruct((n, d_ret), BF16),
            jax.ShapeDtypeStruct((n, d_ret), F32),
            jax.ShapeDtypeStruct((n, d_ret), BF16),
            jax.ShapeDtypeStruct((n, d_ret), F32),
        ],
        scratch_shapes=[pltpu.VMEM((tm, d), BF16)],
        compiler_params=_params("parallel"),
        name="proj",
    )(x, ln, w, m128, gq, gk, ca, sa, cr, sr)


def _attend(items, valid, sink_ref, t, n_pairs, g_sz):
    lane = lax.broadcasted_iota(jnp.int32, (1, LANES), 1)
    slot_a = ((lane // HALF) % 2) == 0
    first_half = lane < HEAD_DIM
    zero = jnp.zeros((), BF16)
    jobs = [(i, p) for i in range(len(items)) for p in range(n_pairs)]
    scores = []
    for i, p in jobs:
        qa, k_all, _ = items[i]
        parts = []
        for g in range(g_sz):
            qb = qa[:, (p * g_sz + g) * LANES:(p * g_sz + g + 1) * LANES]
            parts.append(jnp.where(slot_a, qb, zero))
            parts.append(jnp.where(slot_a, zero, qb))
        qs = jnp.concatenate(parts, axis=0)
        scores.append(lax.dot_general(qs, k_all[:, p * LANES:(p + 1) * LANES], (((1,), (1,)), ((), ())),
                                      preferred_element_type=F32))
    probs = []
    for (i, p), s in zip(jobs, scores):
        if valid is not None:
            s = jnp.where(valid, s, -jnp.inf)
        sk = sink_ref[p][:, 0:1]
        m = jnp.maximum(jnp.max(s, axis=-1, keepdims=True), sk)
        e = jnp.exp(s - m)
        den = jnp.sum(e, axis=-1, keepdims=True) + jnp.exp(sk - m)
        probs.append((e / den).astype(BF16))
    outs = [[] for _ in items]
    for (i, p), pn in zip(jobs, probs):
        o = jnp.dot(pn, items[i][2][:, p * LANES:(p + 1) * LANES], preferred_element_type=F32)
        for g in range(g_sz):
            o_a = o[(2 * g) * t:(2 * g + 1) * t]
            o_b = o[(2 * g + 1) * t:(2 * g + 2) * t]
            outs[i].append(jnp.where(first_half, o_a, o_b))
    return outs


def _retain(items, dmat_ref, decin_ref, deckv_ref, dchunk_ref, gng, gnb, n_heads):
    nt = (((1,), (1,)), ((), ()))
    tn = (((0,), (0,)), ((), ()))
    jobs = [(i, h) for i in range(len(items)) for h in range(n_heads)]
    sl = lambda h: slice(h * RET_HEAD_DIM, (h + 1) * RET_HEAD_DIM)
    intra, inter, r_prevs = [], [], []
    for i, h in jobs:
        qr, kr, _, _, r_in_ref, _ = items[i]
        q = qr[:, sl(h)]
        r_prev = r_in_ref[h]
        r_prevs.append(r_prev)
        intra.append(lax.dot_general(q, kr[:, sl(h)].astype(BF16), nt, preferred_element_type=F32) * dmat_ref[h])
        inter.append(jnp.dot(q, r_prev.astype(BF16), preferred_element_type=F32) * decin_ref[h])
    for (i, h), r_prev in zip(jobs, r_prevs):
        _, kr, vr, _, _, r_out_ref = items[i]
        kd = (kr[:, sl(h)] * deckv_ref[h]).astype(BF16)
        kv = lax.dot_general(kd, vr[:, sl(h)], tn, preferred_element_type=F32)
        r_out_ref[h] = r_prev * dchunk_ref[h] + kv
    outs = [[] for _ in items]
    for (i, h), s, o_inter in zip(jobs, intra, inter):
        _, _, vr, gate, _, _ = items[i]
        o = jnp.dot(s.astype(BF16), vr[:, sl(h)], preferred_element_type=F32) + o_inter
        mu = jnp.mean(o, axis=-1, keepdims=True)
        dev = o - mu
        var = jnp.mean(dev * dev, axis=-1, keepdims=True)
        r = dev * lax.rsqrt(var + GN_EPS) * gng[:, sl(h)] + gnb[:, sl(h)]
        gt = gate[:, sl(h)]
        outs[i].append(gt * jax.nn.sigmoid(gt) * r)
    return outs


def _store_merged(merged_ref, ao, ro, d_attn):
    for b, blk in enumerate(ao):
        merged_ref[:, b * LANES:(b + 1) * LANES] = blk.astype(BF16)
    for h, blk in enumerate(ro):
        merged_ref[:, d_attn + h * RET_HEAD_DIM:d_attn + (h + 1) * RET_HEAD_DIM] = blk.astype(BF16)


def _mixer_prompt_kernel(nc, t, batch, n_pairs, g_sz, n_heads, d_attn,
                         dchunk_ref, qa_ref, *rest):
    n_win = WINDOW // t + 1
    k_refs, v_refs = rest[:n_win], rest[n_win:2 * n_win]
    (qr_ref, kr_ref, vr_ref, gate_ref, sink_ref, dmat_ref, decin_ref, deckv_ref, gng_ref, gnb_ref,
     merged_ref, rout_ref, r_ref) = rest[2 * n_win:]
    c = pl.program_id(0)

    @pl.when(c == 0)
    def _reset():
        r_ref[...] = jnp.zeros_like(r_ref)

    kidx = lax.broadcasted_iota(jnp.int32, (1, n_win * t), 1)
    valid = kidx >= (n_win - 1 - c) * t
    att_items = [(qa_ref[b],
                  jnp.concatenate([r[b].astype(BF16) for r in k_refs], axis=0),
                  jnp.concatenate([r[b].astype(BF16) for r in v_refs], axis=0)) for b in range(batch)]
    ret_items = [(qr_ref[b], kr_ref[b], vr_ref[b], gate_ref[b], r_ref.at[b], r_ref.at[b]) for b in range(batch)]
    ao = _attend(att_items, valid, sink_ref, t, n_pairs, g_sz)
    ro = _retain(ret_items, dmat_ref, decin_ref, deckv_ref, dchunk_ref, gng_ref[...], gnb_ref[...], n_heads)
    for b in range(batch):
        _store_merged(merged_ref.at[b], ao[b], ro[b], d_attn)

    @pl.when(c == nc - 1)
    def _emit_state():
        rout_ref[...] = r_ref[...]


def _mixer_sample_kernel(t, n_pairs, g_sz, n_heads, d_attn,
                         dchunk_ref, qa_ref, ka_ref, va_ref, qr_ref, kr_ref, vr_ref, gate_ref,
                         ck_ref, cv_ref, r0_ref,
                         sink_ref, dmat_ref, decin_ref, deckv_ref, gng_ref, gnb_ref,
                         merged_ref, rout_ref):
    k_all = jnp.concatenate([ck_ref[...], ka_ref[...].astype(BF16)], axis=0)
    v_all = jnp.concatenate([cv_ref[...], va_ref[...].astype(BF16)], axis=0)
    ao = _attend([(qa_ref[...], k_all, v_all)], None, sink_ref, t, n_pairs, g_sz)
    ro = _retain([(qr_ref[...], kr_ref[...], vr_ref[...], gate_ref[...], r0_ref, rout_ref)],
                 dmat_ref, decin_ref, deckv_ref, dchunk_ref, gng_ref[...], gnb_ref[...], n_heads)
    _store_merged(merged_ref, ao[0], ro[0], d_attn)


def _sink_table(sinks, sink_heads, t):
    per_row = jnp.repeat(sinks.astype(F32)[sink_heads], t, axis=1)
    return jnp.broadcast_to(per_row[:, :, None], per_row.shape + (LANES,))


def _mixer_prompt(proj_out, sinks, sink_heads, gng, gnb, batch, seq, n_kv, n_q, n_heads):
    qa, ka, va, qr, kr, vr, gate = proj_out
    t = CHUNK
    nc = seq // t
    d_attn, d_kv, d_ret = n_q * HEAD_DIM, n_kv * HEAD_DIM, n_heads * RET_HEAD_DIM
    n_pairs, g_sz = n_kv // 2, n_q // n_kv
    dmat, decin, deckv, dchunk = _decay_tables(n_heads, t)
    sink_tab = _sink_table(sinks, sink_heads, t)
    seq3 = lambda a: a.reshape(batch, seq, a.shape[-1])
    qa, ka, va, qr, kr, vr, gate = map(seq3, proj_out)
    row = lambda c, *_: (0, c, 0)
    back = lambda k: (lambda c, *_: (0, jnp.maximum(c - k, 0), 0))
    const3 = lambda c, *_: (0, 0, 0)
    const2 = lambda c, *_: (0, 0)
    n_win = WINDOW // t + 1
    kv_specs = [pl.BlockSpec((batch, t, d_kv), back(n_win - 1 - s)) for s in range(n_win)]
    state_shape = (batch, n_heads, RET_HEAD_DIM, RET_HEAD_DIM)
    grid_spec = pltpu.PrefetchScalarGridSpec(
        num_scalar_prefetch=1,
        grid=(nc,),
        in_specs=[
            pl.BlockSpec((batch, t, d_attn), row), *kv_specs, *kv_specs,
            pl.BlockSpec((batch, t, d_ret), row), pl.BlockSpec((batch, t, d_ret), row),
            pl.BlockSpec((batch, t, d_ret), row), pl.BlockSpec((batch, t, d_ret), row),
            pl.BlockSpec(sink_tab.shape, const3), pl.BlockSpec(dmat.shape, const3),
            pl.BlockSpec(decin.shape, const3), pl.BlockSpec(deckv.shape, const3),
            pl.BlockSpec((1, d_ret), const2), pl.BlockSpec((1, d_ret), const2),
        ],
        out_specs=[
            pl.BlockSpec((batch, t, d_attn + d_ret), row),
            pl.BlockSpec(state_shape, lambda c, *_: (0, 0, 0, 0)),
        ],
        scratch_shapes=[pltpu.VMEM(state_shape, F32)],
    )
    merged, r_fin = pl.pallas_call(
        functools.partial(_mixer_prompt_kernel, nc, t, batch, n_pairs, g_sz, n_heads, d_attn),
        grid_spec=grid_spec,
        out_shape=[jax.ShapeDtypeStruct((batch, seq, d_attn + d_ret), BF16),
                   jax.ShapeDtypeStruct(state_shape, F32)],
        compiler_params=_params("arbitrary"),
        name="mixer_prompt",
    )(dchunk, qa, *([ka] * n_win), *([va] * n_win), qr, kr, vr, gate, sink_tab, dmat, decin, deckv, gng, gnb)
    return merged.reshape(batch * seq, d_attn + d_ret), r_fin


def _mixer_sample(proj_out, ck, cv, r0, sinks, sink_heads, gng, gnb, row0, dbatch, t, n_kv, n_q, n_heads):
    qa, ka, va, qr, kr, vr, gate = proj_out
    d_attn, d_kv, d_ret = n_q * HEAD_DIM, n_kv * HEAD_DIM, n_heads * RET_HEAD_DIM
    n_pairs, g_sz = n_kv // 2, n_q // n_kv
    dmat, decin, deckv, dchunk = _decay_tables(n_heads, t)
    sink_tab = _sink_table(sinks, sink_heads, t)
    blk0 = row0 // t
    row = lambda b, *_: (blk0 + b, 0)
    const3 = lambda b, *_: (0, 0, 0)
    const2 = lambda b, *_: (0, 0)
    state_spec = pl.BlockSpec((None, n_heads, RET_HEAD_DIM, RET_HEAD_DIM), lambda b, *_: (b, 0, 0, 0))
    n_hist = ck.shape[1]
    grid_spec = pltpu.PrefetchScalarGridSpec(
        num_scalar_prefetch=1,
        grid=(dbatch,),
        in_specs=[
            pl.BlockSpec((t, d_attn), row), pl.BlockSpec((t, d_kv), row), pl.BlockSpec((t, d_kv), row),
            pl.BlockSpec((t, d_ret), row), pl.BlockSpec((t, d_ret), row), pl.BlockSpec((t, d_ret), row),
            pl.BlockSpec((t, d_ret), row),
            pl.BlockSpec((None, n_hist, d_kv), lambda b, *_: (b, 0, 0)),
            pl.BlockSpec((None, n_hist, d_kv), lambda b, *_: (b, 0, 0)),
            state_spec,
            pl.BlockSpec(sink_tab.shape, const3), pl.BlockSpec(dmat.shape, const3),
            pl.BlockSpec(decin.shape, const3), pl.BlockSpec(deckv.shape, const3),
            pl.BlockSpec((1, d_ret), const2), pl.BlockSpec((1, d_ret), const2),
        ],
        out_specs=[pl.BlockSpec((t, d_attn + d_ret), lambda b, *_: (b, 0)), state_spec],
    )
    return pl.pallas_call(
        functools.partial(_mixer_sample_kernel, t, n_pairs, g_sz, n_heads, d_attn),
        grid_spec=grid_spec,
        out_shape=[jax.ShapeDtypeStruct((dbatch * t, d_attn + d_ret), BF16),
                   jax.ShapeDtypeStruct(r0.shape, F32)],
        compiler_params=_params("parallel"),
        name="mixer_sample",
    )(dchunk, qa, ka, va, qr, kr, vr, gate, ck, cv, r0, sink_tab, dmat, decin, deckv, gng, gnb)


def _outproj_kernel(m_ref, w_ref, x_ref, o_ref):
    o_ref[...] = x_ref[...] + jnp.dot(m_ref[...], w_ref[...], preferred_element_type=F32)


def _outproj(merged, w, x):
    n, d = x.shape
    k = merged.shape[1]
    tm = _pick_tile(n, 1024)
    tn = _pick_tile(d, 1024, LANES)
    return pl.pallas_call(
        _outproj_kernel,
        grid=(n // tm, d // tn),
        in_specs=[pl.BlockSpec((tm, k), lambda i, j: (i, 0)),
                  pl.BlockSpec((k, tn), lambda i, j: (0, j)),
                  pl.BlockSpec((tm, tn), lambda i, j: (i, j))],
        out_specs=pl.BlockSpec((tm, tn), lambda i, j: (i, j)),
        out_shape=jax.ShapeDtypeStruct((n, d), F32),
        compiler_params=_params("parallel", "arbitrary"),
        name="outproj",
    )(merged, w, x)


def _swiglu_step(xn, wg, wu, wd):
    g = jnp.dot(xn, wg, preferred_element_type=F32)
    u = jnp.dot(xn, wu, preferred_element_type=F32)
    h = (g * jax.nn.sigmoid(g) * u).astype(BF16)
    return jnp.dot(h, wd, preferred_element_type=F32)


def _ffn_kernel(x_ref, ln_ref, wg_ref, wu_ref, wd_ref, o_ref, xn_ref):
    f = pl.program_id(1)

    @pl.when(f == 0)
    def _norm():
        x = x_ref[...]
        ms = jnp.mean(x * x, axis=-1, keepdims=True)
        xn_ref[...] = (x * lax.rsqrt(ms + RMS_EPS) * ln_ref[...]).astype(BF16)
        o_ref[...] = x

    o_ref[...] += _swiglu_step(xn_ref[...], wg_ref[...], wu_ref[...], wd_ref[...])


def _ffn(x, ln, wg, wu, wd):
    n, d = x.shape
    fp = wg.shape[1]
    tf = _pick_tile(fp, 512, LANES)
    tm = _pick_tile(n, 1024)
    return pl.pallas_call(
        _ffn_kernel,
        grid=(n // tm, fp // tf),
        in_specs=[pl.BlockSpec((tm, d), lambda i, f: (i, 0)),
                  pl.BlockSpec((1, d), lambda i, f: (0, 0)),
                  pl.BlockSpec((d, tf), lambda i, f: (0, f)),
                  pl.BlockSpec((d, tf), lambda i, f: (0, f)),
                  pl.BlockSpec((tf, d), lambda i, f: (f, 0))],
        out_specs=pl.BlockSpec((tm, d), lambda i, f: (i, 0)),
        out_shape=jax.ShapeDtypeStruct((n, d), F32),
        scratch_shapes=[pltpu.VMEM((tm, d), BF16)],
        compiler_params=_params("parallel", "arbitrary"),
        name="ffn",
    )(x, ln, wg, wu, wd)


def _router_kernel(n_exp, x_ref, ln_ref, wr_hi_ref, wr_lo_ref, hn_ref, idx_ref, gate_ref):
    x = x_ref[...]
    ms = jnp.mean(x * x, axis=-1, keepdims=True)
    hf = x * lax.rsqrt(ms + RMS_EPS) * ln_ref[...]
    hn = hf.astype(BF16)
    bits = lax.bitcast_convert_type(hn.astype(F32), jnp.uint32)
    half = bits.shape[1] // 2
    hn_ref[...] = (bits[:, :half] >> 16) | (bits[:, half:] & jnp.uint32(0xFFFF0000))
    h_lo = (hf - hn.astype(F32)).astype(BF16)
    logits = (jnp.dot(hn, wr_hi_ref[...], preferred_element_type=F32)
              + jnp.dot(h_lo, wr_hi_ref[...], preferred_element_type=F32)
              + jnp.dot(hn, wr_lo_ref[...], preferred_element_type=F32))
    lane = lax.broadcasted_iota(jnp.int32, logits.shape, 1)
    logits = jnp.where(lane < n_exp, logits, -jnp.inf)
    m1 = jnp.max(logits, axis=-1, keepdims=True)
    i1 = jnp.min(jnp.where(logits == m1, lane, LANES), axis=-1, keepdims=True)
    rest = jnp.where(lane == i1, -jnp.inf, logits)
    m2 = jnp.max(rest, axis=-1, keepdims=True)
    i2 = jnp.min(jnp.where(rest == m2, lane, LANES), axis=-1, keepdims=True)
    e2 = jnp.exp(m2 - m1)
    den = 1.0 + e2
    idx_ref[...] = jnp.where(lane == 0, i1, jnp.where(lane == 1, i2, 0))
    gate_ref[...] = jnp.where(lane == 0, 1.0 / den, jnp.where(lane == 1, e2 / den, 0.0))


def _router(x, ln, wr_hi, wr_lo, n_exp):
    n, d = x.shape
    tm = _pick_tile(n, 512)
    row = lambda i: (i, 0)
    return pl.pallas_call(
        functools.partial(_router_kernel, n_exp),
        grid=(n // tm,),
        in_specs=[pl.BlockSpec((tm, d), row), pl.BlockSpec((1, d), lambda i: (0, 0)),
                  pl.BlockSpec((d, LANES), lambda i: (0, 0)), pl.BlockSpec((d, LANES), lambda i: (0, 0))],
        out_specs=[pl.BlockSpec((tm, d // 2), row), pl.BlockSpec((tm, LANES), row), pl.BlockSpec((tm, LANES), row)],
        out_shape=[jax.ShapeDtypeStruct((n, d // 2), jnp.uint32), jax.ShapeDtypeStruct((n, LANES), jnp.int32),
                   jax.ShapeDtypeStruct((n, LANES), F32)],
        compiler_params=_params("parallel"),
        name="router",
    )(x, ln, wr_hi, wr_lo)


def _experts_kernel(tm, nf, n_split, te_ref, act_ref, tok_ref, nxt_ref, hn_hbm, *rest):
    wg_refs, wu_refs, wd_refs = rest[:n_split], rest[n_split:2 * n_split], rest[2 * n_split:3 * n_split]
    o_ref, rows_ref, x_ref, wgb_ref, wub_ref, wdb_ref, sem = rest[3 * n_split:]
    i = pl.program_id(0)
    f = pl.program_id(1)
    act = act_ref[i]
    slot = i % 2
    per_step = rows_ref.shape[1] // nf

    def start_row(tokens_ref, dst_slot, v):
        tok = tokens_ref[0, jnp.minimum(v, tm - 1)]
        pltpu.make_async_copy(hn_hbm.at[pl.ds(tok, 1)], rows_ref.at[dst_slot, pl.ds(v, 1)],
                              sem.at[dst_slot]).start()

    @pl.when(f == 0)
    def _init():
        o_ref[...] = jnp.zeros_like(o_ref)

    @pl.when((f == 0) & (i == 0) & (act > 0))
    def _first_tile_rows():
        def body(v, carry):
            start_row(tok_ref, 0, v)
            return carry

        lax.fori_loop(0, per_step * nf, body, 0, unroll=8)

    rows_started = jnp.where(i == 0, act, act_ref[jnp.maximum(i - 1, 0)]) > 0

    @pl.when((f == 0) & rows_started)
    def _rows_ready():
        pltpu.make_async_copy(hn_hbm.at[pl.ds(0, per_step * nf)], rows_ref.at[slot], sem.at[slot]).wait()

    @pl.when((f == 0) & (act > 0))
    def _unpack():
        words = rows_ref[slot, :tm, :]
        half = words.shape[1]
        x_ref[:, :half] = lax.bitcast_convert_type(words << 16, F32).astype(BF16)
        x_ref[:, half:] = lax.bitcast_convert_type(words & jnp.uint32(0xFFFF0000), F32).astype(BF16)

    def start_next_tile_rows():
        for r in range(per_step):
            start_row(nxt_ref, 1 - slot, f * per_step + r)

    def round_weights():
        for slabs, dst in ((wg_refs, wgb_ref), (wu_refs, wub_ref), (wd_refs, wdb_ref)):
            rows = dst.shape[0] // n_split
            for k, slab in enumerate(slabs):
                dst[k * rows:(k + 1) * rows, :] = slab[...].astype(BF16)

    @pl.when(act == 1)
    def _step():
        start_next_tile_rows()
        round_weights()
        o_ref[...] += _swiglu_step(x_ref[...], wgb_ref[...], wub_ref[...], wdb_ref[...])

    @pl.when(act == 2)
    def _half_step():
        start_next_tile_rows()
        round_weights()
        o_ref[:tm // 2, :] += _swiglu_step(x_ref[:tm // 2, :], wgb_ref[...], wub_ref[...], wdb_ref[...])


def _experts(hn32, src_tok, wg, wu, wd, tile_expert, tile_active, tm):
    n, half = hn32.shape
    d = 2 * half
    assert n >= tm
    n_tiles = src_tok.shape[0] // tm
    n_exp, _, fe = wg.shape
    tf = _pick_tile(fe, 512, LANES)
    nf = fe // tf
    fidx = lambda i, f, te, act: jnp.where(act[i] > 0, f, nf - 1)
    n_split = EXPERT_WEIGHT_STREAMS
    in_slab = lambda k: (lambda i, f, te, act: (te[i], k, fidx(i, f, te, act)))
    out_slab = lambda k: (lambda i, f, te, act: (te[i], n_split * fidx(i, f, te, act) + k, 0))
    in_specs = [pl.BlockSpec((None, d // n_split, tf), in_slab(k)) for k in range(n_split)]
    out_specs_w = [pl.BlockSpec((None, tf // n_split, d), out_slab(k)) for k in range(n_split)]
    buf_rows = next(r for r in range(tm, tm + 8 * nf + 1) if r % nf == 0 and r % 8 == 0)
    assert n >= buf_rows
    tokens = src_tok.reshape(n_tiles, 1, tm)
    tok_spec = lambda step: pl.BlockSpec((None, 1, tm), lambda i, f, te, act: (jnp.minimum(i + step, n_tiles - 1), 0, 0),
                                         memory_space=pltpu.SMEM)
    grid_spec = pltpu.PrefetchScalarGridSpec(
        num_scalar_prefetch=2,
        grid=(n_tiles, nf),
        in_specs=[tok_spec(0), tok_spec(1), pl.BlockSpec(memory_space=pl.ANY),
                  *in_specs, *in_specs, *out_specs_w],
        out_specs=pl.BlockSpec((tm, d), lambda i, f, te, act: (i, 0), pipeline_mode=pl.Buffered(1)),
        scratch_shapes=[pltpu.VMEM((2, buf_rows, half), jnp.uint32), pltpu.VMEM((tm, d), BF16),
                        pltpu.VMEM((d, tf), BF16), pltpu.VMEM((d, tf), BF16), pltpu.VMEM((tf, d), BF16),
                        pltpu.SemaphoreType.DMA((2,))],
    )
    return pl.pallas_call(
        functools.partial(_experts_kernel, tm, nf, n_split),
        grid_spec=grid_spec,
        out_shape=jax.ShapeDtypeStruct((n_tiles * tm, d), F32),
        compiler_params=_params("arbitrary", "arbitrary"),
        name="experts",
    )(tile_expert, tile_active, tokens, tokens, hn32,
      *([wg] * n_split), *([wu] * n_split), *([wd] * n_split))


def _combine_kernel(x_ref, y0_ref, y1_ref, g_ref, o_ref):
    g = g_ref[...]
    o_ref[...] = x_ref[...] + (g[:, 0:1] * y0_ref[...] + g[:, 1:2] * y1_ref[...])


def _combine(x, y0, y1, gate, row0):
    n, d = x.shape
    tm = _pick_tile(n, 256)
    assert row0 % tm == 0
    blk0 = row0 // tm
    own = lambda i: (i, 0)
    shifted = lambda i: (blk0 + i, 0)
    return pl.pallas_call(
        _combine_kernel,
        grid=(n // tm,),
        in_specs=[pl.BlockSpec((tm, d), own), pl.BlockSpec((tm, d), shifted), pl.BlockSpec((tm, d), shifted),
                  pl.BlockSpec((tm, LANES), own)],
        out_specs=pl.BlockSpec((tm, d), own),
        out_shape=jax.ShapeDtypeStruct((n, d), F32),
        compiler_params=_params("parallel"),
        name="combine",
    )(x, y0, y1, gate)


def _moe(x_p, x_s, ln, w_router, wg, wu, wd):
    n_p, d = x_p.shape
    n = n_p + x_s.shape[0]
    n_exp = w_router.shape[1]
    wr = jnp.pad(w_router.astype(F32), ((0, 0), (0, LANES - n_exp)))
    wr_hi = wr.astype(BF16)
    wr_lo = (wr - wr_hi.astype(F32)).astype(BF16)
    hn_p, ridx_p, rgate_p = _router(x_p, ln, wr_hi, wr_lo, n_exp)
    hn_s, ridx_s, rgate_s = _router(x_s, ln, wr_hi, wr_lo, n_exp)
    hn = jnp.concatenate([hn_p, hn_s], axis=0)
    tm = 1024 if TOP_K * n >= 8192 else 64
    n_asg = TOP_K * n
    n_tiles = -(-(n_asg + n_exp * (tm - 1)) // tm) + 1
    flat_e = jnp.concatenate([ridx_p[:, :TOP_K], ridx_s[:, :TOP_K]], axis=0).reshape(-1)
    onehot = (flat_e[:, None] == jnp.arange(n_exp, dtype=jnp.int32)[None, :]).astype(jnp.int32)
    csum = jnp.cumsum(onehot, axis=0)
    counts = csum[-1]
    rank = jnp.take_along_axis(csum, flat_e[:, None], axis=1)[:, 0] - 1
    padded = ((counts + tm - 1) // tm) * tm
    ends = jnp.cumsum(padded)
    pos = (ends - padded)[flat_e] + rank
    src_tok = jnp.zeros((n_tiles * tm,), jnp.int32).at[pos].set(
        jnp.arange(n_asg, dtype=jnp.int32) // TOP_K, unique_indices=True, mode="promise_in_bounds")
    tile_start = jnp.arange(n_tiles, dtype=jnp.int32) * tm
    te = jnp.sum((tile_start[:, None] >= ends[None, :]).astype(jnp.int32), axis=1)
    tile_expert = jnp.minimum(te, n_exp - 1)
    rows_in_tile = (ends - padded + counts)[tile_expert] - tile_start
    tile_active = jnp.where(te >= n_exp, 0, jnp.where(rows_in_tile <= tm // 2, 2, 1)).astype(jnp.int32)
    ys = _experts(hn, src_tok, wg, wu, wd, tile_expert, tile_active, tm)
    pos2 = pos.reshape(n, TOP_K)
    y0 = jnp.take(ys, pos2[:, 0], axis=0, mode="clip")
    y1 = jnp.take(ys, pos2[:, 1], axis=0, mode="clip")
    return _combine(x_p, y0, y1, rgate_p, 0), _combine(x_s, y0, y1, rgate_s, n_p)


def kernel(x_prompt, x_sample, cache_attn_k, cache_attn_v, state_ret, ln_mix, w_in, q_norm, k_norm, attn_sinks,
           ret_gn_g, ret_gn_b, w_out, ln_ffn, ffn_w_gate, ffn_w_up, ffn_w_down, moe_router, moe_w_gate, moe_w_up,
           moe_w_down):
    batch, seq, d = x_prompt.shape
    dbatch, t_s, _ = x_sample.shape
    depth = ln_mix.shape[0]
    n_kv = cache_attn_k.shape[3]
    n_heads = state_ret.shape[2]
    d_kv = n_kv * HEAD_DIM
    d_ret = n_heads * RET_HEAD_DIM
    d_attn = w_out.shape[1] - d_ret
    n_q = d_attn // HEAD_DIM
    assert cache_attn_k.shape[4] == HEAD_DIM and state_ret.shape[3] == RET_HEAD_DIM and n_kv % 2 == 0
    assert seq % CHUNK == 0 and WINDOW % CHUNK == 0 and seq >= WINDOW and cache_attn_k.shape[2] == WINDOW
    assert w_in.shape[2] == d_attn + 2 * d_kv + 4 * d_ret
    n_p, n_s = batch * seq, dbatch * t_s
    keep = min(WINDOW, seq)

    perm_q, perm_k, perm_ao, sink_heads = _layout_perms(n_q, n_kv)
    inv_k = np.argsort(perm_k)
    qk_perm = np.concatenate([perm_q, d_attn + perm_k])

    tabs_p = _rope_tables(np.arange(seq))
    tabs_s = _rope_tables(PAST_LEN + np.tile(np.arange(t_s), dbatch))
    lane = np.arange(LANES)
    m128 = jnp.asarray(((lane[:, None] // HALF) % 2) == ((lane[None, :] // HALF) % 2), BF16)
    head_gain = lambda g: jnp.tile(g.astype(F32).reshape(2, HALF), (1, 2)).reshape(1, LANES)

    x_p, x_s = x_prompt.reshape(n_p, d), x_sample.reshape(n_s, d)
    nk_p, nv_p, nr_p, nk_s, nv_s, nr_s = [], [], [], [], [], []
    for l in range(depth):
        w_in_l = jnp.concatenate([w_in[l][:, :d_attn + d_kv][:, qk_perm], w_in[l][:, d_attn + d_kv:]], axis=1)
        w_in_l = w_in_l.astype(BF16)
        w_out_l = jnp.concatenate([w_out[l][:d_attn][perm_ao], w_out[l][d_attn:]], axis=0).astype(BF16)
        ln, gq, gk = ln_mix[l][None, :], head_gain(q_norm[l]), head_gain(k_norm[l])
        gng, gnb = ret_gn_g[l][None, :], ret_gn_b[l][None, :]
        proj_p = _proj(x_p, ln, w_in_l, m128, gq, gk, *tabs_p, d_attn, d_kv, d_ret, 512)
        proj_s = _proj(x_s, ln, w_in_l, m128, gq, gk, *tabs_s, d_attn, d_kv, d_ret, 256)
        merged_p, r_p = _mixer_prompt(proj_p, attn_sinks[l], sink_heads, gng, gnb, batch, seq, n_kv, n_q, n_heads)
        ck = cache_attn_k[l].reshape(dbatch, WINDOW, d_kv)[:, :, perm_k].astype(BF16)
        cv = cache_attn_v[l].reshape(dbatch, WINDOW, d_kv).astype(BF16)
        merged_s, r_s = _mixer_sample(proj_s, ck, cv, state_ret[l], attn_sinks[l], sink_heads, gng, gnb,
                                      0, dbatch, t_s, n_kv, n_q, n_heads)
        x_p = _outproj(merged_p, w_out_l, x_p)
        x_s = _outproj(merged_s, w_out_l, x_s)

        ka_last = proj_p[1].reshape(batch, seq, d_kv)[:, seq - keep:]
        nk_p.append(ka_last[:, :, inv_k].reshape(batch, keep, n_kv, HEAD_DIM))
        nv_p.append(proj_p[2].reshape(batch, seq, d_kv)[:, seq - keep:].reshape(batch, keep, n_kv, HEAD_DIM))
        nk_s.append(proj_s[1][:, inv_k].reshape(dbatch, t_s, n_kv, HEAD_DIM))
        nv_s.append(proj_s[2].reshape(dbatch, t_s, n_kv, HEAD_DIM))
        nr_p.append(r_p)
        nr_s.append(r_s)

        i = l // 2
        lf = ln_ffn[l][None, :]
        if l % 2 == 0:
            fpad = (-ffn_w_gate.shape[2]) % COL_TILE
            wg = jnp.pad(ffn_w_gate[i].astype(BF16), ((0, 0), (0, fpad)))
            wu = jnp.pad(ffn_w_up[i].astype(BF16), ((0, 0), (0, fpad)))
            wd = jnp.pad(ffn_w_down[i].astype(BF16), ((0, fpad), (0, 0)))
            x_p, x_s = _ffn(x_p, lf, wg, wu, wd), _ffn(x_s, lf, wg, wu, wd)
        else:
            x_p, x_s = _moe(x_p, x_s, lf, moe_router[i], moe_w_gate[i], moe_w_up[i], moe_w_down[i])

    return (x_p.reshape(batch, seq, d), x_s.reshape(dbatch, t_s, d),
            jnp.stack(nk_p), jnp.stack(nv_p), jnp.stack(nr_p), jnp.stack(nk_s), jnp.stack(nv_s), jnp.stack(nr_s))
```

```python
import functools

import numpy as np
import jax
import jax.numpy as jnp
from jax import lax
from jax.experimental import pallas as pl
from jax.experimental.pallas import tpu as pltpu

CHUNK = 64
WINDOW = 128
HEAD_DIM = 64
RET_HEAD_DIM = 128
PAST_LEN = 2048
ROPE_THETA = 10000.0
TOP_K = 2
RMS_EPS = 1e-6
GN_EPS = 1e-5

LANES = 128
HALF = HEAD_DIM // 2
VMEM_LIMIT = 56 * 1024 * 1024
COL_TILE = 512
EXPERT_WEIGHT_STREAMS = 1
F32 = jnp.float32
BF16 = jnp.bfloat16


def _pick_tile(n, target, mult=16):
    best = None
    for t in range(mult, min(n, target) + 1, mult):
        if n % t == 0:
            best = t
    assert best is not None, (n, target)
    return best


def _params(*sem):
    return pltpu.CompilerParams(dimension_semantics=sem, vmem_limit_bytes=VMEM_LIMIT)


def _layout_perms(n_q, n_kv):
    g_sz = n_q // n_kv
    n_pairs = n_kv // 2
    perm_k = np.zeros(n_kv * HEAD_DIM, np.int32)
    perm_q = np.zeros(n_q * HEAD_DIM, np.int32)
    perm_ao = np.zeros(n_q * HEAD_DIM, np.int32)
    sink_heads = np.zeros((n_pairs, 2 * g_sz), np.int32)
    for p in range(n_pairs):
        for hf in range(2):
            for s in range(2):
                for r in range(HALF):
                    perm_k[p * LANES + hf * 64 + s * HALF + r] = (2 * p + s) * HEAD_DIM + hf * HALF + r
        for g in range(g_sz):
            bq = p * g_sz + g
            for s in range(2):
                head = g_sz * (2 * p + s) + g
                sink_heads[p, 2 * g + s] = head
                for hf in range(2):
                    for r in range(HALF):
                        perm_q[bq * LANES + hf * 64 + s * HALF + r] = head * HEAD_DIM + hf * HALF + r
                for d in range(HEAD_DIM):
                    perm_ao[bq * LANES + s * HEAD_DIM + d] = head * HEAD_DIM + d
    return perm_q, perm_k, perm_ao, sink_heads


def _rope_tables(pos):
    f32 = np.float32
    pf = np.asarray(pos).astype(f32)[:, None]
    inv_a = (f32(ROPE_THETA) ** (-np.arange(0, HEAD_DIM, 2, dtype=f32) / f32(HEAD_DIM))).astype(f32)
    ang = pf * inv_a[None, :]
    c, s = np.cos(ang), np.sin(ang)
    ca = np.concatenate([c, c, c, c], axis=-1)
    sa = np.concatenate([-s, -s, s, s], axis=-1)
    inv_r = (f32(10000.0) ** (-np.linspace(0.0, 1.0, RET_HEAD_DIM // 2, dtype=f32))).astype(f32)
    ang = pf * inv_r[None, :]
    c, s = np.cos(ang), np.sin(ang)
    cr = np.concatenate([c, c], axis=-1)
    sr = np.concatenate([-s, s], axis=-1)
    return tuple(jnp.asarray(t, F32) for t in (ca, sa, cr, sr))


def _decay_tables(n_heads, chunk):
    f32 = np.float32
    log_g = np.log1p(-np.exp2(f32(-5.0) - np.arange(n_heads, dtype=f32))).astype(f32)
    idx = np.arange(chunk, dtype=f32)
    dmat = np.exp(log_g[:, None, None] * np.abs(idx[:, None] - idx[None, :]))
    dec_in = np.exp(log_g[None, :] * (idx[:, None] + f32(1.0)))
    dec_kv = np.exp(log_g[None, :] * (f32(chunk - 1.0) - idx[:, None]))
    dec_chunk = np.exp(log_g * f32(chunk))
    bc = lambda t: np.broadcast_to(t.T[:, :, None], (n_heads, chunk, LANES))
    return tuple(jnp.asarray(t, F32) for t in (dmat, bc(dec_in), bc(dec_kv), dec_chunk))


def _rope128(y, c, s):
    return y * c + pltpu.roll(y, 64, 1) * s


def _proj_kernel(nq, nr, x_ref, ln_ref, w_ref, m_ref, gq_ref, gk_ref, ca_ref, sa_ref, cr_ref, sr_ref,
                 qa_ref, ka_ref, va_ref, qr_ref, kr_ref, vr_ref, gate_ref, xn_ref):
    x = x_ref[...]
    ms = jnp.mean(x * x, axis=-1, keepdims=True)
    xn_ref[...] = (x * lax.rsqrt(ms + RMS_EPS) * ln_ref[...]).astype(BF16)
    n_sub = COL_TILE // LANES

    def col_tile(j):
        return jnp.dot(xn_ref[...], w_ref[:, j * COL_TILE:(j + 1) * COL_TILE], preferred_element_type=F32)

    def lanes(j, c):
        return slice(j * COL_TILE + c * LANES, j * COL_TILE + (c + 1) * LANES)

    def head_norm_rope(yc, g):
        y2 = yc * yc
        hi = y2.astype(BF16)
        lo = (y2 - hi.astype(F32)).astype(BF16)
        ss = (jnp.dot(hi, m_ref[...], preferred_element_type=F32)
              + jnp.dot(lo, m_ref[...], preferred_element_type=F32))
        yn = yc * lax.rsqrt(ss * (1.0 / HEAD_DIM) + RMS_EPS) * g
        return _rope128(yn, ca_ref[...], sa_ref[...])

    for j in range(nq):
        y = col_tile(j)
        for c in range(n_sub):
            qa_ref[:, lanes(j, c)] = (head_norm_rope(y[:, lanes(0, c)], gq_ref[...])
                                      * (HEAD_DIM ** -0.5)).astype(BF16)

    y = col_tile(nq)
    half = COL_TILE // 2
    for c in range(half // LANES):
        ka_ref[:, lanes(0, c)] = head_norm_rope(y[:, lanes(0, c)], gk_ref[...])
    va_ref[...] = y[:, half:]

    for j in range(nr):
        y = col_tile(nq + 1 + j)
        for c in range(n_sub):
            qr_ref[:, lanes(j, c)] = _rope128(y[:, lanes(0, c)], cr_ref[...], sr_ref[...]).astype(BF16)
    for j in range(nr):
        y = col_tile(nq + 1 + nr + j)
        for c in range(n_sub):
            kr_ref[:, lanes(j, c)] = (_rope128(y[:, lanes(0, c)], cr_ref[...], sr_ref[...])
                                      * (RET_HEAD_DIM ** -0.5))
    for j in range(nr):
        vr_ref[:, j * COL_TILE:(j + 1) * COL_TILE] = col_tile(nq + 1 + 2 * nr + j).astype(BF16)
    for j in range(nr):
        gate_ref[:, j * COL_TILE:(j + 1) * COL_TILE] = col_tile(nq + 1 + 3 * nr + j)


def _proj(x, ln, w, m128, gq, gk, ca, sa, cr, sr, d_attn, d_kv, d_ret, tm_target):
    n, d = x.shape
    assert d_attn % COL_TILE == 0 and 2 * d_kv == COL_TILE and d_ret % COL_TILE == 0
    nq, nr = d_attn // COL_TILE, d_ret // COL_TILE
    n_col = nq + 1 + 4 * nr
    tab_rows = ca.shape[0]
    assert n % tab_rows == 0
    tm = _pick_tile(tab_rows, tm_target)
    row = lambda i: (i, 0)
    tab = pl.BlockSpec((tm, LANES), lambda i: (i % (tab_rows // tm), 0))
    vec = pl.BlockSpec((1, LANES), lambda i: (0, 0))
    return pl.pallas_call(
        functools.partial(_proj_kernel, nq, nr),
        grid=(n // tm,),
        in_specs=[
            pl.BlockSpec((tm, d), row),
            pl.BlockSpec((1, d), lambda i: (0, 0)),
            pl.BlockSpec((d, n_col * COL_TILE), lambda i: (0, 0), pipeline_mode=pl.Buffered(1)),
            pl.BlockSpec((LANES, LANES), lambda i: (0, 0)),
            vec, vec, tab, tab, tab, tab,
        ],
        out_specs=[
            pl.BlockSpec((tm, d_attn), row),
            pl.BlockSpec((tm, d_kv), row),
            pl.BlockSpec((tm, d_kv), row),
            pl.BlockSpec((tm, d_ret), row),
            pl.BlockSpec((tm, d_ret), row),
            pl.BlockSpec((tm, d_ret), row),
            pl.BlockSpec((tm, d_ret), row),
        ],
        out_shape=[
            jax.ShapeDtypeStruct((n, d_attn), BF16),
            jax.ShapeDtypeStruct((n, d_kv), F32),
            jax.ShapeDtypeStruct((n, d_kv), F32),
            jax.ShapeDtypeStruct((n, d_ret), BF16),
            jax.ShapeDtypeStruct((n, d_ret), F32),
            jax.ShapeDtypeStruct((n, d_ret), BF16),
            jax.ShapeDtypeStruct((n, d_ret), F32),
        ],
        scratch_shapes=[pltpu.VMEM((tm, d), BF16)],
        compiler_params=_params("parallel"),
        name="proj",
    )(x, ln, w, m128, gq, gk, ca, sa, cr, sr)


def _attend(items, valid, sink_ref, t, n_pairs, g_sz):
    lane = lax.broadcasted_iota(jnp.int32, (1, LANES), 1)
    slot_a = ((lane // HALF) % 2) == 0
    first_half = lane < HEAD_DIM
    zero = jnp.zeros((), BF16)
    jobs = [(i, p) for i in range(len(items)) for p in range(n_pairs)]
    scores = []
    for i, p in jobs:
        qa, k_all, _ = items[i]
        parts = []
        for g in range(g_sz):
            qb = qa[:, (p * g_sz + g) * LANES:(p * g_sz + g + 1) * LANES]
            parts.append(jnp.where(slot_a, qb, zero))
            parts.append(jnp.where(slot_a, zero, qb))
        qs = jnp.concatenate(parts, axis=0)
        scores.append(lax.dot_general(qs, k_all[:, p * LANES:(p + 1) * LANES], (((1,), (1,)), ((), ())),
                                      preferred_element_type=F32))
    probs = []
    for (i, p), s in zip(jobs, scores):
        if valid is not None:
            s = jnp.where(valid, s, -jnp.inf)
        sk = sink_ref[p][:, 0:1]
        m = jnp.maximum(jnp.max(s, axis=-1, keepdims=True), sk)
        e = jnp.exp(s - m)
        den = jnp.sum(e, axis=-1, keepdims=True) + jnp.exp(sk - m)
        probs.append((e / den).astype(BF16))
    outs = [[] for _ in items]
    for (i, p), pn in zip(jobs, probs):
        o = jnp.dot(pn, items[i][2][:, p * LANES:(p + 1) * LANES], preferred_element_type=F32)
        for g in range(g_sz):
            o_a = o[(2 * g) * t:(2 * g + 1) * t]
            o_b = o[(2 * g + 1) * t:(2 * g + 2) * t]
            outs[i].append(jnp.where(first_half, o_a, o_b))
    return outs


def _retain(items, dmat_ref, decin_ref, deckv_ref, dchunk_ref, gng, gnb, n_heads):
    nt = (((1,), (1,)), ((), ()))
    tn = (((0,), (0,)), ((), ()))
    jobs = [(i, h) for i in range(len(items)) for h in range(n_heads)]
    sl = lambda h: slice(h * RET_HEAD_DIM, (h + 1) * RET_HEAD_DIM)
    intra, inter, r_prevs = [], [], []
    for i, h in jobs:
        qr, kr, _, _, r_in_ref, _ = items[i]
        q = qr[:, sl(h)]
        r_prev = r_in_ref[h]
        r_prevs.append(r_prev)
        intra.append(lax.dot_general(q, kr[:, sl(h)].astype(BF16), nt, preferred_element_type=F32) * dmat_ref[h])
        inter.append(jnp.dot(q, r_prev.astype(BF16), preferred_element_type=F32) * decin_ref[h])
    for (i, h), r_prev in zip(jobs, r_prevs):
        _, kr, vr, _, _, r_out_ref = items[i]
        kd = (kr[:, sl(h)] * deckv_ref[h]).astype(BF16)
        kv = lax.dot_general(kd, vr[:, sl(h)], tn, preferred_element_type=F32)
        r_out_ref[h] = r_prev * dchunk_ref[h] + kv
    outs = [[] for _ in items]
    for (i, h), s, o_inter in zip(jobs, intra, inter):
        _, _, vr, gate, _, _ = items[i]
        o = jnp.dot(s.astype(BF16), vr[:, sl(h)], preferred_element_type=F32) + o_inter
        mu = jnp.mean(o, axis=-1, keepdims=True)
        dev = o - mu
        var = jnp.mean(dev * dev, axis=-1, keepdims=True)
        r = dev * lax.rsqrt(var + GN_EPS) * gng[:, sl(h)] + gnb[:, sl(h)]
        gt = gate[:, sl(h)]
        outs[i].append(gt * jax.nn.sigmoid(gt) * r)
    return outs


def _store_merged(merged_ref, ao, ro, d_attn):
    for b, blk in enumerate(ao):
        merged_ref[:, b * LANES:(b + 1) * LANES] = blk.astype(BF16)
    for h, blk in enumerate(ro):
        merged_ref[:, d_attn + h * RET_HEAD_DIM:d_attn + (h + 1) * RET_HEAD_DIM] = blk.astype(BF16)


def _mixer_prompt_kernel(nc, t, batch, n_pairs, g_sz, n_heads, d_attn,
                         dchunk_ref, qa_ref, *rest):
    n_win = WINDOW // t + 1
    k_refs, v_refs = rest[:n_win], rest[n_win:2 * n_win]
    (qr_ref, kr_ref, vr_ref, gate_ref, sink_ref, dmat_ref, decin_ref, deckv_ref, gng_ref, gnb_ref,
     merged_ref, rout_ref, r_ref) = rest[2 * n_win:]
    c = pl.program_id(0)

    @pl.when(c == 0)
    def _reset():
        r_ref[...] = jnp.zeros_like(r_ref)

    kidx = lax.broadcasted_iota(jnp.int32, (1, n_win * t), 1)
    valid = kidx >= (n_win - 1 - c) * t
    att_items = [(qa_ref[b],
                  jnp.concatenate([r[b].astype(BF16) for r in k_refs], axis=0),
                  jnp.concatenate([r[b].astype(BF16) for r in v_refs], axis=0)) for b in range(batch)]
    ret_items = [(qr_ref[b], kr_ref[b], vr_ref[b], gate_ref[b], r_ref.at[b], r_ref.at[b]) for b in range(batch)]
    ao = _attend(att_items, valid, sink_ref, t, n_pairs, g_sz)
    ro = _retain(ret_items, dmat_ref, decin_ref, deckv_ref, dchunk_ref, gng_ref[...], gnb_ref[...], n_heads)
    for b in range(batch):
        _store_merged(merged_ref.at[b], ao[b], ro[b], d_attn)

    @pl.when(c == nc - 1)
    def _emit_state():
        rout_ref[...] = r_ref[...]


def _mixer_sample_kernel(t, n_pairs, g_sz, n_heads, d_attn,
                         dchunk_ref, qa_ref, ka_ref, va_ref, qr_ref, kr_ref, vr_ref, gate_ref,
                         ck_ref, cv_ref, r0_ref,
                         sink_ref, dmat_ref, decin_ref, deckv_ref, gng_ref, gnb_ref,
                         merged_ref, rout_ref):
    k_all = jnp.concatenate([ck_ref[...], ka_ref[...].astype(BF16)], axis=0)
    v_all = jnp.concatenate([cv_ref[...], va_ref[...].astype(BF16)], axis=0)
    ao = _attend([(qa_ref[...], k_all, v_all)], None, sink_ref, t, n_pairs, g_sz)
    ro = _retain([(qr_ref[...], kr_ref[...], vr_ref[...], gate_ref[...], r0_ref, rout_ref)],
                 dmat_ref, decin_ref, deckv_ref, dchunk_ref, gng_ref[...], gnb_ref[...], n_heads)
    _store_merged(merged_ref, ao[0], ro[0], d_attn)


def _sink_table(sinks, sink_heads, t):
    per_row = jnp.repeat(sinks.astype(F32)[sink_heads], t, axis=1)
    return jnp.broadcast_to(per_row[:, :, None], per_row.shape + (LANES,))


def _mixer_prompt(proj_out, sinks, sink_heads, gng, gnb, batch, seq, n_kv, n_q, n_heads):
    qa, ka, va, qr, kr, vr, gate = proj_out
    t = CHUNK
    nc = seq // t
    d_attn, d_kv, d_ret = n_q * HEAD_DIM, n_kv * HEAD_DIM, n_heads * RET_HEAD_DIM
    n_pairs, g_sz = n_kv // 2, n_q // n_kv
    dmat, decin, deckv, dchunk = _decay_tables(n_heads, t)
    sink_tab = _sink_table(sinks, sink_heads, t)
    seq3 = lambda a: a.reshape(batch, seq, a.shape[-1])
    qa, ka, va, qr, kr, vr, gate = map(seq3, proj_out)
    row = lambda c, *_: (0, c, 0)
    back = lambda k: (lambda c, *_: (0, jnp.maximum(c - k, 0), 0))
    const3 = lambda c, *_: (0, 0, 0)
    const2 = lambda c, *_: (0, 0)
    n_win = WINDOW // t + 1
    kv_specs = [pl.BlockSpec((batch, t, d_kv), back(n_win - 1 - s)) for s in range(n_win)]
    state_shape = (batch, n_heads, RET_HEAD_DIM, RET_HEAD_DIM)
    grid_spec = pltpu.PrefetchScalarGridSpec(
        num_scalar_prefetch=1,
        grid=(nc,),
        in_specs=[
            pl.BlockSpec((batch, t, d_attn), row), *kv_specs, *kv_specs,
            pl.BlockSpec((batch, t, d_ret), row), pl.BlockSpec((batch, t, d_ret), row),
            pl.BlockSpec((batch, t, d_ret), row), pl.BlockSpec((batch, t, d_ret), row),
            pl.BlockSpec(sink_tab.shape, const3), pl.BlockSpec(dmat.shape, const3),
            pl.BlockSpec(decin.shape, const3), pl.BlockSpec(deckv.shape, const3),
            pl.BlockSpec((1, d_ret), const2), pl.BlockSpec((1, d_ret), const2),
        ],
        out_specs=[
            pl.BlockSpec((batch, t, d_attn + d_ret), row),
            pl.BlockSpec(state_shape, lambda c, *_: (0, 0, 0, 0)),
        ],
        scratch_shapes=[pltpu.VMEM(state_shape, F32)],
    )
    merged, r_fin = pl.pallas_call(
        functools.partial(_mixer_prompt_kernel, nc, t, batch, n_pairs, g_sz, n_heads, d_attn),
        grid_spec=grid_spec,
        out_shape=[jax.ShapeDtypeStruct((batch, seq, d_attn + d_ret), BF16),
                   jax.ShapeDtypeStruct(state_shape, F32)],
        compiler_params=_params("arbitrary"),
        name="mixer_prompt",
    )(dchunk, qa, *([ka] * n_win), *([va] * n_win), qr, kr, vr, gate, sink_tab, dmat, decin, deckv, gng, gnb)
    return merged.reshape(batch * seq, d_attn + d_ret), r_fin


def _mixer_sample(proj_out, ck, cv, r0, sinks, sink_heads, gng, gnb, row0, dbatch, t, n_kv, n_q, n_heads):
    qa, ka, va, qr, kr, vr, gate = proj_out
    d_attn, d_kv, d_ret = n_q * HEAD_DIM, n_kv * HEAD_DIM, n_heads * RET_HEAD_DIM
    n_pairs, g_sz = n_kv // 2, n_q // n_kv
    dmat, decin, deckv, dchunk = _decay_tables(n_heads, t)
    sink_tab = _sink_table(sinks, sink_heads, t)
    blk0 = row0 // t
    row = lambda b, *_: (blk0 + b, 0)
    const3 = lambda b, *_: (0, 0, 0)
    const2 = lambda b, *_: (0, 0)
    state_spec = pl.BlockSpec((None, n_heads, RET_HEAD_DIM, RET_HEAD_DIM), lambda b, *_: (b, 0, 0, 0))
    n_hist = ck.shape[1]
    grid_spec = pltpu.PrefetchScalarGridSpec(
        num_scalar_prefetch=1,
        grid=(dbatch,),
        in_specs=[
            pl.BlockSpec((t, d_attn), row), pl.BlockSpec((t, d_kv), row), pl.BlockSpec((t, d_kv), row),
            pl.BlockSpec((t, d_ret), row), pl.BlockSpec((t, d_ret), row), pl.BlockSpec((t, d_ret), row),
            pl.BlockSpec((t, d_ret), row),
            pl.BlockSpec((None, n_hist, d_kv), lambda b, *_: (b, 0, 0)),
            pl.BlockSpec((None, n_hist, d_kv), lambda b, *_: (b, 0, 0)),
            state_spec,
            pl.BlockSpec(sink_tab.shape, const3), pl.BlockSpec(dmat.shape, const3),
            pl.BlockSpec(decin.shape, const3), pl.BlockSpec(deckv.shape, const3),
            pl.BlockSpec((1, d_ret), const2), pl.BlockSpec((1, d_ret), const2),
        ],
        out_specs=[pl.BlockSpec((t, d_attn + d_ret), lambda b, *_: (b, 0)), state_spec],
    )
    return pl.pallas_call(
        functools.partial(_mixer_sample_kernel, t, n_pairs, g_sz, n_heads, d_attn),
        grid_spec=grid_spec,
        out_shape=[jax.ShapeDtypeStruct((dbatch * t, d_attn + d_ret), BF16),
                   jax.ShapeDtypeStruct(r0.shape, F32)],
        compiler_params=_params("parallel"),
        name="mixer_sample",
    )(dchunk, qa, ka, va, qr, kr, vr, gate, ck, cv, r0, sink_tab, dmat, decin, deckv, gng, gnb)


def _outproj_kernel(m_ref, w_ref, x_ref, o_ref):
    o_ref[...] = x_ref[...] + jnp.dot(m_ref[...], w_ref[...], preferred_element_type=F32)


def _outproj(merged, w, x):
    n, d = x.shape
    k = merged.shape[1]
    tm = _pick_tile(n, 1024)
    tn = _pick_tile(d, 1024, LANES)
    return pl.pallas_call(
        _outproj_kernel,
        grid=(n // tm, d // tn),
        in_specs=[pl.BlockSpec((tm, k), lambda i, j: (i, 0)),
                  pl.BlockSpec((k, tn), lambda i, j: (0, j)),
                  pl.BlockSpec((tm, tn), lambda i, j: (i, j))],
        out_specs=pl.BlockSpec((tm, tn), lambda i, j: (i, j)),
        out_shape=jax.ShapeDtypeStruct((n, d), F32),
        compiler_params=_params("parallel", "arbitrary"),
        name="outproj",
    )(merged, w, x)


def _swiglu_step(xn, wg, wu, wd):
    g = jnp.dot(xn, wg, preferred_element_type=F32)
    u = jnp.dot(xn, wu, preferred_element_type=F32)
    h = (g * jax.nn.sigmoid(g) * u).astype(BF16)
    return jnp.dot(h, wd, preferred_element_type=F32)


def _ffn_kernel(x_ref, ln_ref, wg_ref, wu_ref, wd_ref, o_ref, xn_ref):
    f = pl.program_id(1)

    @pl.when(f == 0)
    def _norm():
        x = x_ref[...]
        ms = jnp.mean(x * x, axis=-1, keepdims=True)
        xn_ref[...] = (x * lax.rsqrt(ms + RMS_EPS) * ln_ref[...]).astype(BF16)
        o_ref[...] = x

    o_ref[...] += _swiglu_step(xn_ref[...], wg_ref[...], wu_ref[...], wd_ref[...])


def _ffn(x, ln, wg, wu, wd):
    n, d = x.shape
    fp = wg.shape[1]
    tf = _pick_tile(fp, 512, LANES)
    tm = _pick_tile(n, 1024)
    return pl.pallas_call(
        _ffn_kernel,
        grid=(n // tm, fp // tf),
        in_specs=[pl.BlockSpec((tm, d), lambda i, f: (i, 0)),
                  pl.BlockSpec((1, d), lambda i, f: (0, 0)),
                  pl.BlockSpec((d, tf), lambda i, f: (0, f)),
                  pl.BlockSpec((d, tf), lambda i, f: (0, f)),
                  pl.BlockSpec((tf, d), lambda i, f: (f, 0))],
        out_specs=pl.BlockSpec((tm, d), lambda i, f: (i, 0)),
        out_shape=jax.ShapeDtypeStruct((n, d), F32),
        scratch_shapes=[pltpu.VMEM((tm, d), BF16)],
        compiler_params=_params("parallel", "arbitrary"),
        name="ffn",
    )(x, ln, wg, wu, wd)


def _router_kernel(n_exp, x_ref, ln_ref, wr_hi_ref, wr_lo_ref, hn_ref, idx_ref, gate_ref):
    x = x_ref[...]
    ms = jnp.mean(x * x, axis=-1, keepdims=True)
    hf = x * lax.rsqrt(ms + RMS_EPS) * ln_ref[...]
    hn = hf.astype(BF16)
    bits = lax.bitcast_convert_type(hn.astype(F32), jnp.uint32)
    half = bits.shape[1] // 2
    hn_ref[...] = (bits[:, :half] >> 16) | (bits[:, half:] & jnp.uint32(0xFFFF0000))
    h_lo = (hf - hn.astype(F32)).astype(BF16)
    logits = (jnp.dot(hn, wr_hi_ref[...], preferred_element_type=F32)
              + jnp.dot(h_lo, wr_hi_ref[...], preferred_element_type=F32)
              + jnp.dot(hn, wr_lo_ref[...], preferred_element_type=F32))
    lane = lax.broadcasted_iota(jnp.int32, logits.shape, 1)
    logits = jnp.where(lane < n_exp, logits, -jnp.inf)
    m1 = jnp.max(logits, axis=-1, keepdims=True)
    i1 = jnp.min(jnp.where(logits == m1, lane, LANES), axis=-1, keepdims=True)
    rest = jnp.where(lane == i1, -jnp.inf, logits)
    m2 = jnp.max(rest, axis=-1, keepdims=True)
    i2 = jnp.min(jnp.where(rest == m2, lane, LANES), axis=-1, keepdims=True)
    e2 = jnp.exp(m2 - m1)
    den = 1.0 + e2
    idx_ref[...] = jnp.where(lane == 0, i1, jnp.where(lane == 1, i2, 0))
    gate_ref[...] = jnp.where(lane == 0, 1.0 / den, jnp.where(lane == 1, e2 / den, 0.0))


def _router(x, ln, wr_hi, wr_lo, n_exp):
    n, d = x.shape
    tm = _pick_tile(n, 512)
    row = lambda i: (i, 0)
    return pl.pallas_call(
        functools.partial(_router_kernel, n_exp),
        grid=(n // tm,),
        in_specs=[pl.BlockSpec((tm, d), row), pl.BlockSpec((1, d), lambda i: (0, 0)),
                  pl.BlockSpec((d, LANES), lambda i: (0, 0)), pl.BlockSpec((d, LANES), lambda i: (0, 0))],
        out_specs=[pl.BlockSpec((tm, d // 2), row), pl.BlockSpec((tm, LANES), row), pl.BlockSpec((tm, LANES), row)],
        out_shape=[jax.ShapeDtypeStruct((n, d // 2), jnp.uint32), jax.ShapeDtypeStruct((n, LANES), jnp.int32),
                   jax.ShapeDtypeStruct((n, LANES), F32)],
        compiler_params=_params("parallel"),
        name="router",
    )(x, ln, wr_hi, wr_lo)


def _experts_kernel(tm, nf, n_split, te_ref, act_ref, tok_ref, nxt_ref, hn_hbm, *rest):
    wg_refs, wu_refs, wd_refs = rest[:n_split], rest[n_split:2 * n_split], rest[2 * n_split:3 * n_split]
    o_ref, rows_ref, x_ref, wgb_ref, wub_ref, wdb_ref, sem = rest[3 * n_split:]
    i = pl.program_id(0)
    f = pl.program_id(1)
    act = act_ref[i]
    slot = i % 2
    per_step = rows_ref.shape[1] // nf

    def start_row(tokens_ref, dst_slot, v):
        tok = tokens_ref[0, jnp.minimum(v, tm - 1)]
        pltpu.make_async_copy(hn_hbm.at[pl.ds(tok, 1)], rows_ref.at[dst_slot, pl.ds(v, 1)],
                              sem.at[dst_slot]).start()

    @pl.when(f == 0)
    def _init():
        o_ref[...] = jnp.zeros_like(o_ref)

    @pl.when((f == 0) & (i == 0) & (act > 0))
    def _first_tile_rows():
        def body(v, carry):
            start_row(tok_ref, 0, v)
            return carry

        lax.fori_loop(0, per_step * nf, body, 0, unroll=8)

    rows_started = jnp.where(i == 0, act, act_ref[jnp.maximum(i - 1, 0)]) > 0

    @pl.when((f == 0) & rows_started)
    def _rows_ready():
        pltpu.make_async_copy(hn_hbm.at[pl.ds(0, per_step * nf)], rows_ref.at[slot], sem.at[slot]).wait()

    @pl.when((f == 0) & (act > 0))
    def _unpack():
        words = rows_ref[slot, :tm, :]
        half = words.shape[1]
        x_ref[:, :half] = lax.bitcast_convert_type(words << 16, F32).astype(BF16)
        x_ref[:, half:] = lax.bitcast_convert_type(words & jnp.uint32(0xFFFF0000), F32).astype(BF16)

    def start_next_tile_rows():
        for r in range(per_step):
            start_row(nxt_ref, 1 - slot, f * per_step + r)

    def round_weights():
        for slabs, dst in ((wg_refs, wgb_ref), (wu_refs, wub_ref), (wd_refs, wdb_ref)):
            rows = dst.shape[0] // n_split
            for k, slab in enumerate(slabs):
                dst[k * rows:(k + 1) * rows, :] = slab[...].astype(BF16)

    @pl.when(act == 1)
    def _step():
        start_next_tile_rows()
        round_weights()
        o_ref[...] += _swiglu_step(x_ref[...], wgb_ref[...], wub_ref[...], wdb_ref[...])

    @pl.when(act == 2)
    def _half_step():
        start_next_tile_rows()
        round_weights()
        o_ref[:tm // 2, :] += _swiglu_step(x_ref[:tm // 2, :], wgb_ref[...], wub_ref[...], wdb_ref[...])


def _experts(hn32, src_tok, wg, wu, wd, tile_expert, tile_active, tm):
    n, half = hn32.shape
    d = 2 * half
    assert n >= tm
    n_tiles = src_tok.shape[0] // tm
    n_exp, _, fe = wg.shape
    tf = _pick_tile(fe, 512, LANES)
    nf = fe // tf
    fidx = lambda i, f, te, act: jnp.where(act[i] > 0, f, nf - 1)
    n_split = EXPERT_WEIGHT_STREAMS
    in_slab = lambda k: (lambda i, f, te, act: (te[i], k, fidx(i, f, te, act)))
    out_slab = lambda k: (lambda i, f, te, act: (te[i], n_split * fidx(i, f, te, act) + k, 0))
    in_specs = [pl.BlockSpec((None, d // n_split, tf), in_slab(k)) for k in range(n_split)]
    out_specs_w = [pl.BlockSpec((None, tf // n_split, d), out_slab(k)) for k in range(n_split)]
    buf_rows = next(r for r in range(tm, tm + 8 * nf + 1) if r % nf == 0 and r % 8 == 0)
    assert n >= buf_rows
    tokens = src_tok.reshape(n_tiles, 1, tm)
    tok_spec = lambda step: pl.BlockSpec((None, 1, tm), lambda i, f, te, act: (jnp.minimum(i + step, n_tiles - 1), 0, 0),
                                         memory_space=pltpu.SMEM)
    grid_spec = pltpu.PrefetchScalarGridSpec(
        num_scalar_prefetch=2,
        grid=(n_tiles, nf),
        in_specs=[tok_spec(0), tok_spec(1), pl.BlockSpec(memory_space=pl.ANY),
                  *in_specs, *in_specs, *out_specs_w],
        out_specs=pl.BlockSpec((tm, d), lambda i, f, te, act: (i, 0), pipeline_mode=pl.Buffered(1)),
        scratch_shapes=[pltpu.VMEM((2, buf_rows, half), jnp.uint32), pltpu.VMEM((tm, d), BF16),
                        pltpu.VMEM((d, tf), BF16), pltpu.VMEM((d, tf), BF16), pltpu.VMEM((tf, d), BF16),
                        pltpu.SemaphoreType.DMA((2,))],
    )
    return pl.pallas_call(
        functools.partial(_experts_kernel, tm, nf, n_split),
        grid_spec=grid_spec,
        out_shape=jax.ShapeDtypeStruct((n_tiles * tm, d), F32),
        compiler_params=_params("arbitrary", "arbitrary"),
        name="experts",
    )(tile_expert, tile_active, tokens, tokens, hn32,
      *([wg] * n_split), *([wu] * n_split), *([wd] * n_split))


def _combine_kernel(x_ref, y_ref, g_ref, o_ref):
    g = g_ref[...]
    d = x_ref.shape[1]
    o_ref[...] = x_ref[...] + (g[:, 0:1] * y_ref[:, :d] + g[:, 1:2] * y_ref[:, d:])


def _combine(x, y_pairs, gate, row0):
    n, d = x.shape
    tm = _pick_tile(n, 256)
    assert row0 % tm == 0
    blk0 = row0 // tm
    own = lambda i: (i, 0)
    return pl.pallas_call(
        _combine_kernel,
        grid=(n // tm,),
        in_specs=[pl.BlockSpec((tm, d), own), pl.BlockSpec((tm, TOP_K * d), lambda i: (blk0 + i, 0)),
                  pl.BlockSpec((tm, LANES), own)],
        out_specs=pl.BlockSpec((tm, d), own),
        out_shape=jax.ShapeDtypeStruct((n, d), F32),
        compiler_params=_params("parallel"),
        name="combine",
    )(x, y_pairs, gate)


def _moe(x_p, x_s, ln, w_router, wg, wu, wd):
    n_p, d = x_p.shape
    n = n_p + x_s.shape[0]
    n_exp = w_router.shape[1]
    wr = jnp.pad(w_router.astype(F32), ((0, 0), (0, LANES - n_exp)))
    wr_hi = wr.astype(BF16)
    wr_lo = (wr - wr_hi.astype(F32)).astype(BF16)
    hn_p, ridx_p, rgate_p = _router(x_p, ln, wr_hi, wr_lo, n_exp)
    hn_s, ridx_s, rgate_s = _router(x_s, ln, wr_hi, wr_lo, n_exp)
    hn = jnp.concatenate([hn_p, hn_s], axis=0)
    tm = 1024 if TOP_K * n >= 8192 else 64
    n_asg = TOP_K * n
    n_tiles = -(-(n_asg + n_exp * (tm - 1)) // tm) + 1
    flat_e = jnp.concatenate([ridx_p[:, :TOP_K], ridx_s[:, :TOP_K]], axis=0).reshape(-1)
    onehot = (flat_e[:, None] == jnp.arange(n_exp, dtype=jnp.int32)[None, :]).astype(jnp.int32)
    csum = jnp.cumsum(onehot, axis=0)
    counts = csum[-1]
    rank = jnp.take_along_axis(csum, flat_e[:, None], axis=1)[:, 0] - 1
    padded = ((counts + tm - 1) // tm) * tm
    ends = jnp.cumsum(padded)
    pos = (ends - padded)[flat_e] + rank
    src_tok = jnp.zeros((n_tiles * tm,), jnp.int32).at[pos].set(
        jnp.arange(n_asg, dtype=jnp.int32) // TOP_K, unique_indices=True, mode="promise_in_bounds")
    tile_start = jnp.arange(n_tiles, dtype=jnp.int32) * tm
    te = jnp.sum((tile_start[:, None] >= ends[None, :]).astype(jnp.int32), axis=1)
    tile_expert = jnp.minimum(te, n_exp - 1)
    rows_in_tile = (ends - padded + counts)[tile_expert] - tile_start
    tile_active = jnp.where(te >= n_exp, 0, jnp.where(rows_in_tile <= tm // 2, 2, 1)).astype(jnp.int32)
    ys = _experts(hn, src_tok, wg, wu, wd, tile_expert, tile_active, tm)
    y_pairs = jnp.take(ys, pos, axis=0, mode="clip").reshape(n, TOP_K * d)
    return _combine(x_p, y_pairs, rgate_p, 0), _combine(x_s, y_pairs, rgate_s, n_p)


def kernel(x_prompt, x_sample, cache_attn_k, cache_attn_v, state_ret, ln_mix, w_in, q_norm, k_norm, attn_sinks,
           ret_gn_g, ret_gn_b, w_out, ln_ffn, ffn_w_gate, ffn_w_up, ffn_w_down, moe_router, moe_w_gate, moe_w_up,
           moe_w_down):
    batch, seq, d = x_prompt.shape
    dbatch, t_s, _ = x_sample.shape
    depth = ln_mix.shape[0]
    n_kv = cache_attn_k.shape[3]
    n_heads = state_ret.shape[2]
    d_kv = n_kv * HEAD_DIM
    d_ret = n_heads * RET_HEAD_DIM
    d_attn = w_out.shape[1] - d_ret
    n_q = d_attn // HEAD_DIM
    assert cache_attn_k.shape[4] == HEAD_DIM and state_ret.shape[3] == RET_HEAD_DIM and n_kv % 2 == 0
    assert seq % CHUNK == 0 and WINDOW % CHUNK == 0 and seq >= WINDOW and cache_attn_k.shape[2] == WINDOW
    assert w_in.shape[2] == d_attn + 2 * d_kv + 4 * d_ret
    n_p, n_s = batch * seq, dbatch * t_s
    keep = min(WINDOW, seq)

    perm_q, perm_k, perm_ao, sink_heads = _layout_perms(n_q, n_kv)
    inv_k = np.argsort(perm_k)
    qk_perm = np.concatenate([perm_q, d_attn + perm_k])

    tabs_p = _rope_tables(np.arange(seq))
    tabs_s = _rope_tables(PAST_LEN + np.tile(np.arange(t_s), dbatch))
    lane = np.arange(LANES)
    m128 = jnp.asarray(((lane[:, None] // HALF) % 2) == ((lane[None, :] // HALF) % 2), BF16)
    head_gain = lambda g: jnp.tile(g.astype(F32).reshape(2, HALF), (1, 2)).reshape(1, LANES)

    x_p, x_s = x_prompt.reshape(n_p, d), x_sample.reshape(n_s, d)
    nk_p, nv_p, nr_p, nk_s, nv_s, nr_s = [], [], [], [], [], []
    for l in range(depth):
        w_in_l = jnp.concatenate([w_in[l][:, :d_attn + d_kv][:, qk_perm], w_in[l][:, d_attn + d_kv:]], axis=1)
        w_in_l = w_in_l.astype(BF16)
        w_out_l = jnp.concatenate([w_out[l][:d_attn][perm_ao], w_out[l][d_attn:]], axis=0).astype(BF16)
        ln, gq, gk = ln_mix[l][None, :], head_gain(q_norm[l]), head_gain(k_norm[l])
        gng, gnb = ret_gn_g[l][None, :], ret_gn_b[l][None, :]
        proj_p = _proj(x_p, ln, w_in_l, m128, gq, gk, *tabs_p, d_attn, d_kv, d_ret, 512)
        proj_s = _proj(x_s, ln, w_in_l, m128, gq, gk, *tabs_s, d_attn, d_kv, d_ret, 256)
        merged_p, r_p = _mixer_prompt(proj_p, attn_sinks[l], sink_heads, gng, gnb, batch, seq, n_kv, n_q, n_heads)
        ck = cache_attn_k[l].reshape(dbatch, WINDOW, d_kv)[:, :, perm_k].astype(BF16)
        cv = cache_attn_v[l].reshape(dbatch, WINDOW, d_kv).astype(BF16)
        merged_s, r_s = _mixer_sample(proj_s, ck, cv, state_ret[l], attn_sinks[l], sink_heads, gng, gnb,
                                      0, dbatch, t_s, n_kv, n_q, n_heads)
        x_p = _outproj(merged_p, w_out_l, x_p)
        x_s = _outproj(merged_s, w_out_l, x_s)

        ka_last = proj_p[1].reshape(batch, seq, d_kv)[:, seq - keep:]
        nk_p.append(ka_last[:, :, inv_k].reshape(batch, keep, n_kv, HEAD_DIM))
        nv_p.append(proj_p[2].reshape(batch, seq, d_kv)[:, seq - keep:].reshape(batch, keep, n_kv, HEAD_DIM))
        nk_s.append(proj_s[1][:, inv_k].reshape(dbatch, t_s, n_kv, HEAD_DIM))
        nv_s.append(proj_s[2].reshape(dbatch, t_s, n_kv, HEAD_DIM))
        nr_p.append(r_p)
        nr_s.append(r_s)

        i = l // 2
        lf = ln_ffn[l][None, :]
        if l % 2 == 0:
            fpad = (-ffn_w_gate.shape[2]) % COL_TILE
            zcols, zrows = jnp.zeros((d, fpad), BF16), jnp.zeros((fpad, d), BF16)
            wg = jnp.concatenate([ffn_w_gate[i].astype(BF16), zcols], axis=1)
            wu = jnp.concatenate([ffn_w_up[i].astype(BF16), zcols], axis=1)
            wd = jnp.concatenate([ffn_w_down[i].astype(BF16), zrows], axis=0)
            x_p, x_s = _ffn(x_p, lf, wg, wu, wd), _ffn(x_s, lf, wg, wu, wd)
        else:
            x_p, x_s = _moe(x_p, x_s, lf, moe_router[i], moe_w_gate[i], moe_w_up[i], moe_w_down[i])

    return (x_p.reshape(batch, seq, d), x_s.reshape(dbatch, t_s, d),
            jnp.stack(nk_p), jnp.stack(nv_p), jnp.stack(nr_p), jnp.stack(nk_s), jnp.stack(nv_s), jnp.stack(nr_s))
```

```python
import functools

import numpy as np
import jax
import jax.numpy as jnp
from jax import lax
from jax.experimental import pallas as pl
from jax.experimental.pallas import tpu as pltpu

CHUNK = 64
WINDOW = 128
HEAD_DIM = 64
RET_HEAD_DIM = 128
PAST_LEN = 2048
ROPE_THETA = 10000.0
TOP_K = 2
RMS_EPS = 1e-6
GN_EPS = 1e-5

LANES = 128
HALF = HEAD_DIM // 2
VMEM_LIMIT = 56 * 1024 * 1024
COL_TILE = 512
EXPERT_WEIGHT_STREAMS = 1
F32 = jnp.float32
BF16 = jnp.bfloat16


def _pick_tile(n, target, mult=16):
    best = None
    for t in range(mult, min(n, target) + 1, mult):
        if n % t == 0:
            best = t
    assert best is not None, (n, target)
    return best


def _params(*sem):
    return pltpu.CompilerParams(dimension_semantics=sem, vmem_limit_bytes=VMEM_LIMIT)


def _layout_perms(n_q, n_kv):
    g_sz = n_q // n_kv
    n_pairs = n_kv // 2
    perm_k = np.zeros(n_kv * HEAD_DIM, np.int32)
    perm_q = np.zeros(n_q * HEAD_DIM, np.int32)
    perm_ao = np.zeros(n_q * HEAD_DIM, np.int32)
    sink_heads = np.zeros((n_pairs, 2 * g_sz), np.int32)
    for p in range(n_pairs):
        for hf in range(2):
            for s in range(2):
                for r in range(HALF):
                    perm_k[p * LANES + hf * 64 + s * HALF + r] = (2 * p + s) * HEAD_DIM + hf * HALF + r
        for g in range(g_sz):
            bq = p * g_sz + g
            for s in range(2):
                head = g_sz * (2 * p + s) + g
                sink_heads[p, 2 * g + s] = head
                for hf in range(2):
                    for r in range(HALF):
                        perm_q[bq * LANES + hf * 64 + s * HALF + r] = head * HEAD_DIM + hf * HALF + r
                for d in range(HEAD_DIM):
                    perm_ao[bq * LANES + s * HEAD_DIM + d] = head * HEAD_DIM + d
    return perm_q, perm_k, perm_ao, sink_heads


def _rope_tables(pos):
    pf = pos.astype(F32)[:, None]
    inv_a = ROPE_THETA ** (-jnp.arange(0, HEAD_DIM, 2, dtype=F32) / HEAD_DIM)
    ang = pf * inv_a[None, :]
    c, s = jnp.cos(ang), jnp.sin(ang)
    ca = jnp.concatenate([c, c, c, c], axis=-1)
    sa = jnp.concatenate([-s, -s, s, s], axis=-1)
    inv_r = 10000.0 ** (-jnp.linspace(0.0, 1.0, RET_HEAD_DIM // 2, dtype=F32))
    ang = pf * inv_r[None, :]
    c, s = jnp.cos(ang), jnp.sin(ang)
    cr = jnp.concatenate([c, c], axis=-1)
    sr = jnp.concatenate([-s, s], axis=-1)
    return ca, sa, cr, sr


def _decay_tables(n_heads, chunk):
    log_g = jnp.log1p(-jnp.exp2(-5.0 - jnp.arange(n_heads, dtype=F32)))
    idx = jnp.arange(chunk, dtype=F32)
    dmat = jnp.exp(log_g[:, None, None] * jnp.abs(idx[:, None] - idx[None, :]))
    dec_in = jnp.exp(log_g[None, :] * (idx[:, None] + 1.0))
    dec_kv = jnp.exp(log_g[None, :] * (chunk - 1.0 - idx[:, None]))
    dec_chunk = jnp.exp(log_g * chunk)
    bc = lambda t: jnp.broadcast_to(t.T[:, :, None], (n_heads, chunk, LANES))
    return dmat, bc(dec_in), bc(dec_kv), dec_chunk


def _rope128(y, c, s):
    return y * c + pltpu.roll(y, 64, 1) * s


def _proj_kernel(nq, nr, x_ref, ln_ref, w_ref, m_ref, gq_ref, gk_ref, ca_ref, sa_ref, cr_ref, sr_ref,
                 qa_ref, ka_ref, va_ref, qr_ref, kr_ref, vr_ref, gate_ref, xn_ref):
    x = x_ref[...]
    ms = jnp.mean(x * x, axis=-1, keepdims=True)
    xn_ref[...] = (x * lax.rsqrt(ms + RMS_EPS) * ln_ref[...]).astype(BF16)
    n_sub = COL_TILE // LANES

    def col_tile(j):
        return jnp.dot(xn_ref[...], w_ref[j], preferred_element_type=F32)

    def lanes(j, c):
        return slice(j * COL_TILE + c * LANES, j * COL_TILE + (c + 1) * LANES)

    def head_norm_rope(yc, g):
        y2 = yc * yc
        hi = y2.astype(BF16)
        lo = (y2 - hi.astype(F32)).astype(BF16)
        ss = (jnp.dot(hi, m_ref[...], preferred_element_type=F32)
              + jnp.dot(lo, m_ref[...], preferred_element_type=F32))
        yn = yc * lax.rsqrt(ss * (1.0 / HEAD_DIM) + RMS_EPS) * g
        return _rope128(yn, ca_ref[...], sa_ref[...])

    for j in range(nq):
        y = col_tile(j)
        for c in range(n_sub):
            qa_ref[:, lanes(j, c)] = (head_norm_rope(y[:, lanes(0, c)], gq_ref[...])
                                      * (HEAD_DIM ** -0.5)).astype(BF16)

    y = col_tile(nq)
    half = COL_TILE // 2
    for c in range(half // LANES):
        ka_ref[:, lanes(0, c)] = head_norm_rope(y[:, lanes(0, c)], gk_ref[...])
    va_ref[...] = y[:, half:]

    for j in range(nr):
        y = col_tile(nq + 1 + j)
        for c in range(n_sub):
            qr_ref[:, lanes(j, c)] = _rope128(y[:, lanes(0, c)], cr_ref[...], sr_ref[...]).astype(BF16)
    for j in range(nr):
        y = col_tile(nq + 1 + nr + j)
        for c in range(n_sub):
            kr_ref[:, lanes(j, c)] = (_rope128(y[:, lanes(0, c)], cr_ref[...], sr_ref[...])
                                      * (RET_HEAD_DIM ** -0.5))
    for j in range(nr):
        vr_ref[:, j * COL_TILE:(j + 1) * COL_TILE] = col_tile(nq + 1 + 2 * nr + j).astype(BF16)
    for j in range(nr):
        gate_ref[:, j * COL_TILE:(j + 1) * COL_TILE] = col_tile(nq + 1 + 3 * nr + j)


def _proj(x, ln, w, m128, gq, gk, ca, sa, cr, sr, d_attn, d_kv, d_ret, tm_target):
    n, d = x.shape
    assert d_attn % COL_TILE == 0 and 2 * d_kv == COL_TILE and d_ret % COL_TILE == 0
    nq, nr = d_attn // COL_TILE, d_ret // COL_TILE
    n_col = nq + 1 + 4 * nr
    tab_rows = ca.shape[0]
    assert n % tab_rows == 0
    tm = _pick_tile(tab_rows, tm_target)
    row = lambda i: (i, 0)
    tab = pl.BlockSpec((tm, LANES), lambda i: (i % (tab_rows // tm), 0))
    vec = pl.BlockSpec((1, LANES), lambda i: (0, 0))
    return pl.pallas_call(
        functools.partial(_proj_kernel, nq, nr),
        grid=(n // tm,),
        in_specs=[
            pl.BlockSpec((tm, d), row),
            pl.BlockSpec((1, d), lambda i: (0, 0)),
            pl.BlockSpec((n_col, d, COL_TILE), lambda i: (0, 0, 0), pipeline_mode=pl.Buffered(1)),
            pl.BlockSpec((LANES, LANES), lambda i: (0, 0)),
            vec, vec, tab, tab, tab, tab,
        ],
        out_specs=[
            pl.BlockSpec((tm, d_attn), row),
            pl.BlockSpec((tm, d_kv), row),
            pl.BlockSpec((tm, d_kv), row),
            pl.BlockSpec((tm, d_ret), row),
            pl.BlockSpec((tm, d_ret), row),
            pl.BlockSpec((tm, d_ret), row),
            pl.BlockSpec((tm, d_ret), row),
        ],
        out_shape=[
            jax.ShapeDtypeStruct((n, d_attn), BF16),
            jax.ShapeDtypeStruct((n, d_kv), F32),
            jax.ShapeDtypeStruct((n, d_kv), F32),
            jax.ShapeDtypeStruct((n, d_ret), BF16),
            jax.ShapeDtypeStruct((n, d_ret), F32),
            jax.ShapeDtypeStruct((n, d_ret), BF16),
            jax.ShapeDtypeStruct((n, d_ret), F32),
        ],
        scratch_shapes=[pltpu.VMEM((tm, d), BF16)],
        compiler_params=_params("parallel"),
        name="proj",
    )(x, ln, w, m128, gq, gk, ca, sa, cr, sr)


def _attend(items, valid, sink_ref, t, n_pairs, g_sz):
    lane = lax.broadcasted_iota(jnp.int32, (1, LANES), 1)
    slot_a = ((lane // HALF) % 2) == 0
    first_half = lane < HEAD_DIM
    zero = jnp.zeros((), BF16)
    jobs = [(i, p) for i in range(len(items)) for p in range(n_pairs)]
    scores = []
    for i, p in jobs:
        qa, k_all, _ = items[i]
        parts = []
        for g in range(g_sz):
            qb = qa[:, (p * g_sz + g) * LANES:(p * g_sz + g + 1) * LANES]
            parts.append(jnp.where(slot_a, qb, zero))
            parts.append(jnp.where(slot_a, zero, qb))
        qs = jnp.concatenate(parts, axis=0)
        scores.append(lax.dot_general(qs, k_all[:, p * LANES:(p + 1) * LANES], (((1,), (1,)), ((), ())),
                                      preferred_element_type=F32))
    probs = []
    for (i, p), s in zip(jobs, scores):
        if valid is not None:
            s = jnp.where(valid, s, -jnp.inf)
        sk = sink_ref[p][:, 0:1]
        m = jnp.maximum(jnp.max(s, axis=-1, keepdims=True), sk)
        e = jnp.exp(s - m)
        den = jnp.sum(e, axis=-1, keepdims=True) + jnp.exp(sk - m)
        probs.append((e / den).astype(BF16))
    outs = [[] for _ in items]
    for (i, p), pn in zip(jobs, probs):
        o = jnp.dot(pn, items[i][2][:, p * LANES:(p + 1) * LANES], preferred_element_type=F32)
        for g in range(g_sz):
            o_a = o[(2 * g) * t:(2 * g + 1) * t]
            o_b = o[(2 * g + 1) * t:(2 * g + 2) * t]
            outs[i].append(jnp.where(first_half, o_a, o_b))
    return outs


def _retain(items, dmat_ref, decin_ref, deckv_ref, dchunk_ref, gng, gnb, n_heads):
    nt = (((1,), (1,)), ((), ()))
    tn = (((0,), (0,)), ((), ()))
    jobs = [(i, h) for i in range(len(items)) for h in range(n_heads)]
    sl = lambda h: slice(h * RET_HEAD_DIM, (h + 1) * RET_HEAD_DIM)
    intra, inter, r_prevs = [], [], []
    for i, h in jobs:
        qr, kr, _, _, r_in_ref, _ = items[i]
        q = qr[:, sl(h)]
        r_prev = r_in_ref[h]
        r_prevs.append(r_prev)
        intra.append(lax.dot_general(q, kr[:, sl(h)].astype(BF16), nt, preferred_element_type=F32) * dmat_ref[h])
        inter.append(jnp.dot(q, r_prev.astype(BF16), preferred_element_type=F32) * decin_ref[h])
    for (i, h), r_prev in zip(jobs, r_prevs):
        _, kr, vr, _, _, r_out_ref = items[i]
        kd = (kr[:, sl(h)] * deckv_ref[h]).astype(BF16)
        kv = lax.dot_general(kd, vr[:, sl(h)], tn, preferred_element_type=F32)
        r_out_ref[h] = r_prev * dchunk_ref[h] + kv
    outs = [[] for _ in items]
    for (i, h), s, o_inter in zip(jobs, intra, inter):
        _, _, vr, gate, _, _ = items[i]
        o = jnp.dot(s.astype(BF16), vr[:, sl(h)], preferred_element_type=F32) + o_inter
        mu = jnp.mean(o, axis=-1, keepdims=True)
        dev = o - mu
        var = jnp.mean(dev * dev, axis=-1, keepdims=True)
        r = dev * lax.rsqrt(var + GN_EPS) * gng[:, sl(h)] + gnb[:, sl(h)]
        gt = gate[:, sl(h)]
        outs[i].append(gt * jax.nn.sigmoid(gt) * r)
    return outs


def _store_merged(merged_ref, ao, ro, d_attn):
    for b, blk in enumerate(ao):
        merged_ref[:, b * LANES:(b + 1) * LANES] = blk.astype(BF16)
    for h, blk in enumerate(ro):
        merged_ref[:, d_attn + h * RET_HEAD_DIM:d_attn + (h + 1) * RET_HEAD_DIM] = blk.astype(BF16)


def _mixer_prompt_kernel(nc, t, batch, n_pairs, g_sz, n_heads, d_attn,
                         dchunk_ref, qa_ref, *rest):
    n_win = WINDOW // t + 1
    k_refs, v_refs = rest[:n_win], rest[n_win:2 * n_win]
    (qr_ref, kr_ref, vr_ref, gate_ref, sink_ref, dmat_ref, decin_ref, deckv_ref, gng_ref, gnb_ref,
     merged_ref, rout_ref, r_ref) = rest[2 * n_win:]
    c = pl.program_id(0)

    @pl.when(c == 0)
    def _reset():
        r_ref[...] = jnp.zeros_like(r_ref)

    kidx = lax.broadcasted_iota(jnp.int32, (1, n_win * t), 1)
    valid = kidx >= (n_win - 1 - c) * t
    att_items = [(qa_ref[b],
                  jnp.concatenate([r[b].astype(BF16) for r in k_refs], axis=0),
                  jnp.concatenate([r[b].astype(BF16) for r in v_refs], axis=0)) for b in range(batch)]
    ret_items = [(qr_ref[b], kr_ref[b], vr_ref[b], gate_ref[b], r_ref.at[b], r_ref.at[b]) for b in range(batch)]
    ao = _attend(att_items, valid, sink_ref, t, n_pairs, g_sz)
    ro = _retain(ret_items, dmat_ref, decin_ref, deckv_ref, dchunk_ref, gng_ref[...], gnb_ref[...], n_heads)
    for b in range(batch):
        _store_merged(merged_ref.at[b], ao[b], ro[b], d_attn)

    @pl.when(c == nc - 1)
    def _emit_state():
        rout_ref[...] = r_ref[...]


def _mixer_sample_kernel(t, n_pairs, g_sz, n_heads, d_attn,
                         dchunk_ref, qa_ref, ka_ref, va_ref, qr_ref, kr_ref, vr_ref, gate_ref,
                         ck_ref, cv_ref, r0_ref,
                         sink_ref, dmat_ref, decin_ref, deckv_ref, gng_ref, gnb_ref,
                         merged_ref, rout_ref):
    k_all = jnp.concatenate([ck_ref[...], ka_ref[...].astype(BF16)], axis=0)
    v_all = jnp.concatenate([cv_ref[...], va_ref[...].astype(BF16)], axis=0)
    ao = _attend([(qa_ref[...], k_all, v_all)], None, sink_ref, t, n_pairs, g_sz)
    ro = _retain([(qr_ref[...], kr_ref[...], vr_ref[...], gate_ref[...], r0_ref, rout_ref)],
                 dmat_ref, decin_ref, deckv_ref, dchunk_ref, gng_ref[...], gnb_ref[...], n_heads)
    _store_merged(merged_ref, ao[0], ro[0], d_attn)


def _sink_table(sinks, sink_heads, t):
    per_row = jnp.repeat(sinks.astype(F32)[sink_heads], t, axis=1)
    return jnp.broadcast_to(per_row[:, :, None], per_row.shape + (LANES,))


def _mixer_prompt(proj_out, sinks, sink_heads, gng, gnb, batch, seq, n_kv, n_q, n_heads):
    qa, ka, va, qr, kr, vr, gate = proj_out
    t = CHUNK
    nc = seq // t
    d_attn, d_kv, d_ret = n_q * HEAD_DIM, n_kv * HEAD_DIM, n_heads * RET_HEAD_DIM
    n_pairs, g_sz = n_kv // 2, n_q // n_kv
    dmat, decin, deckv, dchunk = _decay_tables(n_heads, t)
    sink_tab = _sink_table(sinks, sink_heads, t)
    seq3 = lambda a: a.reshape(batch, seq, a.shape[-1])
    qa, ka, va, qr, kr, vr, gate = map(seq3, proj_out)
    row = lambda c, *_: (0, c, 0)
    back = lambda k: (lambda c, *_: (0, jnp.maximum(c - k, 0), 0))
    const3 = lambda c, *_: (0, 0, 0)
    const2 = lambda c, *_: (0, 0)
    n_win = WINDOW // t + 1
    kv_specs = [pl.BlockSpec((batch, t, d_kv), back(n_win - 1 - s)) for s in range(n_win)]
    state_shape = (batch, n_heads, RET_HEAD_DIM, RET_HEAD_DIM)
    grid_spec = pltpu.PrefetchScalarGridSpec(
        num_scalar_prefetch=1,
        grid=(nc,),
        in_specs=[
            pl.BlockSpec((batch, t, d_attn), row), *kv_specs, *kv_specs,
            pl.BlockSpec((batch, t, d_ret), row), pl.BlockSpec((batch, t, d_ret), row),
            pl.BlockSpec((batch, t, d_ret), row), pl.BlockSpec((batch, t, d_ret), row),
            pl.BlockSpec(sink_tab.shape, const3), pl.BlockSpec(dmat.shape, const3),
            pl.BlockSpec(decin.shape, const3), pl.BlockSpec(deckv.shape, const3),
            pl.BlockSpec((1, d_ret), const2), pl.BlockSpec((1, d_ret), const2),
        ],
        out_specs=[
            pl.BlockSpec((batch, t, d_attn + d_ret), row),
            pl.BlockSpec(state_shape, lambda c, *_: (0, 0, 0, 0)),
        ],
        scratch_shapes=[pltpu.VMEM(state_shape, F32)],
    )
    merged, r_fin = pl.pallas_call(
        functools.partial(_mixer_prompt_kernel, nc, t, batch, n_pairs, g_sz, n_heads, d_attn),
        grid_spec=grid_spec,
        out_shape=[jax.ShapeDtypeStruct((batch, seq, d_attn + d_ret), BF16),
                   jax.ShapeDtypeStruct(state_shape, F32)],
        compiler_params=_params("arbitrary"),
        name="mixer_prompt",
    )(dchunk, qa, *([ka] * n_win), *([va] * n_win), qr, kr, vr, gate, sink_tab, dmat, decin, deckv, gng, gnb)
    return merged.reshape(batch * seq, d_attn + d_ret), r_fin


def _mixer_sample(proj_out, ck, cv, r0, sinks, sink_heads, gng, gnb, row0, dbatch, t, n_kv, n_q, n_heads):
    qa, ka, va, qr, kr, vr, gate = proj_out
    d_attn, d_kv, d_ret = n_q * HEAD_DIM, n_kv * HEAD_DIM, n_heads * RET_HEAD_DIM
    n_pairs, g_sz = n_kv // 2, n_q // n_kv
    dmat, decin, deckv, dchunk = _decay_tables(n_heads, t)
    sink_tab = _sink_table(sinks, sink_heads, t)
    blk0 = row0 // t
    row = lambda b, *_: (blk0 + b, 0)
    const3 = lambda b, *_: (0, 0, 0)
    const2 = lambda b, *_: (0, 0)
    state_spec = pl.BlockSpec((None, n_heads, RET_HEAD_DIM, RET_HEAD_DIM), lambda b, *_: (b, 0, 0, 0))
    n_hist = ck.shape[1]
    grid_spec = pltpu.PrefetchScalarGridSpec(
        num_scalar_prefetch=1,
        grid=(dbatch,),
        in_specs=[
            pl.BlockSpec((t, d_attn), row), pl.BlockSpec((t, d_kv), row), pl.BlockSpec((t, d_kv), row),
            pl.BlockSpec((t, d_ret), row), pl.BlockSpec((t, d_ret), row), pl.BlockSpec((t, d_ret), row),
            pl.BlockSpec((t, d_ret), row),
            pl.BlockSpec((None, n_hist, d_kv), lambda b, *_: (b, 0, 0)),
            pl.BlockSpec((None, n_hist, d_kv), lambda b, *_: (b, 0, 0)),
            state_spec,
            pl.BlockSpec(sink_tab.shape, const3), pl.BlockSpec(dmat.shape, const3),
            pl.BlockSpec(decin.shape, const3), pl.BlockSpec(deckv.shape, const3),
            pl.BlockSpec((1, d_ret), const2), pl.BlockSpec((1, d_ret), const2),
        ],
        out_specs=[pl.BlockSpec((t, d_attn + d_ret), lambda b, *_: (b, 0)), state_spec],
    )
    return pl.pallas_call(
        functools.partial(_mixer_sample_kernel, t, n_pairs, g_sz, n_heads, d_attn),
        grid_spec=grid_spec,
        out_shape=[jax.ShapeDtypeStruct((dbatch * t, d_attn + d_ret), BF16),
                   jax.ShapeDtypeStruct(r0.shape, F32)],
        compiler_params=_params("parallel"),
        name="mixer_sample",
    )(dchunk, qa, ka, va, qr, kr, vr, gate, ck, cv, r0, sink_tab, dmat, decin, deckv, gng, gnb)


def _outproj_kernel(m_ref, w_ref, x_ref, o_ref):
    o_ref[...] = x_ref[...] + jnp.dot(m_ref[...], w_ref[...], preferred_element_type=F32)


def _outproj(merged, w, x):
    n, d = x.shape
    k = merged.shape[1]
    tm = _pick_tile(n, 1024)
    tn = _pick_tile(d, 1024, LANES)
    return pl.pallas_call(
        _outproj_kernel,
        grid=(n // tm, d // tn),
        in_specs=[pl.BlockSpec((tm, k), lambda i, j: (i, 0)),
                  pl.BlockSpec((k, tn), lambda i, j: (0, j)),
                  pl.BlockSpec((tm, tn), lambda i, j: (i, j))],
        out_specs=pl.BlockSpec((tm, tn), lambda i, j: (i, j)),
        out_shape=jax.ShapeDtypeStruct((n, d), F32),
        compiler_params=_params("parallel", "arbitrary"),
        name="outproj",
    )(merged, w, x)


def _swiglu_step(xn, wg, wu, wd):
    g = jnp.dot(xn, wg, preferred_element_type=F32)
    u = jnp.dot(xn, wu, preferred_element_type=F32)
    h = (g * jax.nn.sigmoid(g) * u).astype(BF16)
    return jnp.dot(h, wd, preferred_element_type=F32)


def _ffn_kernel(x_ref, ln_ref, wg_ref, wu_ref, wd_ref, o_ref, xn_ref):
    f = pl.program_id(1)

    @pl.when(f == 0)
    def _norm():
        x = x_ref[...]
        ms = jnp.mean(x * x, axis=-1, keepdims=True)
        xn_ref[...] = (x * lax.rsqrt(ms + RMS_EPS) * ln_ref[...]).astype(BF16)
        o_ref[...] = x

    o_ref[...] += _swiglu_step(xn_ref[...], wg_ref[...], wu_ref[...], wd_ref[...])


def _ffn(x, ln, wg, wu, wd):
    n, d = x.shape
    fp = wg.shape[1]
    tf = _pick_tile(fp, 512, LANES)
    tm = _pick_tile(n, 1024)
    return pl.pallas_call(
        _ffn_kernel,
        grid=(n // tm, fp // tf),
        in_specs=[pl.BlockSpec((tm, d), lambda i, f: (i, 0)),
                  pl.BlockSpec((1, d), lambda i, f: (0, 0)),
                  pl.BlockSpec((d, tf), lambda i, f: (0, f)),
                  pl.BlockSpec((d, tf), lambda i, f: (0, f)),
                  pl.BlockSpec((tf, d), lambda i, f: (f, 0))],
        out_specs=pl.BlockSpec((tm, d), lambda i, f: (i, 0)),
        out_shape=jax.ShapeDtypeStruct((n, d), F32),
        scratch_shapes=[pltpu.VMEM((tm, d), BF16)],
        compiler_params=_params("parallel", "arbitrary"),
        name="ffn",
    )(x, ln, wg, wu, wd)


def _router_kernel(n_exp, x_ref, ln_ref, wr_hi_ref, wr_lo_ref, hn_ref, idx_ref, gate_ref):
    x = x_ref[...]
    ms = jnp.mean(x * x, axis=-1, keepdims=True)
    hf = x * lax.rsqrt(ms + RMS_EPS) * ln_ref[...]
    hn = hf.astype(BF16)
    bits = lax.bitcast_convert_type(hn.astype(F32), jnp.uint32)
    half = bits.shape[1] // 2
    hn_ref[...] = (bits[:, :half] >> 16) | (bits[:, half:] & jnp.uint32(0xFFFF0000))
    h_lo = (hf - hn.astype(F32)).astype(BF16)
    logits = (jnp.dot(hn, wr_hi_ref[...], preferred_element_type=F32)
              + jnp.dot(h_lo, wr_hi_ref[...], preferred_element_type=F32)
              + jnp.dot(hn, wr_lo_ref[...], preferred_element_type=F32))
    lane = lax.broadcasted_iota(jnp.int32, logits.shape, 1)
    logits = jnp.where(lane < n_exp, logits, -jnp.inf)
    m1 = jnp.max(logits, axis=-1, keepdims=True)
    i1 = jnp.min(jnp.where(logits == m1, lane, LANES), axis=-1, keepdims=True)
    rest = jnp.where(lane == i1, -jnp.inf, logits)
    m2 = jnp.max(rest, axis=-1, keepdims=True)
    i2 = jnp.min(jnp.where(rest == m2, lane, LANES), axis=-1, keepdims=True)
    e2 = jnp.exp(m2 - m1)
    den = 1.0 + e2
    idx_ref[...] = jnp.where(lane == 0, i1, jnp.where(lane == 1, i2, 0))
    gate_ref[...] = jnp.where(lane == 0, 1.0 / den, jnp.where(lane == 1, e2 / den, 0.0))


def _router(x, ln, wr_hi, wr_lo, n_exp):
    n, d = x.shape
    tm = _pick_tile(n, 512)
    row = lambda i: (i, 0)
    return pl.pallas_call(
        functools.partial(_router_kernel, n_exp),
        grid=(n // tm,),
        in_specs=[pl.BlockSpec((tm, d), row), pl.BlockSpec((1, d), lambda i: (0, 0)),
                  pl.BlockSpec((d, LANES), lambda i: (0, 0)), pl.BlockSpec((d, LANES), lambda i: (0, 0))],
        out_specs=[pl.BlockSpec((tm, d // 2), row), pl.BlockSpec((tm, LANES), row), pl.BlockSpec((tm, LANES), row)],
        out_shape=[jax.ShapeDtypeStruct((n, d // 2), jnp.uint32), jax.ShapeDtypeStruct((n, LANES), jnp.int32),
                   jax.ShapeDtypeStruct((n, LANES), F32)],
        compiler_params=_params("parallel"),
        name="router",
    )(x, ln, wr_hi, wr_lo)


def _experts_kernel(tm, nf, n_split, te_ref, act_ref, tok_ref, nxt_ref, hn_hbm, *rest):
    wg_refs, wu_refs, wd_refs = rest[:n_split], rest[n_split:2 * n_split], rest[2 * n_split:3 * n_split]
    o_ref, rows_ref, x_ref, wgb_ref, wub_ref, wdb_ref, sem = rest[3 * n_split:]
    i = pl.program_id(0)
    f = pl.program_id(1)
    act = act_ref[i]
    slot = i % 2
    per_step = rows_ref.shape[1] // nf

    def start_row(tokens_ref, dst_slot, v):
        tok = tokens_ref[0, jnp.minimum(v, tm - 1)]
        pltpu.make_async_copy(hn_hbm.at[pl.ds(tok, 1)], rows_ref.at[dst_slot, pl.ds(v, 1)],
                              sem.at[dst_slot]).start(priority=1)

    @pl.when(f == 0)
    def _init():
        o_ref[...] = jnp.zeros_like(o_ref)

    @pl.when((f == 0) & (i == 0) & (act > 0))
    def _first_tile_rows():
        def body(v, carry):
            start_row(tok_ref, 0, v)
            return carry

        lax.fori_loop(0, per_step * nf, body, 0, unroll=8)

    rows_started = jnp.where(i == 0, act, act_ref[jnp.maximum(i - 1, 0)]) > 0

    @pl.when((f == 0) & rows_started)
    def _rows_ready():
        pltpu.make_async_copy(hn_hbm.at[pl.ds(0, per_step * nf)], rows_ref.at[slot], sem.at[slot]).wait()

    @pl.when((f == 0) & (act > 0))
    def _unpack():
        words = rows_ref[slot, :tm, :]
        half = words.shape[1]
        x_ref[:, :half] = lax.bitcast_convert_type(words << 16, F32).astype(BF16)
        x_ref[:, half:] = lax.bitcast_convert_type(words & jnp.uint32(0xFFFF0000), F32).astype(BF16)

    def start_next_tile_rows():
        for r in range(per_step):
            start_row(nxt_ref, 1 - slot, f * per_step + r)

    def round_weights():
        for slabs, dst in ((wg_refs, wgb_ref), (wu_refs, wub_ref), (wd_refs, wdb_ref)):
            rows = dst.shape[0] // n_split
            for k, slab in enumerate(slabs):
                dst[k * rows:(k + 1) * rows, :] = slab[...].astype(BF16)

    @pl.when(act == 1)
    def _step():
        start_next_tile_rows()
        round_weights()
        o_ref[...] += _swiglu_step(x_ref[...], wgb_ref[...], wub_ref[...], wdb_ref[...])

    @pl.when(act == 2)
    def _half_step():
        start_next_tile_rows()
        round_weights()
        o_ref[:tm // 2, :] += _swiglu_step(x_ref[:tm // 2, :], wgb_ref[...], wub_ref[...], wdb_ref[...])


def _experts(hn32, src_tok, wg, wu, wd, tile_expert, tile_active, tm):
    n, half = hn32.shape
    d = 2 * half
    assert n >= tm
    n_tiles = src_tok.shape[0] // tm
    n_exp, _, fe = wg.shape
    tf = _pick_tile(fe, 512, LANES)
    nf = fe // tf
    fidx = lambda i, f, te, act: jnp.where(act[i] > 0, f, nf - 1)
    n_split = EXPERT_WEIGHT_STREAMS
    in_slab = lambda k: (lambda i, f, te, act: (te[i], k, fidx(i, f, te, act)))
    out_slab = lambda k: (lambda i, f, te, act: (te[i], n_split * fidx(i, f, te, act) + k, 0))
    in_specs = [pl.BlockSpec((None, d // n_split, tf), in_slab(k)) for k in range(n_split)]
    out_specs_w = [pl.BlockSpec((None, tf // n_split, d), out_slab(k)) for k in range(n_split)]
    buf_rows = next(r for r in range(tm, tm + 8 * nf + 1) if r % nf == 0 and r % 8 == 0)
    assert n >= buf_rows
    tokens = src_tok.reshape(n_tiles, 1, tm)
    tok_spec = lambda step: pl.BlockSpec((None, 1, tm), lambda i, f, te, act: (jnp.minimum(i + step, n_tiles - 1), 0, 0),
                                         memory_space=pltpu.SMEM)
    grid_spec = pltpu.PrefetchScalarGridSpec(
        num_scalar_prefetch=2,
        grid=(n_tiles, nf),
        in_specs=[tok_spec(0), tok_spec(1), pl.BlockSpec(memory_space=pl.ANY),
                  *in_specs, *in_specs, *out_specs_w],
        out_specs=pl.BlockSpec((tm, d), lambda i, f, te, act: (i, 0), pipeline_mode=pl.Buffered(1)),
        scratch_shapes=[pltpu.VMEM((2, buf_rows, half), jnp.uint32), pltpu.VMEM((tm, d), BF16),
                        pltpu.VMEM((d, tf), BF16), pltpu.VMEM((d, tf), BF16), pltpu.VMEM((tf, d), BF16),
                        pltpu.SemaphoreType.DMA((2,))],
    )
    return pl.pallas_call(
        functools.partial(_experts_kernel, tm, nf, n_split),
        grid_spec=grid_spec,
        out_shape=jax.ShapeDtypeStruct((n_tiles * tm, d), F32),
        compiler_params=_params("arbitrary", "arbitrary"),
        name="experts",
    )(tile_expert, tile_active, tokens, tokens, hn32,
      *([wg] * n_split), *([wu] * n_split), *([wd] * n_split))


def _combine_kernel(x_ref, y0_ref, y1_ref, g_ref, o_ref):
    g = g_ref[...]
    o_ref[...] = x_ref[...] + (g[:, 0:1] * y0_ref[...] + g[:, 1:2] * y1_ref[...])


def _combine(x, y0, y1, gate, row0):
    n, d = x.shape
    tm = _pick_tile(n, 256)
    assert row0 % tm == 0
    blk0 = row0 // tm
    own = lambda i: (i, 0)
    shifted = lambda i: (blk0 + i, 0)
    return pl.pallas_call(
        _combine_kernel,
        grid=(n // tm,),
        in_specs=[pl.BlockSpec((tm, d), own), pl.BlockSpec((tm, d), shifted), pl.BlockSpec((tm, d), shifted),
                  pl.BlockSpec((tm, LANES), own)],
        out_specs=pl.BlockSpec((tm, d), own),
        out_shape=jax.ShapeDtypeStruct((n, d), F32),
        compiler_params=_params("parallel"),
        name="combine",
    )(x, y0, y1, gate)


def _moe(x_p, x_s, ln, w_router, wg, wu, wd):
    n_p, d = x_p.shape
    n = n_p + x_s.shape[0]
    n_exp = w_router.shape[1]
    wr = jnp.pad(w_router.astype(F32), ((0, 0), (0, LANES - n_exp)))
    wr_hi = wr.astype(BF16)
    wr_lo = (wr - wr_hi.astype(F32)).astype(BF16)
    hn_p, ridx_p, rgate_p = _router(x_p, ln, wr_hi, wr_lo, n_exp)
    hn_s, ridx_s, rgate_s = _router(x_s, ln, wr_hi, wr_lo, n_exp)
    hn = jnp.concatenate([hn_p, hn_s], axis=0)
    tm = 1024 if TOP_K * n >= 8192 else 64
    n_asg = TOP_K * n
    n_tiles = -(-(n_asg + n_exp * (tm - 1)) // tm) + 1
    flat_e = jnp.concatenate([ridx_p[:, :TOP_K], ridx_s[:, :TOP_K]], axis=0).reshape(-1)
    onehot = (flat_e[:, None] == jnp.arange(n_exp, dtype=jnp.int32)[None, :]).astype(jnp.int32)
    csum = jnp.cumsum(onehot, axis=0)
    counts = csum[-1]
    rank = jnp.take_along_axis(csum, flat_e[:, None], axis=1)[:, 0] - 1
    padded = ((counts + tm - 1) // tm) * tm
    ends = jnp.cumsum(padded)
    pos = (ends - padded)[flat_e] + rank
    src_tok = jnp.zeros((n_tiles * tm,), jnp.int32).at[pos].set(
        jnp.arange(n_asg, dtype=jnp.int32) // TOP_K, unique_indices=True, mode="promise_in_bounds")
    tile_start = jnp.arange(n_tiles, dtype=jnp.int32) * tm
    te = jnp.sum((tile_start[:, None] >= ends[None, :]).astype(jnp.int32), axis=1)
    tile_expert = jnp.minimum(te, n_exp - 1)
    rows_in_tile = (ends - padded + counts)[tile_expert] - tile_start
    tile_active = jnp.where(te >= n_exp, 0, jnp.where(rows_in_tile <= tm // 2, 2, 1)).astype(jnp.int32)
    ys = _experts(hn, src_tok, wg, wu, wd, tile_expert, tile_active, tm)
    pos2 = pos.reshape(n, TOP_K)
    y0 = jnp.take(ys, pos2[:, 0], axis=0, mode="clip")
    y1 = jnp.take(ys, pos2[:, 1], axis=0, mode="clip")
    return _combine(x_p, y0, y1, rgate_p, 0), _combine(x_s, y0, y1, rgate_s, n_p)


def kernel(x_prompt, x_sample, cache_attn_k, cache_attn_v, state_ret, ln_mix, w_in, q_norm, k_norm, attn_sinks,
           ret_gn_g, ret_gn_b, w_out, ln_ffn, ffn_w_gate, ffn_w_up, ffn_w_down, moe_router, moe_w_gate, moe_w_up,
           moe_w_down):
    batch, seq, d = x_prompt.shape
    dbatch, t_s, _ = x_sample.shape
    depth = ln_mix.shape[0]
    n_kv = cache_attn_k.shape[3]
    n_heads = state_ret.shape[2]
    d_kv = n_kv * HEAD_DIM
    d_ret = n_heads * RET_HEAD_DIM
    d_attn = w_out.shape[1] - d_ret
    n_q = d_attn // HEAD_DIM
    assert cache_attn_k.shape[4] == HEAD_DIM and state_ret.shape[3] == RET_HEAD_DIM and n_kv % 2 == 0
    assert seq % CHUNK == 0 and WINDOW % CHUNK == 0 and seq >= WINDOW and cache_attn_k.shape[2] == WINDOW
    assert w_in.shape[2] == d_attn + 2 * d_kv + 4 * d_ret
    n_p, n_s = batch * seq, dbatch * t_s
    keep = min(WINDOW, seq)

    perm_q, perm_k, perm_ao, sink_heads = _layout_perms(n_q, n_kv)
    inv_k = np.argsort(perm_k)
    qk_perm = np.concatenate([perm_q, d_attn + perm_k])

    tabs_p = _rope_tables(jnp.arange(seq))
    tabs_s = _rope_tables(PAST_LEN + jnp.tile(jnp.arange(t_s), dbatch))
    lane = np.arange(LANES)
    m128 = jnp.asarray(((lane[:, None] // HALF) % 2) == ((lane[None, :] // HALF) % 2), BF16)
    head_gain = lambda g: jnp.tile(g.astype(F32).reshape(2, HALF), (1, 2)).reshape(1, LANES)

    x_p, x_s = x_prompt.reshape(n_p, d), x_sample.reshape(n_s, d)
    nk_p, nv_p, nr_p, nk_s, nv_s, nr_s = [], [], [], [], [], []
    for l in range(depth):
        w_in_l = jnp.concatenate([w_in[l][:, :d_attn + d_kv][:, qk_perm], w_in[l][:, d_attn + d_kv:]], axis=1)
        w_in_l = w_in_l.astype(BF16).reshape(d, -1, COL_TILE).transpose(1, 0, 2)
        w_out_l = jnp.concatenate([w_out[l][:d_attn][perm_ao], w_out[l][d_attn:]], axis=0).astype(BF16)
        ln, gq, gk = ln_mix[l][None, :], head_gain(q_norm[l]), head_gain(k_norm[l])
        gng, gnb = ret_gn_g[l][None, :], ret_gn_b[l][None, :]
        proj_p = _proj(x_p, ln, w_in_l, m128, gq, gk, *tabs_p, d_attn, d_kv, d_ret, 512)
        proj_s = _proj(x_s, ln, w_in_l, m128, gq, gk, *tabs_s, d_attn, d_kv, d_ret, 256)
        merged_p, r_p = _mixer_prompt(proj_p, attn_sinks[l], sink_heads, gng, gnb, batch, seq, n_kv, n_q, n_heads)
        ck = cache_attn_k[l].reshape(dbatch, WINDOW, d_kv)[:, :, perm_k].astype(BF16)
        cv = cache_attn_v[l].reshape(dbatch, WINDOW, d_kv).astype(BF16)
        merged_s, r_s = _mixer_sample(proj_s, ck, cv, state_ret[l], attn_sinks[l], sink_heads, gng, gnb,
                                      0, dbatch, t_s, n_kv, n_q, n_heads)
        x_p = _outproj(merged_p, w_out_l, x_p)
        x_s = _outproj(merged_s, w_out_l, x_s)

        ka_last = proj_p[1].reshape(batch, seq, d_kv)[:, seq - keep:]
        nk_p.append(ka_last[:, :, inv_k].reshape(batch, keep, n_kv, HEAD_DIM))
        nv_p.append(proj_p[2].reshape(batch, seq, d_kv)[:, seq - keep:].reshape(batch, keep, n_kv, HEAD_DIM))
        nk_s.append(proj_s[1][:, inv_k].reshape(dbatch, t_s, n_kv, HEAD_DIM))
        nv_s.append(proj_s[2].reshape(dbatch, t_s, n_kv, HEAD_DIM))
        nr_p.append(r_p)
        nr_s.append(r_s)

        i = l // 2
        lf = ln_ffn[l][None, :]
        if l % 2 == 0:
            fpad = (-ffn_w_gate.shape[2]) % COL_TILE
            wg = jnp.pad(ffn_w_gate[i].astype(BF16), ((0, 0), (0, fpad)))
            wu = jnp.pad(ffn_w_up[i].astype(BF16), ((0, 0), (0, fpad)))
            wd = jnp.pad(ffn_w_down[i].astype(BF16), ((0, fpad), (0, 0)))
            x_p, x_s = _ffn(x_p, lf, wg, wu, wd), _ffn(x_s, lf, wg, wu, wd)
        else:
            x_p, x_s = _moe(x_p, x_s, lf, moe_router[i], moe_w_gate[i], moe_w_up[i], moe_w_down[i])

    return (x_p.reshape(batch, seq, d), x_s.reshape(dbatch, t_s, d),
            jnp.stack(nk_p), jnp.stack(nv_p), jnp.stack(nr_p), jnp.stack(nk_s), jnp.stack(nv_s), jnp.stack(nr_s))
```
